```python
import math
import jax, jax.numpy as jnp
from jax import lax
import numpy as np

D_MODEL = 2048
BATCH = 8
SEQ = 8192
DEPTH = 4

CHUNK = 64
Q_BLOCK = 128
EPS = 1e-6
NEG_INF = -1e30

FOX_HEADS = 8
FOX_HEAD_DIM = D_MODEL // 16
FOX_WIDTH = FOX_HEADS * FOX_HEAD_DIM
SSM_WIDTH = D_MODEL // 2
SSM_HEAD_DIM = 64
SSM_HEADS = SSM_WIDTH // SSM_HEAD_DIM
SSM_GROUPS = 2
SSM_STATE = 128
SSM_CONV = 4
SSM_CONV_DIM = SSM_WIDTH + 2 * SSM_GROUPS * SSM_STATE
SC_WIDTH = D_MODEL // 2
SC_CONV = 3
CF_WIDTH = D_MODEL // 2
CF_CONV = 31
N_BRANCH = 4
BRANCH_WIDTH = D_MODEL // 2

IN_SIZES = (
    FOX_WIDTH, FOX_WIDTH, FOX_WIDTH, FOX_HEADS, FOX_WIDTH,
    SSM_WIDTH, SSM_CONV_DIM, SSM_HEADS,
    SC_WIDTH, SC_WIDTH, SC_WIDTH, SC_WIDTH,
    2 * CF_WIDTH, CF_WIDTH,
)
N_IN = sum(IN_SIZES)

kernel_name = "hybrid_fox_ssd_shortconv_conformer_block"


def _split_columns(u, sizes):
    parts, off = [], 0
    for s in sizes:
        parts.append(u[..., off:off + s])
        off += s
    return parts


def rms_norm(x, w):
    xf = x.astype(jnp.float32)
    y = xf * lax.rsqrt(jnp.mean(xf * xf, axis=-1, keepdims=True) + EPS)
    return (y * w.astype(jnp.float32)).astype(x.dtype)


def layer_norm(x, w, b):
    xf = x.astype(jnp.float32)
    mu = jnp.mean(xf, axis=-1, keepdims=True)
    var = jnp.mean(jnp.square(xf - mu), axis=-1, keepdims=True)
    return ((xf - mu) * lax.rsqrt(var + EPS) * w + b).astype(x.dtype)


def causal_dwconv(x, w, b):
    k = w.shape[0]
    y = lax.conv_general_dilated(
        x, w[:, None, :].astype(x.dtype), window_strides=(1,), padding=((k - 1, 0),),
        dimension_numbers=("NWC", "WIO", "NWC"), feature_group_count=x.shape[-1])
    return y + b.astype(x.dtype)


def forgetting_attention(q, k, v, logf):
    bsz, s_len, n_h, dh = q.shape
    nb = s_len // Q_BLOCK
    scale = dh ** -0.5
    c = jnp.cumsum(logf, axis=1)
    qb = q.reshape(bsz, nb, Q_BLOCK, n_h, dh).transpose(1, 0, 3, 2, 4)
    cq = c.reshape(bsz, nb, Q_BLOCK, n_h).transpose(1, 0, 3, 2)
    kt = k.transpose(0, 2, 1, 3)
    vt = v.transpose(0, 2, 1, 3)
    ck = c.transpose(0, 2, 1)
    kpos = jnp.arange(s_len)

    def block(args):
        q_i, c_i, i = args
        s = jnp.einsum("bhqd,bhkd->bhqk", q_i, kt,
                       preferred_element_type=jnp.float32) * scale
        s = s + c_i[..., None] - ck[:, :, None, :]
        qpos = i * Q_BLOCK + jnp.arange(Q_BLOCK)
        s = jnp.where(kpos[None, :] <= qpos[:, None], s, NEG_INF)
        p = jax.nn.softmax(s, axis=-1)
        return jnp.einsum("bhqk,bhkd->bhqd", p.astype(vt.dtype), vt)

    o = lax.map(block, (qb, cq, jnp.arange(nb)))
    return o.transpose(1, 0, 3, 2, 4).reshape(bsz, s_len, n_h * dh)


def ssd(xh, dt, a, bm, cm, d_skip):
    bsz, s_len, n_h, p_dim = xh.shape
    g, n = bm.shape[2], bm.shape[3]
    r = n_h // g
    nc = s_len // CHUNK
    x = (xh * dt[..., None]).reshape(bsz, nc, CHUNK, g, r, p_dim)
    da = (dt * a).reshape(bsz, nc, CHUNK, g, r)
    bc = bm.reshape(bsz, nc, CHUNK, g, n)
    cc = cm.reshape(bsz, nc, CHUNK, g, n)
    cs = jnp.cumsum(da, axis=2)
    seg = cs[:, :, :, None] - cs[:, :, None, :]
    causal = jnp.tril(jnp.ones((CHUNK, CHUNK), dtype=bool))
    lmat = jnp.exp(jnp.where(causal[:, :, None, None], seg, -jnp.inf))
    cb = jnp.einsum("bclgn,bcsgn->bclsg", cc, bc)
    y_diag = jnp.einsum("bclsg,bclsgr,bcsgrp->bclgrp", cb, lmat, x)
    decay = jnp.exp(cs[:, :, -1:] - cs)
    states = jnp.einsum("bclgn,bclgr,bclgrp->bcgrpn", bc, decay, x)
    chunk_decay = jnp.exp(cs[:, :, -1])

    def step(h, inp):
        s_c, a_c = inp
        return h * a_c[..., None, None] + s_c, h

    h0 = jnp.zeros((bsz, g, r, p_dim, n), dtype=states.dtype)
    _, prev = lax.scan(step, h0, (jnp.moveaxis(states, 1, 0),
                                  jnp.moveaxis(chunk_decay, 1, 0).astype(states.dtype)))
    prev = jnp.moveaxis(prev, 0, 1)
    y_off = jnp.einsum("bclgn,bcgrpn,bclgr->bclgrp", cc, prev, jnp.exp(cs))
    y = (y_diag + y_off).reshape(bsz, s_len, n_h, p_dim) + xh * d_skip[:, None]
    return y.reshape(bsz, s_len, n_h * p_dim).astype(xh.dtype)


def hybrid_layer(x, norm_w, w_in, fg_bias, ssm_conv_w, ssm_conv_b, dt_bias, a_log, d_skip,
                 ssm_norm_w, sc_conv_w, sc_conv_b, cf_conv_w, cf_conv_b, cf_ln_w, cf_ln_b,
                 w_gate, b_gate, w_branch, w_out):
    bsz, s_len, _ = x.shape
    h = rms_norm(x, norm_w)
    u = h @ w_in
    (q, k, v, f_raw, g_a, z, xbc, dt_raw,
     sc_b, sc_c, sc_x, g_c, glu, g_d) = _split_columns(u, IN_SIZES)

    logf = jax.nn.log_sigmoid((f_raw + fg_bias).astype(jnp.float32))
    heads = lambda t: t.reshape(bsz, s_len, FOX_HEADS, FOX_HEAD_DIM)
    y_a = forgetting_attention(heads(q), heads(k), heads(v), logf) * jax.nn.silu(g_a)

    xbc = jax.nn.silu(causal_dwconv(xbc, ssm_conv_w, ssm_conv_b))
    xs, bm, cm = _split_columns(xbc, (SSM_WIDTH, SSM_GROUPS * SSM_STATE, SSM_GROUPS * SSM_STATE))
    dt = jax.nn.softplus((dt_raw + dt_bias).astype(jnp.float32))
    a = -jnp.exp(a_log.astype(jnp.float32))
    y = ssd(xs.reshape(bsz, s_len, SSM_HEADS, SSM_HEAD_DIM), dt, a,
            bm.reshape(bsz, s_len, SSM_GROUPS, SSM_STATE),
            cm.reshape(bsz, s_len, SSM_GROUPS, SSM_STATE), d_skip)
    y_b = rms_norm(y * jax.nn.silu(z), ssm_norm_w)

    y_c = sc_b * causal_dwconv(sc_c * sc_x, sc_conv_w, sc_conv_b) * jax.nn.silu(g_c)

    glu_a, glu_g = _split_columns(glu, (CF_WIDTH, CF_WIDTH))
    cf = causal_dwconv(glu_a * jax.nn.sigmoid(glu_g), cf_conv_w, cf_conv_b)
    y_d = jax.nn.silu(layer_norm(cf, cf_ln_w, cf_ln_b)) * jax.nn.silu(g_d)

    merged = None
    for i, y_i in enumerate((y_a, y_b, y_c, y_d)):
        gate = jax.nn.sigmoid(h @ w_gate[i] + b_gate[i])
        term = gate * (y_i @ w_branch[i])
        merged = term if merged is None else merged + term
    return x + merged @ w_out


def _fwd_setup_inputs(seed: int = 0) -> dict:
    key = jax.random.key(seed)
    ks = jax.random.split(key, 24)
    f32 = jnp.float32
    L = DEPTH

    def nrm(k, shape, scale):
        return jax.random.normal(k, shape, f32) * scale

    dt0 = jnp.exp(jax.random.uniform(ks[6], (L, SSM_HEADS), f32, math.log(1e-3), math.log(1e-1)))
    return {
        "x": nrm(ks[0], (BATCH, SEQ, D_MODEL), 1.0),
        "norm_w": 1.0 + nrm(ks[1], (L, D_MODEL), 0.02),
        "w_in": nrm(ks[2], (L, D_MODEL, N_IN), D_MODEL ** -0.5),
        "fg_bias": jax.random.uniform(ks[3], (L, FOX_HEADS), f32, 1.0, 6.0),
        "ssm_conv_w": nrm(ks[4], (L, SSM_CONV, SSM_CONV_DIM), SSM_CONV ** -0.5),
        "ssm_conv_b": nrm(ks[5], (L, SSM_CONV_DIM), 0.02),
        "dt_bias": dt0 + jnp.log(-jnp.expm1(-dt0)),
        "a_log": jnp.log(jax.random.uniform(ks[7], (L, SSM_HEADS), f32, 1.0, 16.0)),
        "d_skip": 1.0 + nrm(ks[8], (L, SSM_HEADS), 0.02),
        "ssm_norm_w": 1.0 + nrm(ks[9], (L, SSM_WIDTH), 0.02),
        "sc_conv_w": nrm(ks[10], (L, SC_CONV, SC_WIDTH), SC_CONV ** -0.5),
        "sc_conv_b": nrm(ks[11], (L, SC_WIDTH), 0.02),
        "cf_conv_w": nrm(ks[12], (L, CF_CONV, CF_WIDTH), CF_CONV ** -0.5),
        "cf_conv_b": nrm(ks[13], (L, CF_WIDTH), 0.02),
        "cf_ln_w": 1.0 + nrm(ks[14], (L, CF_WIDTH), 0.02),
        "cf_ln_b": nrm(ks[15], (L, CF_WIDTH), 0.02),
        "w_gate": nrm(ks[16], (L, N_BRANCH, D_MODEL, D_MODEL), D_MODEL ** -0.5),
        "b_gate": nrm(ks[17], (L, N_BRANCH, D_MODEL), 0.02),
        "w_branch": nrm(ks[18], (L, N_BRANCH, BRANCH_WIDTH, D_MODEL), BRANCH_WIDTH ** -0.5),
        "w_out": nrm(ks[19], (L, D_MODEL, D_MODEL), D_MODEL ** -0.5),
        "final_norm_w": 1.0 + nrm(ks[20], (D_MODEL,), 0.02),
    }


def _fwd_reference(x, norm_w, w_in, fg_bias, ssm_conv_w, ssm_conv_b, dt_bias, a_log, d_skip,
              ssm_norm_w, sc_conv_w, sc_conv_b, cf_conv_w, cf_conv_b, cf_ln_w, cf_ln_b,
              w_gate, b_gate, w_branch, w_out, final_norm_w):
    for l in range(DEPTH):
        x = hybrid_layer(x, norm_w[l], w_in[l], fg_bias[l], ssm_conv_w[l], ssm_conv_b[l],
                         dt_bias[l], a_log[l], d_skip[l], ssm_norm_w[l], sc_conv_w[l],
                         sc_conv_b[l], cf_conv_w[l], cf_conv_b[l], cf_ln_w[l], cf_ln_b[l],
                         w_gate[l], b_gate[l], w_branch[l], w_out[l])
    return rms_norm(x, final_norm_w)


import jax as _jax
import jax.numpy as _jnp

TWIN_FORMAT = 'train_step'
FWD_PARAMS = ['x', 'norm_w', 'w_in', 'fg_bias', 'ssm_conv_w', 'ssm_conv_b', 'dt_bias', 'a_log', 'd_skip', 'ssm_norm_w', 'sc_conv_w', 'sc_conv_b', 'cf_conv_w', 'cf_conv_b', 'cf_ln_w', 'cf_ln_b', 'w_gate', 'b_gate', 'w_branch', 'w_out', 'final_norm_w']
TWIN_WEIGHTS = ['norm_w', 'w_in', 'fg_bias', 'ssm_conv_w', 'ssm_conv_b', 'dt_bias', 'a_log', 'd_skip', 'ssm_norm_w', 'sc_conv_w', 'sc_conv_b', 'cf_conv_w', 'cf_conv_b', 'cf_ln_w', 'cf_ln_b', 'w_gate', 'b_gate', 'w_branch', 'w_out', 'final_norm_w']
TWIN_DIFF_INPUT = 'x'
TWIN_INPUTS = ['x', 'norm_w', 'w_in', 'fg_bias', 'ssm_conv_w', 'ssm_conv_b', 'dt_bias', 'a_log', 'd_skip', 'ssm_norm_w', 'sc_conv_w', 'sc_conv_b', 'cf_conv_w', 'cf_conv_b', 'cf_ln_w', 'cf_ln_b', 'w_gate', 'b_gate', 'w_branch', 'w_out', 'final_norm_w', 'loss_target', 'm_norm_w', 'm_w_in', 'm_fg_bias', 'm_ssm_conv_w', 'm_ssm_conv_b', 'm_dt_bias', 'm_a_log', 'm_d_skip', 'm_ssm_norm_w', 'm_sc_conv_w', 'm_sc_conv_b', 'm_cf_conv_w', 'm_cf_conv_b', 'm_cf_ln_w', 'm_cf_ln_b', 'm_w_gate', 'm_b_gate', 'm_w_branch', 'm_w_out', 'm_final_norm_w', 'v_norm_w', 'v_w_in', 'v_fg_bias', 'v_ssm_conv_w', 'v_ssm_conv_b', 'v_dt_bias', 'v_a_log', 'v_d_skip', 'v_ssm_norm_w', 'v_sc_conv_w', 'v_sc_conv_b', 'v_cf_conv_w', 'v_cf_conv_b', 'v_cf_ln_w', 'v_cf_ln_b', 'v_w_gate', 'v_b_gate', 'v_w_branch', 'v_w_out', 'v_final_norm_w']
TWIN_OUTPUTS = ['loss', 'grad_x', 'grad_norm_w', 'grad_w_in', 'grad_fg_bias', 'grad_ssm_conv_w', 'grad_ssm_conv_b', 'grad_dt_bias', 'grad_a_log', 'grad_d_skip', 'grad_ssm_norm_w', 'grad_sc_conv_w', 'grad_sc_conv_b', 'grad_cf_conv_w', 'grad_cf_conv_b', 'grad_cf_ln_w', 'grad_cf_ln_b', 'grad_w_gate', 'grad_b_gate', 'grad_w_branch', 'grad_w_out', 'grad_final_norm_w', 'delta_norm_w', 'delta_w_in', 'delta_fg_bias', 'delta_ssm_conv_w', 'delta_ssm_conv_b', 'delta_dt_bias', 'delta_a_log', 'delta_d_skip', 'delta_ssm_norm_w', 'delta_sc_conv_w', 'delta_sc_conv_b', 'delta_cf_conv_w', 'delta_cf_conv_b', 'delta_cf_ln_w', 'delta_cf_ln_b', 'delta_w_gate', 'delta_b_gate', 'delta_w_branch', 'delta_w_out', 'delta_final_norm_w', 'new_m_norm_w', 'new_m_w_in', 'new_m_fg_bias', 'new_m_ssm_conv_w', 'new_m_ssm_conv_b', 'new_m_dt_bias', 'new_m_a_log', 'new_m_d_skip', 'new_m_ssm_norm_w', 'new_m_sc_conv_w', 'new_m_sc_conv_b', 'new_m_cf_conv_w', 'new_m_cf_conv_b', 'new_m_cf_ln_w', 'new_m_cf_ln_b', 'new_m_w_gate', 'new_m_b_gate', 'new_m_w_branch', 'new_m_w_out', 'new_m_final_norm_w', 'new_v_norm_w', 'new_v_w_in', 'new_v_fg_bias', 'new_v_ssm_conv_w', 'new_v_ssm_conv_b', 'new_v_dt_bias', 'new_v_a_log', 'new_v_d_skip', 'new_v_ssm_norm_w', 'new_v_sc_conv_w', 'new_v_sc_conv_b', 'new_v_cf_conv_w', 'new_v_cf_conv_b', 'new_v_cf_ln_w', 'new_v_cf_ln_b', 'new_v_w_gate', 'new_v_b_gate', 'new_v_w_branch', 'new_v_w_out', 'new_v_final_norm_w']
TWIN_LEAF_KINDS = {'loss': 'loss', 'grad_x': 'grad_x', 'grad_norm_w': 'grad_w', 'grad_w_in': 'grad_w', 'grad_fg_bias': 'grad_w', 'grad_ssm_conv_w': 'grad_w', 'grad_ssm_conv_b': 'grad_w', 'grad_dt_bias': 'grad_w', 'grad_a_log': 'grad_w', 'grad_d_skip': 'grad_w', 'grad_ssm_norm_w': 'grad_w', 'grad_sc_conv_w': 'grad_w', 'grad_sc_conv_b': 'grad_w', 'grad_cf_conv_w': 'grad_w', 'grad_cf_conv_b': 'grad_w', 'grad_cf_ln_w': 'grad_w', 'grad_cf_ln_b': 'grad_w', 'grad_w_gate': 'grad_w', 'grad_b_gate': 'grad_w', 'grad_w_branch': 'grad_w', 'grad_w_out': 'grad_w', 'grad_final_norm_w': 'grad_w', 'delta_norm_w': 'delta_w', 'delta_w_in': 'delta_w', 'delta_fg_bias': 'delta_w', 'delta_ssm_conv_w': 'delta_w', 'delta_ssm_conv_b': 'delta_w', 'delta_dt_bias': 'delta_w', 'delta_a_log': 'delta_w', 'delta_d_skip': 'delta_w', 'delta_ssm_norm_w': 'delta_w', 'delta_sc_conv_w': 'delta_w', 'delta_sc_conv_b': 'delta_w', 'delta_cf_conv_w': 'delta_w', 'delta_cf_conv_b': 'delta_w', 'delta_cf_ln_w': 'delta_w', 'delta_cf_ln_b': 'delta_w', 'delta_w_gate': 'delta_w', 'delta_b_gate': 'delta_w', 'delta_w_branch': 'delta_w', 'delta_w_out': 'delta_w', 'delta_final_norm_w': 'delta_w', 'new_m_norm_w': 'new_m', 'new_m_w_in': 'new_m', 'new_m_fg_bias': 'new_m', 'new_m_ssm_conv_w': 'new_m', 'new_m_ssm_conv_b': 'new_m', 'new_m_dt_bias': 'new_m', 'new_m_a_log': 'new_m', 'new_m_d_skip': 'new_m', 'new_m_ssm_norm_w': 'new_m', 'new_m_sc_conv_w': 'new_m', 'new_m_sc_conv_b': 'new_m', 'new_m_cf_conv_w': 'new_m', 'new_m_cf_conv_b': 'new_m', 'new_m_cf_ln_w': 'new_m', 'new_m_cf_ln_b': 'new_m', 'new_m_w_gate': 'new_m', 'new_m_b_gate': 'new_m', 'new_m_w_branch': 'new_m', 'new_m_w_out': 'new_m', 'new_m_final_norm_w': 'new_m', 'new_v_norm_w': 'new_v', 'new_v_w_in': 'new_v', 'new_v_fg_bias': 'new_v', 'new_v_ssm_conv_w': 'new_v', 'new_v_ssm_conv_b': 'new_v', 'new_v_dt_bias': 'new_v', 'new_v_a_log': 'new_v', 'new_v_d_skip': 'new_v', 'new_v_ssm_norm_w': 'new_v', 'new_v_sc_conv_w': 'new_v', 'new_v_sc_conv_b': 'new_v', 'new_v_cf_conv_w': 'new_v', 'new_v_cf_conv_b': 'new_v', 'new_v_cf_ln_w': 'new_v', 'new_v_cf_ln_b': 'new_v', 'new_v_w_gate': 'new_v', 'new_v_b_gate': 'new_v', 'new_v_w_branch': 'new_v', 'new_v_w_out': 'new_v', 'new_v_final_norm_w': 'new_v'}


def _forward(args):
    return _fwd_reference(*[args[k] for k in FWD_PARAMS])


def _output_shape():
    def fwd():
        inp = _fwd_setup_inputs(0)
        return _fwd_reference(*[inp[k] for k in FWD_PARAMS])
    out = _jax.eval_shape(fwd)
    return out.shape, out.dtype

N_MICROBATCH = 1
ADAM_LR = 0.001
ADAM_B1 = 0.9
ADAM_B2 = 0.999
ADAM_EPS = 1e-08
ADAM_WD = 0.01
ADAM_STEP = 10
PER_EXAMPLE_BATCH_AXIS = {'x': 0, 'loss_target': 0}
SHARED_INPUTS = []
_WEIGHT_DTYPES = {'norm_w': _jnp.float32, 'w_in': _jnp.float32, 'fg_bias': _jnp.float32, 'ssm_conv_w': _jnp.float32, 'ssm_conv_b': _jnp.float32, 'dt_bias': _jnp.float32, 'a_log': _jnp.float32, 'd_skip': _jnp.float32, 'ssm_norm_w': _jnp.float32, 'sc_conv_w': _jnp.float32, 'sc_conv_b': _jnp.float32, 'cf_conv_w': _jnp.float32, 'cf_conv_b': _jnp.float32, 'cf_ln_w': _jnp.float32, 'cf_ln_b': _jnp.float32, 'w_gate': _jnp.float32, 'b_gate': _jnp.float32, 'w_branch': _jnp.float32, 'w_out': _jnp.float32, 'final_norm_w': _jnp.float32}
MOMENT_SCALE = {'norm_w': 1.345201e-01, 'w_in': 5.083464e-02, 'fg_bias': 1.416937e-01, 'ssm_conv_w': 7.964517e-02, 'ssm_conv_b': 1.061394e-01, 'dt_bias': 2.217053e-01, 'a_log': 2.942780e-01, 'd_skip': 4.323032e-01, 'ssm_norm_w': 9.117744e-02, 'sc_conv_w': 5.454305e-02, 'sc_conv_b': 5.496285e-02, 'cf_conv_w': 3.375361e-02, 'cf_conv_b': 7.266077e-02, 'cf_ln_w': 3.978957e-02, 'cf_ln_b': 3.359488e-02, 'w_gate': 1.555643e-02, 'b_gate': 1.578231e-02, 'w_branch': 3.985569e-02, 'w_out': 7.976383e-02, 'final_norm_w': 3.202349e+01}


def _to_microbatches(a, axis):
    t = _jnp.moveaxis(a, axis, 0)
    t = t.reshape((N_MICROBATCH, t.shape[0] // N_MICROBATCH) + t.shape[1:])
    return _jnp.moveaxis(t, 1, axis + 1)


def setup_inputs(seed: int = 0) -> dict:
    inp = _fwd_setup_inputs(seed)
    key = _jax.random.fold_in(_jax.random.key(seed), 7919)
    shape, _ = _output_shape()
    out = dict(inp)
    out["loss_target"] = _jax.random.normal(_jax.random.fold_in(key, 0), shape, _jnp.float32)
    for i, name in enumerate(TWIN_WEIGHTS):
        w = inp[name].astype(_jnp.float32)
        if MOMENT_SCALE is None:
            s = _jnp.sqrt(_jnp.mean(_jnp.square(w)) + 1e-30)
        else:
            s = MOMENT_SCALE[name]
        km, kv = _jax.random.split(_jax.random.fold_in(key, i + 1))
        out[name] = w
        out["m_" + name] = s * _jax.random.normal(km, w.shape, _jnp.float32)
        out["v_" + name] = (s * s) * _jax.random.uniform(kv, w.shape, _jnp.float32, 0.5, 1.5)
    if N_MICROBATCH > 1:
        for name, axis in PER_EXAMPLE_BATCH_AXIS.items():
            out[name] = _to_microbatches(out[name], axis)
    return {'x': out['x'], 'norm_w': out['norm_w'], 'w_in': out['w_in'], 'fg_bias': out['fg_bias'], 'ssm_conv_w': out['ssm_conv_w'], 'ssm_conv_b': out['ssm_conv_b'], 'dt_bias': out['dt_bias'], 'a_log': out['a_log'], 'd_skip': out['d_skip'], 'ssm_norm_w': out['ssm_norm_w'], 'sc_conv_w': out['sc_conv_w'], 'sc_conv_b': out['sc_conv_b'], 'cf_conv_w': out['cf_conv_w'], 'cf_conv_b': out['cf_conv_b'], 'cf_ln_w': out['cf_ln_w'], 'cf_ln_b': out['cf_ln_b'], 'w_gate': out['w_gate'], 'b_gate': out['b_gate'], 'w_branch': out['w_branch'], 'w_out': out['w_out'], 'final_norm_w': out['final_norm_w'], 'loss_target': out['loss_target'], 'm_norm_w': out['m_norm_w'], 'm_w_in': out['m_w_in'], 'm_fg_bias': out['m_fg_bias'], 'm_ssm_conv_w': out['m_ssm_conv_w'], 'm_ssm_conv_b': out['m_ssm_conv_b'], 'm_dt_bias': out['m_dt_bias'], 'm_a_log': out['m_a_log'], 'm_d_skip': out['m_d_skip'], 'm_ssm_norm_w': out['m_ssm_norm_w'], 'm_sc_conv_w': out['m_sc_conv_w'], 'm_sc_conv_b': out['m_sc_conv_b'], 'm_cf_conv_w': out['m_cf_conv_w'], 'm_cf_conv_b': out['m_cf_conv_b'], 'm_cf_ln_w': out['m_cf_ln_w'], 'm_cf_ln_b': out['m_cf_ln_b'], 'm_w_gate': out['m_w_gate'], 'm_b_gate': out['m_b_gate'], 'm_w_branch': out['m_w_branch'], 'm_w_out': out['m_w_out'], 'm_final_norm_w': out['m_final_norm_w'], 'v_norm_w': out['v_norm_w'], 'v_w_in': out['v_w_in'], 'v_fg_bias': out['v_fg_bias'], 'v_ssm_conv_w': out['v_ssm_conv_w'], 'v_ssm_conv_b': out['v_ssm_conv_b'], 'v_dt_bias': out['v_dt_bias'], 'v_a_log': out['v_a_log'], 'v_d_skip': out['v_d_skip'], 'v_ssm_norm_w': out['v_ssm_norm_w'], 'v_sc_conv_w': out['v_sc_conv_w'], 'v_sc_conv_b': out['v_sc_conv_b'], 'v_cf_conv_w': out['v_cf_conv_w'], 'v_cf_conv_b': out['v_cf_conv_b'], 'v_cf_ln_w': out['v_cf_ln_w'], 'v_cf_ln_b': out['v_cf_ln_b'], 'v_w_gate': out['v_w_gate'], 'v_b_gate': out['v_b_gate'], 'v_w_branch': out['v_w_branch'], 'v_w_out': out['v_w_out'], 'v_final_norm_w': out['v_final_norm_w']}


def _loss(weights, diff, rest, loss_target):
    with _jax.named_scope("forward"):
        args = {**rest, TWIN_DIFF_INPUT: diff, **{k: w.astype(_WEIGHT_DTYPES[k]) for k, w in weights.items()}}
        y = _forward(args)
    with _jax.named_scope("loss_head"):
        err = _jnp.square(y.astype(_jnp.float32) - loss_target)
        return 0.5 * _jnp.sum(_jnp.mean(err, axis=-1)) if err.ndim else 0.5 * err


def _adamw(w, g, m, v):
    m = ADAM_B1 * m + (1.0 - ADAM_B1) * g
    v = ADAM_B2 * v + (1.0 - ADAM_B2) * _jnp.square(g)
    m_hat = m / (1.0 - ADAM_B1 ** ADAM_STEP)
    v_hat = v / (1.0 - ADAM_B2 ** ADAM_STEP)
    delta = -ADAM_LR * (m_hat / (_jnp.sqrt(v_hat) + ADAM_EPS) + ADAM_WD * w)
    return delta, m, v


def reference(x, norm_w, w_in, fg_bias, ssm_conv_w, ssm_conv_b, dt_bias, a_log, d_skip, ssm_norm_w, sc_conv_w, sc_conv_b, cf_conv_w, cf_conv_b, cf_ln_w, cf_ln_b, w_gate, b_gate, w_branch, w_out, final_norm_w, loss_target, m_norm_w, m_w_in, m_fg_bias, m_ssm_conv_w, m_ssm_conv_b, m_dt_bias, m_a_log, m_d_skip, m_ssm_norm_w, m_sc_conv_w, m_sc_conv_b, m_cf_conv_w, m_cf_conv_b, m_cf_ln_w, m_cf_ln_b, m_w_gate, m_b_gate, m_w_branch, m_w_out, m_final_norm_w, v_norm_w, v_w_in, v_fg_bias, v_ssm_conv_w, v_ssm_conv_b, v_dt_bias, v_a_log, v_d_skip, v_ssm_norm_w, v_sc_conv_w, v_sc_conv_b, v_cf_conv_w, v_cf_conv_b, v_cf_ln_w, v_cf_ln_b, v_w_gate, v_b_gate, v_w_branch, v_w_out, v_final_norm_w):
    given = dict(x=x, norm_w=norm_w, w_in=w_in, fg_bias=fg_bias, ssm_conv_w=ssm_conv_w, ssm_conv_b=ssm_conv_b, dt_bias=dt_bias, a_log=a_log, d_skip=d_skip, ssm_norm_w=ssm_norm_w, sc_conv_w=sc_conv_w, sc_conv_b=sc_conv_b, cf_conv_w=cf_conv_w, cf_conv_b=cf_conv_b, cf_ln_w=cf_ln_w, cf_ln_b=cf_ln_b, w_gate=w_gate, b_gate=b_gate, w_branch=w_branch, w_out=w_out, final_norm_w=final_norm_w, loss_target=loss_target, m_norm_w=m_norm_w, m_w_in=m_w_in, m_fg_bias=m_fg_bias, m_ssm_conv_w=m_ssm_conv_w, m_ssm_conv_b=m_ssm_conv_b, m_dt_bias=m_dt_bias, m_a_log=m_a_log, m_d_skip=m_d_skip, m_ssm_norm_w=m_ssm_norm_w, m_sc_conv_w=m_sc_conv_w, m_sc_conv_b=m_sc_conv_b, m_cf_conv_w=m_cf_conv_w, m_cf_conv_b=m_cf_conv_b, m_cf_ln_w=m_cf_ln_w, m_cf_ln_b=m_cf_ln_b, m_w_gate=m_w_gate, m_b_gate=m_b_gate, m_w_branch=m_w_branch, m_w_out=m_w_out, m_final_norm_w=m_final_norm_w, v_norm_w=v_norm_w, v_w_in=v_w_in, v_fg_bias=v_fg_bias, v_ssm_conv_w=v_ssm_conv_w, v_ssm_conv_b=v_ssm_conv_b, v_dt_bias=v_dt_bias, v_a_log=v_a_log, v_d_skip=v_d_skip, v_ssm_norm_w=v_ssm_norm_w, v_sc_conv_w=v_sc_conv_w, v_sc_conv_b=v_sc_conv_b, v_cf_conv_w=v_cf_conv_w, v_cf_conv_b=v_cf_conv_b, v_cf_ln_w=v_cf_ln_w, v_cf_ln_b=v_cf_ln_b, v_w_gate=v_w_gate, v_b_gate=v_b_gate, v_w_branch=v_w_branch, v_w_out=v_w_out, v_final_norm_w=v_final_norm_w)
    weights = {n: given[n] for n in TWIN_WEIGHTS}
    shared = {n: given[n] for n in SHARED_INPUTS}
    per_example = {n: given[n] for n in ['x']}
    grad_fn = _jax.value_and_grad(_loss, argnums=(0, 1))

    def one_microbatch(ex, loss_target):
        ex = dict(ex)
        diff = ex.pop(TWIN_DIFF_INPUT)
        return grad_fn(weights, diff, {**shared, **ex}, loss_target)

    if N_MICROBATCH == 1:
        loss, (grad_w, grad_x) = one_microbatch(per_example, given["loss_target"])
    else:
        def body(carry, xs):
            loss_sum, grad_sum = carry
            l_k, (gw_k, gx_k) = one_microbatch(xs[0], xs[1])
            with _jax.named_scope("update"):
                return (loss_sum + l_k, _jax.tree.map(_jnp.add, grad_sum, gw_k)), gx_k

        init = (_jnp.zeros((), _jnp.float32), _jax.tree.map(_jnp.zeros_like, weights))
        (loss, grad_w), grad_x = _jax.lax.scan(body, init, (per_example, given["loss_target"]))
    with _jax.named_scope("update"):
        delta_w, new_m, new_v = {}, {}, {}
        for n in TWIN_WEIGHTS:
            delta_w[n], new_m[n], new_v[n] = _adamw(weights[n], grad_w[n], given["m_" + n], given["v_" + n])
    return (loss, grad_x, *[grad_w[n] for n in TWIN_WEIGHTS], *[delta_w[n] for n in TWIN_WEIGHTS],
            *[new_m[n] for n in TWIN_WEIGHTS], *[new_v[n] for n in TWIN_WEIGHTS])
```

```python
import functools

import jax
import jax.numpy as jnp
from jax import lax
from jax.experimental import pallas as pl
from jax.experimental.pallas import tpu as pltpu

f32 = jnp.float32
bf16 = jnp.bfloat16
SDS = jax.ShapeDtypeStruct

N_DEV = 8
LANES = 128
VMEM_LIMIT = 48 * 1024 * 1024
EPS = 1e-6
NEG = -1e30

D_MODEL = 2048
FOX_HEADS = 8
FOX_DH = 128
SSM_HEADS = 16
SSM_P = 64
SSM_N = 128
SSM_G = 2
SSM_W = 1024
XBC_W = 1536
SSM_K, SC_K, CF_K = 4, 3, 31
BW = 1024
SSD_L = 128
HL = 8
CONV_HALO = 32

ADAM_LR, ADAM_B1, ADAM_B2, ADAM_EPS, ADAM_WD, ADAM_STEP = 0.001, 0.9, 0.999, 1e-08, 0.01, 10

IN_NAMES = ("q", "k", "v", "f", "ga", "z", "xbc", "dt", "scb", "scc", "scx", "gc", "glua", "glug", "gd")
IN_SIZES = (1024, 1024, 1024, 8, 1024, 1024, 1536, 16, 1024, 1024, 1024, 1024, 1024, 1024, 1024)
N_IN = sum(IN_SIZES)
MAIN_NAMES = ("q", "k", "v", "ga", "z", "scb", "scc", "scx", "gc", "glua", "glug", "gd", "xbc")
MAIN_OFF = {n: 1024 * i for i, n in enumerate(MAIN_NAMES)}
N_MAIN = 12 * 1024 + XBC_W


def _pcall(body, **kw):
    return pl.pallas_call(body, **kw)


def _cp(*sem):
    return pltpu.CompilerParams(dimension_semantics=sem if sem else None, vmem_limit_bytes=VMEM_LIMIT)


def _pick(n, cands):
    for c in cands:
        if c <= n and n % c == 0:
            return c
    return n


_DIMS = {"nn": ((1,), (0,)), "nt": ((1,), (1,)), "tn": ((0,), (0,))}


def _mm(a, b, *, mode, out_dtype, name, add=None, tm=512, tn=None, tk=None):
    if mode == "nn":
        (M, K), (_, N) = a.shape, b.shape
    elif mode == "nt":
        (M, K), (N, _) = a.shape, b.shape
    else:
        (K, M), (_, N) = a.shape, b.shape
    tm = _pick(M, (tm, 256, 128))
    tn = _pick(N, (tn,) if tn else (1536, 1024, 512, 256, 128))
    tk = _pick(K, (tk,) if tk else (2048, 1536, 1024, 512, 256, 128))
    nk = K // tk
    a_spec = (pl.BlockSpec((tk, tm), lambda j, i, k: (k, i)) if mode == "tn"
              else pl.BlockSpec((tm, tk), lambda j, i, k: (i, k)))
    b_spec = (pl.BlockSpec((tn, tk), lambda j, i, k: (j, k)) if mode == "nt"
              else pl.BlockSpec((tk, tn), lambda j, i, k: (k, j)))
    o_spec = pl.BlockSpec((tm, tn), lambda j, i, k: (i, j))
    ins, specs = [a, b], [a_spec, b_spec]
    if add is not None:
        ins.append(add)
        specs.append(o_spec)
    dims = (_DIMS[mode], ((), ()))

    def body(*refs):
        a_ref, b_ref = refs[0], refs[1]
        add_ref = refs[2] if add is not None else None
        o_ref = refs[3] if add is not None else refs[2]
        p = lax.dot_general(a_ref[...], b_ref[...], dims, preferred_element_type=f32)

        def finish(v):
            if add_ref is not None:
                v = v + add_ref[...].astype(f32)
            o_ref[...] = v.astype(o_ref.dtype)

        if nk == 1:
            finish(p)
        else:
            acc = refs[-1]
            k = pl.program_id(2)

            @pl.when(k == 0)
            def _():
                acc[...] = p

            @pl.when(k > 0)
            def _():
                acc[...] += p

            @pl.when(k == nk - 1)
            def _():
                finish(acc[...])

    return _pcall(
        body, name=name, grid=(N // tn, M // tm, nk), in_specs=specs, out_specs=o_spec,
        out_shape=SDS((M, N), out_dtype),
        scratch_shapes=[pltpu.VMEM((tm, tn), f32)] if nk > 1 else [],
        compiler_params=_cp("parallel", "parallel", "arbitrary"),
    )(*ins)


def _silu(x):
    return x * jax.nn.sigmoid(x)


def _softplus(x):
    return jnp.maximum(x, 0.0) + jnp.log(1.0 + jnp.exp(-jnp.abs(x)))


def _mean(x):
    return jnp.mean(x, axis=-1, keepdims=True)


def _f_rms(x, w):
    return (x * lax.rsqrt(_mean(x * x) + EPS) * w,)


def _f_rms_res(x, w):
    return _f_rms(x, w) + (x,)


def _f_ya(o, g):
    return (o * _silu(g),)


def _f_yb(y, z, w):
    t = y * _silu(z)
    return (t * lax.rsqrt(_mean(t * t) + EPS) * w,)


def _f_mul(a, b):
    return (a * b,)


def _f_yc(b, cv, g):
    return (b * cv * _silu(g),)


def _f_glu(a, g):
    return (a * jax.nn.sigmoid(g),)


def _f_yd(cf, g, lw, lb):
    mu = _mean(cf)
    var = _mean(jnp.square(cf - mu))
    ln = (cf - mu) * lax.rsqrt(var + EPS) * lw + lb
    return (_silu(ln) * _silu(g),)


def _f_silu(x):
    return (_silu(x),)


def _f_merge(g0, g1, g2, g3, p0, p1, p2, p3, b0, b1, b2, b3):
    sg = jax.nn.sigmoid
    return (sg(g0 + b0) * p0 + sg(g1 + b1) * p1 + sg(g2 + b2) * p2 + sg(g3 + b3) * p3,)


def _row_specs(ins, params, tt, cw):
    specs = [pl.BlockSpec((tt, cw), functools.partial(lambda j, i, base: (i, base + j), base=base))
             for _, base in ins]
    specs += [pl.BlockSpec((p.shape[0], cw), lambda j, i: (0, j)) for p in params]
    return specs


def _rowwise(fn, ins, params, out_dtypes, *, cw, ncol, name, tt=256):
    T = ins[0][0].shape[0]
    tt = _pick(T, (tt, 128))
    ni, npar = len(ins), len(params)

    def body(*refs):
        xs = [r[...].astype(f32) for r in refs[:ni]]
        ps = [r[...] for r in refs[ni:ni + npar]]
        ys = fn(*xs, *ps)
        for o_ref, y in zip(refs[ni + npar:], ys):
            o_ref[...] = y.astype(o_ref.dtype)

    out_spec = pl.BlockSpec((tt, cw), lambda j, i: (i, j))
    outs = _pcall(
        body, name=name, grid=(ncol, T // tt), in_specs=_row_specs(ins, params, tt, cw),
        out_specs=[out_spec] * len(out_dtypes),
        out_shape=[SDS((T, cw * ncol), dt) for dt in out_dtypes],
        compiler_params=_cp("parallel", "parallel"),
    )(*[a for a, _ in ins], *params)
    return outs


def _rowwise_bwd(fn, ins, params, cots, din_dtypes, *, cw, ncol, name, tt=256):
    T = ins[0][0].shape[0]
    tt = _pick(T, (tt, 128))
    ni, npar, nc = len(ins), len(params), len(cots)
    keep = [k for k, dt in enumerate(din_dtypes) if dt is not None]

    def body(*refs):
        xs = [r[...].astype(f32) for r in refs[:ni]]
        ps = [r[...] for r in refs[ni:ni + npar]]
        cs = tuple(r[...].astype(f32) for r in refs[ni + npar:ni + npar + nc])
        outs = refs[ni + npar + nc:]
        _, vjp = jax.vjp(fn, *xs, *ps)
        g = vjp(cs)
        for o_ref, k in zip(outs[:len(keep)], keep):
            o_ref[...] = g[k].astype(o_ref.dtype)
        i = pl.program_id(1)
        for o_ref, gp in zip(outs[len(keep):], g[ni:]):
            @pl.when(i == 0)
            def _(o_ref=o_ref, gp=gp):
                o_ref[...] = gp

            @pl.when(i > 0)
            def _(o_ref=o_ref, gp=gp):
                o_ref[...] += gp

    row_spec = pl.BlockSpec((tt, cw), lambda j, i: (i, j))
    in_specs = _row_specs(ins, params, tt, cw) + [row_spec] * nc
    out_specs = [row_spec] * len(keep) + [pl.BlockSpec((p.shape[0], cw), lambda j, i: (0, j)) for p in params]
    out_shape = [SDS((T, cw * ncol), din_dtypes[k]) for k in keep] + [SDS(p.shape, f32) for p in params]
    return _pcall(
        body, name=name, grid=(ncol, T // tt), in_specs=in_specs, out_specs=out_specs, out_shape=out_shape,
        compiler_params=_cp("arbitrary", "arbitrary"),
    )(*[a for a, _ in ins], *params, *cots)


def _conv_fwd(x, xbase, C, w, b, *, with_silu, name, out_dtype=bf16, tt=512, cw=512):
    xa = x
    T = xa.shape[0]
    K = w.shape[0]
    tt = _pick(T, (tt,))
    hp = CONV_HALO
    r = tt // hp
    ncol = C // cw

    def body(cur_ref, halo_ref, w_ref, b_ref, *rest):
        outs, ext = rest[:-1], rest[-1]
        i = pl.program_id(1)
        ext[0:hp, :] = jnp.where(i > 0, halo_ref[...].astype(f32), 0.0)
        ext[hp:, :] = cur_ref[...].astype(f32)
        acc = jnp.zeros((tt, cw), f32) + b_ref[...]
        for k in range(K):
            acc = acc + w_ref[k:k + 1, :] * ext[pl.ds(hp - (K - 1) + k, tt), :]
        outs[0][...] = acc.astype(outs[0].dtype)
        if with_silu:
            outs[1][...] = _silu(acc).astype(outs[1].dtype)

    n_out = 2 if with_silu else 1
    o_spec = pl.BlockSpec((tt, cw), lambda j, i: (i, j))
    return _pcall(
        body, name=name, grid=(ncol, T // tt),
        in_specs=[pl.BlockSpec((tt, cw), lambda j, i: (i, xbase + j)),
                  pl.BlockSpec((hp, cw), lambda j, i: (jnp.maximum(i * r - 1, 0), xbase + j)),
                  pl.BlockSpec((K, cw), lambda j, i: (0, j)),
                  pl.BlockSpec((1, cw), lambda j, i: (0, j))],
        out_specs=[o_spec] * n_out, out_shape=[SDS((T, C), out_dtype)] * n_out,
        scratch_shapes=[pltpu.VMEM((hp + tt, cw), f32)],
        compiler_params=_cp("parallel", "parallel"),
    )(xa, xa, w, b)


def _conv_bwd(dy, x, xbase, C, w, *, name, dx_dtype=bf16, tt=512, cw=512):
    T = dy.shape[0]
    K = w.shape[0]
    tt = _pick(T, (tt,))
    hp = CONV_HALO
    r = tt // hp
    nt = T // tt
    ncol = C // cw

    def body(dy_ref, dyn_ref, x_ref, xh_ref, w_ref, dx_ref, dw_ref, db_ref, exd, exx):
        i = pl.program_id(1)
        dyc = dy_ref[...].astype(f32)
        exd[0:tt, :] = dyc
        exd[tt:, :] = jnp.where(i < nt - 1, dyn_ref[...].astype(f32), 0.0)
        exx[0:hp, :] = jnp.where(i > 0, xh_ref[...].astype(f32), 0.0)
        exx[hp:, :] = x_ref[...].astype(f32)

        @pl.when(i == 0)
        def _():
            dw_ref[...] = jnp.zeros_like(dw_ref)
            db_ref[...] = jnp.zeros_like(db_ref)

        acc = jnp.zeros((tt, cw), f32)
        for k in range(K):
            acc = acc + w_ref[k:k + 1, :] * exd[pl.ds(K - 1 - k, tt), :]
            dw_ref[k:k + 1, :] += jnp.sum(dyc * exx[pl.ds(hp - (K - 1) + k, tt), :], axis=0, keepdims=True)
        dx_ref[...] = acc.astype(dx_ref.dtype)
        db_ref[...] += jnp.sum(dyc, axis=0, keepdims=True)

    return _pcall(
        body, name=name, grid=(ncol, nt),
        in_specs=[pl.BlockSpec((tt, cw), lambda j, i: (i, j)),
                  pl.BlockSpec((hp, cw), lambda j, i: (jnp.minimum((i + 1) * r, nt * r - 1), j)),
                  pl.BlockSpec((tt, cw), lambda j, i: (i, xbase + j)),
                  pl.BlockSpec((hp, cw), lambda j, i: (jnp.maximum(i * r - 1, 0), xbase + j)),
                  pl.BlockSpec((K, cw), lambda j, i: (0, j))],
        out_specs=[pl.BlockSpec((tt, cw), lambda j, i: (i, j)),
                   pl.BlockSpec((K, cw), lambda j, i: (0, j)),
                   pl.BlockSpec((1, cw), lambda j, i: (0, j))],
        out_shape=[SDS((T, C), dx_dtype), SDS((K, C), f32), SDS((1, C), f32)],
        scratch_shapes=[pltpu.VMEM((tt + hp, cw), f32), pltpu.VMEM((hp + tt, cw), f32)],
        compiler_params=_cp("arbitrary", "arbitrary"),
    )(dy, dy, x, x, w)


def _tri(n, lower):
    r = lax.broadcasted_iota(jnp.int32, (n, n), 0)
    c = lax.broadcasted_iota(jnp.int32, (n, n), 1)
    return jnp.where((r >= c) if lower else (r <= c), 1.0, 0.0).astype(f32)


def _hdot(a, b):
    return jnp.dot(a, b, precision=lax.Precision.HIGHEST, preferred_element_type=f32)


def _fox_gate_fwd(small, fb_row, *, name, tt=512):
    T = small.shape[0]
    tt = _pick(T, (tt,))

    def body(sm_ref, fb_ref, c_ref, carry):
        @pl.when(pl.program_id(0) == 0)
        def _():
            carry[...] = jnp.zeros_like(carry)

        logf = -_softplus(-(sm_ref[...] + fb_ref[...]))
        cs = _hdot(_tri(tt, True), logf) + carry[...]
        c_ref[...] = cs
        carry[...] = cs[tt - 1:tt, :]

    blk = pl.BlockSpec((tt, LANES), lambda i: (i, 0))
    return _pcall(
        body, name=name, grid=(T // tt,), in_specs=[blk, pl.BlockSpec((1, LANES), lambda i: (0, 0))],
        out_specs=blk, out_shape=SDS((T, LANES), f32), scratch_shapes=[pltpu.VMEM((1, LANES), f32)],
        compiler_params=_cp("arbitrary"),
    )(small, fb_row)


def _fox_gate_bwd(dc, small, fb_row, dsmall_dt, *, name, tt=512):
    T = small.shape[0]
    tt = _pick(T, (tt,))
    nt = T // tt

    def body(dc_ref, sm_ref, fb_ref, dd_ref, ds_ref, dfb_ref, carry):
        @pl.when(pl.program_id(0) == 0)
        def _():
            carry[...] = jnp.zeros_like(carry)
            dfb_ref[...] = jnp.zeros_like(dfb_ref)

        dl = _hdot(_tri(tt, False), dc_ref[...]) + carry[...]
        carry[...] = dl[0:1, :]
        lane = lax.broadcasted_iota(jnp.int32, (tt, LANES), 1)
        df = jnp.where(lane < FOX_HEADS, dl * jax.nn.sigmoid(-(sm_ref[...] + fb_ref[...])), 0.0)
        ds_ref[...] = jnp.where(lane < FOX_HEADS, df, dd_ref[...])
        dfb_ref[...] += jnp.sum(df, axis=0, keepdims=True)

    blk = pl.BlockSpec((tt, LANES), lambda i: (nt - 1 - i, 0))
    row = pl.BlockSpec((1, LANES), lambda i: (0, 0))
    return _pcall(
        body, name=name, grid=(nt,), in_specs=[blk, blk, row, blk], out_specs=[blk, row],
        out_shape=[SDS((T, LANES), f32), SDS((1, LANES), f32)], scratch_shapes=[pltpu.VMEM((1, LANES), f32)],
        compiler_params=_cp("arbitrary"),
    )(dc, small, fb_row, dsmall_dt)


_NT = (((1,), (1,)), ((), ()))
_TN = (((0,), (0,)), ((), ()))
_SCALE = FOX_DH ** -0.5


def _causal(n):
    r = lax.broadcasted_iota(jnp.int32, (n, n), 0)
    c = lax.broadcasted_iota(jnp.int32, (n, n), 1)
    return r >= c


def _flash_fwd(u, cq, ck, *, name, tq=512):
    T = u.shape[0]
    H, dh = FOX_HEADS, FOX_DH
    tq = _pick(T, (tq,))
    nq = T // tq
    qo, ko, vo = MAIN_OFF["q"] // dh, MAIN_OFF["k"] // dh, MAIN_OFF["v"] // dh

    def body(q_ref, k_ref, v_ref, cq_ref, ck_ref, o_ref, lse_ref, m_s, l_s, acc_s):
        i, j = pl.program_id(1), pl.program_id(2)

        @pl.when(j == 0)
        def _():
            m_s[...] = jnp.full_like(m_s, NEG)
            l_s[...] = jnp.zeros_like(l_s)
            acc_s[...] = jnp.zeros_like(acc_s)

        def step(masked):
            s = lax.dot_general(q_ref[...], k_ref[...], _NT, preferred_element_type=f32) * _SCALE
            s = s + (cq_ref[...] - ck_ref[...])
            if masked:
                s = jnp.where(_causal(tq), s, NEG)
            m_new = jnp.maximum(m_s[...], jnp.max(s, axis=1, keepdims=True))
            alpha = jnp.exp(m_s[...] - m_new)
            p = jnp.exp(s - m_new)
            l_s[...] = alpha * l_s[...] + jnp.sum(p, axis=1, keepdims=True)
            acc_s[...] = alpha * acc_s[...] + jnp.dot(p.astype(bf16), v_ref[...], preferred_element_type=f32)
            m_s[...] = m_new

        @pl.when(j < i)
        def _():
            step(False)

        @pl.when(j == i)
        def _():
            step(True)

        @pl.when(j == nq - 1)
        def _():
            o_ref[...] = (acc_s[...] / l_s[...]).astype(o_ref.dtype)
            lse_ref[...] = m_s[...] + jnp.log(l_s[...])

    kv = lambda off: pl.BlockSpec((tq, dh), lambda h, i, j: (jnp.minimum(j, i), off + h))
    return _pcall(
        body, name=name, grid=(H, nq, nq),
        in_specs=[pl.BlockSpec((tq, dh), lambda h, i, j: (i, qo + h)), kv(ko), kv(vo),
                  pl.BlockSpec((None, tq, 1), lambda h, i, j: (h, i, 0)),
                  pl.BlockSpec((None, 1, tq), lambda h, i, j: (h, 0, jnp.minimum(j, i)))],
        out_specs=[pl.BlockSpec((tq, dh), lambda h, i, j: (i, h)),
                   pl.BlockSpec((None, tq, 1), lambda h, i, j: (h, i, 0))],
        out_shape=[SDS((T, H * dh), bf16), SDS((H, T, 1), f32)],
        scratch_shapes=[pltpu.VMEM((tq, 1), f32), pltpu.VMEM((tq, 1), f32), pltpu.VMEM((tq, dh), f32)],
        compiler_params=_cp("parallel", "parallel", "arbitrary"),
    )(u, u, u, cq, ck)


def _flash_delta(do, o, *, name, tq=512):
    T = do.shape[0]
    H, dh = FOX_HEADS, FOX_DH
    tq = _pick(T, (tq,))

    def body(do_ref, o_ref, d_ref):
        d_ref[...] = jnp.sum(do_ref[...].astype(f32) * o_ref[...].astype(f32), axis=1, keepdims=True)

    blk = pl.BlockSpec((tq, dh), lambda h, i: (i, h))
    return _pcall(
        body, name=name, grid=(H, T // tq), in_specs=[blk, blk],
        out_specs=pl.BlockSpec((None, tq, 1), lambda h, i: (h, i, 0)), out_shape=SDS((H, T, 1), f32),
        compiler_params=_cp("parallel", "parallel"),
    )(do, o)


def _flash_p_ds(q_ref, k_ref, v_ref, do_ref, cq_ref, ck_ref, lse_ref, dl_ref, masked, tq):
    s = lax.dot_general(q_ref[...], k_ref[...], _NT, preferred_element_type=f32) * _SCALE
    s = s + (cq_ref[...] - ck_ref[...])
    if masked:
        s = jnp.where(_causal(tq), s, NEG)
    p = jnp.exp(s - lse_ref[...])
    dp = lax.dot_general(do_ref[...], v_ref[...], _NT, preferred_element_type=f32)
    ds = p * (dp - dl_ref[...])
    return p, ds


def _flash_bwd_dq(u, do, cq, ck, lse, delta, *, name, tq=512):
    T = u.shape[0]
    H, dh = FOX_HEADS, FOX_DH
    tq = _pick(T, (tq,))
    nq = T // tq
    qo, ko, vo = MAIN_OFF["q"] // dh, MAIN_OFF["k"] // dh, MAIN_OFF["v"] // dh

    def body(q_ref, k_ref, v_ref, do_ref, cq_ref, ck_ref, lse_ref, dl_ref, dq_ref, dc_ref, acc_s, dc_s):
        i, j = pl.program_id(1), pl.program_id(2)

        @pl.when(j == 0)
        def _():
            acc_s[...] = jnp.zeros_like(acc_s)
            dc_s[...] = jnp.zeros_like(dc_s)

        def step(masked):
            _, ds = _flash_p_ds(q_ref, k_ref, v_ref, do_ref, cq_ref, ck_ref, lse_ref, dl_ref, masked, tq)
            acc_s[...] += jnp.dot(ds.astype(bf16), k_ref[...], preferred_element_type=f32)
            dc_s[...] += jnp.sum(ds, axis=1, keepdims=True)

        @pl.when(j < i)
        def _():
            step(False)

        @pl.when(j == i)
        def _():
            step(True)

        @pl.when(j == nq - 1)
        def _():
            dq_ref[...] = (acc_s[...] * _SCALE).astype(dq_ref.dtype)
            dc_ref[...] = dc_s[...]

    kv = lambda off: pl.BlockSpec((tq, dh), lambda h, i, j: (jnp.minimum(j, i), off + h))
    col = pl.BlockSpec((None, tq, 1), lambda h, i, j: (h, i, 0))
    return _pcall(
        body, name=name, grid=(H, nq, nq),
        in_specs=[pl.BlockSpec((tq, dh), lambda h, i, j: (i, qo + h)), kv(ko), kv(vo),
                  pl.BlockSpec((tq, dh), lambda h, i, j: (i, h)), col,
                  pl.BlockSpec((None, 1, tq), lambda h, i, j: (h, 0, jnp.minimum(j, i))), col, col],
        out_specs=[pl.BlockSpec((tq, dh), lambda h, i, j: (i, h)), col],
        out_shape=[SDS((T, H * dh), bf16), SDS((H, T, 1), f32)],
        scratch_shapes=[pltpu.VMEM((tq, dh), f32), pltpu.VMEM((tq, 1), f32)],
        compiler_params=_cp("parallel", "parallel", "arbitrary"),
    )(u, u, u, do, cq, ck, lse, delta)


def _flash_bwd_dkv(u, do, cq, ck, lse, delta, *, name, tq=512):
    T = u.shape[0]
    H, dh = FOX_HEADS, FOX_DH
    tq = _pick(T, (tq,))
    nq = T // tq
    qo, ko, vo = MAIN_OFF["q"] // dh, MAIN_OFF["k"] // dh, MAIN_OFF["v"] // dh

    def body(q_ref, k_ref, v_ref, do_ref, cq_ref, ck_ref, lse_ref, dl_ref, dk_ref, dv_ref, dc_ref,
             dk_s, dv_s, dc_s):
        j, i = pl.program_id(1), pl.program_id(2)

        @pl.when(i == 0)
        def _():
            dk_s[...] = jnp.zeros_like(dk_s)
            dv_s[...] = jnp.zeros_like(dv_s)
            dc_s[...] = jnp.zeros_like(dc_s)

        def step(masked):
            p, ds = _flash_p_ds(q_ref, k_ref, v_ref, do_ref, cq_ref, ck_ref, lse_ref, dl_ref, masked, tq)
            dv_s[...] += lax.dot_general(p.astype(bf16), do_ref[...], _TN, preferred_element_type=f32)
            dk_s[...] += lax.dot_general(ds.astype(bf16), q_ref[...], _TN, preferred_element_type=f32)
            dc_s[...] += jnp.sum(ds, axis=0, keepdims=True)

        @pl.when(i > j)
        def _():
            step(False)

        @pl.when(i == j)
        def _():
            step(True)

        @pl.when(i == nq - 1)
        def _():
            dk_ref[...] = (dk_s[...] * _SCALE).astype(dk_ref.dtype)
            dv_ref[...] = dv_s[...].astype(dv_ref.dtype)
            dc_ref[...] = -dc_s[...]

    qi = lambda h, j, i: jnp.maximum(i, j)
    colq = pl.BlockSpec((None, tq, 1), lambda h, j, i: (h, jnp.maximum(i, j), 0))
    return _pcall(
        body, name=name, grid=(H, nq, nq),
        in_specs=[pl.BlockSpec((tq, dh), lambda h, j, i: (jnp.maximum(i, j), qo + h)),
                  pl.BlockSpec((tq, dh), lambda h, j, i: (j, ko + h)),
                  pl.BlockSpec((tq, dh), lambda h, j, i: (j, vo + h)),
                  pl.BlockSpec((tq, dh), lambda h, j, i: (jnp.maximum(i, j), h)), colq,
                  pl.BlockSpec((None, 1, tq), lambda h, j, i: (h, 0, j)), colq, colq],
        out_specs=[pl.BlockSpec((tq, dh), lambda h, j, i: (j, h)),
                   pl.BlockSpec((tq, dh), lambda h, j, i: (j, h)),
                   pl.BlockSpec((None, 1, tq), lambda h, j, i: (h, 0, j))],
        out_shape=[SDS((T, H * dh), bf16), SDS((T, H * dh), bf16), SDS((H, 1, T), f32)],
        scratch_shapes=[pltpu.VMEM((tq, dh), f32), pltpu.VMEM((tq, dh), f32), pltpu.VMEM((1, tq), f32)],
        compiler_params=_cp("parallel", "parallel", "arbitrary"),
    )(u, u, u, do, cq, ck, lse, delta)


def _ssd_common(sm_ref, dtb_ref, alog_ref):
    L = SSD_L
    dt = _softplus(sm_ref[...] + dtb_ref[...])
    a = -jnp.exp(alog_ref[...])
    cs = _hdot(_tri(L, True), dt * a)
    return dt, a, cs, cs.T, cs[L - 1:L, :]


def _ssd_fwd(xbc, small, dtb_row, alog_row, dskip_row, *, name):
    T = xbc.shape[0]
    L, P, N = SSD_L, SSM_P, SSM_N
    nc = T // L
    hpg = SSM_HEADS // SSM_G

    def body(x_ref, sm_ref, dtb_ref, alog_ref, dsk_ref, y_ref, hs_ref, h_s):
        @pl.when(pl.program_id(0) == 0)
        def _():
            h_s[...] = jnp.zeros_like(h_s)

        dt, a, cs, csT, tot = _ssd_common(sm_ref, dtb_ref, alog_ref)
        hs_ref[...] = h_s[...]
        causal = _causal(L)
        for g in range(SSM_G):
            bg = x_ref[:, SSM_W + g * N:SSM_W + (g + 1) * N]
            cg = x_ref[:, SSM_W + SSM_G * N + g * N:SSM_W + SSM_G * N + (g + 1) * N]
            cb = lax.dot_general(cg, bg, _NT, preferred_element_type=f32)
            hg = h_s[g * hpg * P:(g + 1) * hpg * P, :]
            yoff = lax.dot_general(cg, hg.astype(bf16), _NT, preferred_element_type=f32)
            for hh in range(hpg):
                h = g * hpg + hh
                ln = HL + h
                acol, arow = cs[:, ln:ln + 1], csT[ln:ln + 1, :]
                w = jnp.where(causal, cb * jnp.exp(jnp.minimum(acol - arow, 0.0)), 0.0)
                xs = x_ref[:, h * P:(h + 1) * P].astype(f32)
                xh = xs * dt[:, ln:ln + 1]
                y = jnp.dot(w.astype(bf16), xh.astype(bf16), preferred_element_type=f32)
                y = y + jnp.exp(acol) * yoff[:, hh * P:(hh + 1) * P] + xs * dsk_ref[:, ln:ln + 1]
                y_ref[:, h * P:(h + 1) * P] = y.astype(y_ref.dtype)
                decay = jnp.exp(tot[:, ln:ln + 1] - acol)
                st = lax.dot_general((xh * decay).astype(bf16), bg, _TN, preferred_element_type=f32)
                h_s[h * P:(h + 1) * P, :] = jnp.exp(tot[:, ln:ln + 1]) * h_s[h * P:(h + 1) * P, :] + st

    row = pl.BlockSpec((1, LANES), lambda c: (0, 0))
    return _pcall(
        body, name=name, grid=(nc,),
        in_specs=[pl.BlockSpec((L, XBC_W), lambda c: (c, 0)), pl.BlockSpec((L, LANES), lambda c: (c, 0)),
                  row, row, row],
        out_specs=[pl.BlockSpec((L, SSM_W), lambda c: (c, 0)),
                   pl.BlockSpec((None, SSM_HEADS * P, N), lambda c: (c, 0, 0))],
        out_shape=[SDS((T, SSM_W), bf16), SDS((nc, SSM_HEADS * P, N), f32)],
        scratch_shapes=[pltpu.VMEM((SSM_HEADS * P, N), f32)],
        compiler_params=_cp("arbitrary"),
    )(xbc, small, dtb_row, alog_row, dskip_row)


def _ssd_bwd(dy, xbc, small, dtb_row, alog_row, dskip_row, hstates, *, name):
    T = xbc.shape[0]
    L, P, N = SSD_L, SSM_P, SSM_N
    nc = T // L
    hpg = SSM_HEADS // SSM_G
    GW = hpg * P

    def body(dy_ref, x_ref, sm_ref, dtb_ref, alog_ref, dsk_ref, hs_ref,
             dx_ref, dsm_ref, ddtb_ref, dalog_ref, ddsk_ref, dh_s, ea_s, dx_s):
        @pl.when(pl.program_id(0) == 0)
        def _():
            dh_s[...] = jnp.zeros_like(dh_s)
            ddtb_ref[...] = jnp.zeros_like(ddtb_ref)
            dalog_ref[...] = jnp.zeros_like(dalog_ref)
            ddsk_ref[...] = jnp.zeros_like(ddsk_ref)

        dt, a, cs, csT, tot = _ssd_common(sm_ref, dtb_ref, alog_ref)
        causal = _causal(L)
        lane = lax.broadcasted_iota(jnp.int32, (L, LANES), 1)
        sub = lax.broadcasted_iota(jnp.int32, (LANES, L), 0)
        lane1 = lax.broadcasted_iota(jnp.int32, (1, LANES), 1)
        da_col = jnp.zeros((L, LANES), f32)
        da_row = jnp.zeros((LANES, L), f32)
        da_tot = jnp.zeros((1, LANES), f32)
        ddt = jnp.zeros((L, LANES), f32)
        ddsk = jnp.zeros((1, LANES), f32)
        for g in range(SSM_G):
            bo, co = SSM_W + g * N, SSM_W + SSM_G * N + g * N
            bg, cg = x_ref[:, bo:bo + N], x_ref[:, co:co + N]
            cb = lax.dot_general(cg, bg, _NT, preferred_element_type=f32)
            hg = hs_ref[g * GW:(g + 1) * GW, :]
            hgb = hg.astype(bf16)
            dhn = dh_s[g * GW:(g + 1) * GW, :]
            dhnb = dhn.astype(bf16)
            yoff = lax.dot_general(cg, hgb, _NT, preferred_element_type=f32)
            bdh = lax.dot_general(bg, dhnb, _NT, preferred_element_type=f32)
            dcb = jnp.zeros((L, L), f32)
            for hh in range(hpg):
                h = g * hpg + hh
                ln = HL + h
                sl = slice(hh * P, (hh + 1) * P)
                acol, arow = cs[:, ln:ln + 1], csT[ln:ln + 1, :]
                e = jnp.where(causal, jnp.exp(jnp.minimum(acol - arow, 0.0)), 0.0)
                w = cb * e
                xs = x_ref[:, h * P:(h + 1) * P].astype(f32)
                dth = dt[:, ln:ln + 1]
                xh = xs * dth
                dyh = dy_ref[:, h * P:(h + 1) * P].astype(f32)
                dyb = dyh.astype(bf16)
                ea = jnp.exp(acol)
                decay = jnp.exp(tot[:, ln:ln + 1] - acol)
                etot = jnp.exp(tot[:, ln:ln + 1])
                dw = lax.dot_general(dyb, xh.astype(bf16), _NT, preferred_element_type=f32)
                m = dw * w
                dcb = dcb + dw * e
                dxs_state = decay * bdh[:, sl]
                dxh = lax.dot_general(w.astype(bf16), dyb, _TN, preferred_element_type=f32) + dxs_state
                r = jnp.sum(xh * dxs_state, axis=1, keepdims=True)
                col = (jnp.sum(m, axis=1, keepdims=True) - r
                       + jnp.sum(dyh * (ea * yoff[:, sl]), axis=1, keepdims=True))
                da_col = da_col + jnp.where(lane == ln, col, 0.0)
                da_row = da_row + jnp.where(sub == ln, jnp.sum(m, axis=0, keepdims=True), 0.0)
                hprod = jnp.sum(jnp.sum(dhn[sl, :] * hg[sl, :], axis=1, keepdims=True), axis=0, keepdims=True)
                t_h = jnp.sum(r, axis=0, keepdims=True) + etot * hprod
                da_tot = da_tot + jnp.where(lane1 == ln, t_h, 0.0)
                ea_s[:, sl] = (ea * dyh).astype(bf16)
                dx_s[:, sl] = (decay * xh).astype(bf16)
                dx_ref[:, h * P:(h + 1) * P] = (dxh * dth + dyh * dsk_ref[:, ln:ln + 1]).astype(dx_ref.dtype)
                ddt = ddt + jnp.where(lane == ln, jnp.sum(dxh * xs, axis=1, keepdims=True), 0.0)
                ddsk = ddsk + jnp.where(
                    lane1 == ln, jnp.sum(jnp.sum(dyh * xs, axis=1, keepdims=True), axis=0, keepdims=True), 0.0)
                dh_s[g * GW + hh * P:g * GW + (hh + 1) * P, :] = etot * dhn[sl, :]
            dcbb = dcb.astype(bf16)
            eab, dxb = ea_s[...], dx_s[...]
            dcg = (jnp.dot(dcbb, bg, preferred_element_type=f32)
                   + jnp.dot(eab, hgb, preferred_element_type=f32))
            dbg = (lax.dot_general(dcbb, cg, _TN, preferred_element_type=f32)
                   + jnp.dot(dxb, dhnb, preferred_element_type=f32))
            dx_ref[:, bo:bo + N] = dbg.astype(dx_ref.dtype)
            dx_ref[:, co:co + N] = dcg.astype(dx_ref.dtype)
            dh_s[g * GW:(g + 1) * GW, :] += lax.dot_general(eab, cg, _TN, preferred_element_type=f32)

        rowi = lax.broadcasted_iota(jnp.int32, (L, LANES), 0)
        da_cs = da_col - da_row.T + jnp.where(rowi == L - 1, da_tot, 0.0)
        dda = _hdot(_tri(L, False), da_cs)
        ddt = ddt + dda * a
        heads = (lane >= HL) & (lane < HL + SSM_HEADS)
        draw = jnp.where(heads, ddt * jax.nn.sigmoid(sm_ref[...] + dtb_ref[...]), 0.0)
        dsm_ref[...] = draw
        ddtb_ref[...] += jnp.sum(draw, axis=0, keepdims=True)
        dalog_ref[...] += jnp.where(lane1 >= HL, jnp.sum(dda * dt, axis=0, keepdims=True) * a, 0.0)
        ddsk_ref[...] += ddsk

    rev = lambda c: (nc - 1 - c, 0)
    row = pl.BlockSpec((1, LANES), lambda c: (0, 0))
    return _pcall(
        body, name=name, grid=(nc,),
        in_specs=[pl.BlockSpec((L, SSM_W), rev), pl.BlockSpec((L, XBC_W), rev), pl.BlockSpec((L, LANES), rev),
                  row, row, row, pl.BlockSpec((None, SSM_HEADS * P, N), lambda c: (nc - 1 - c, 0, 0))],
        out_specs=[pl.BlockSpec((L, XBC_W), rev), pl.BlockSpec((L, LANES), rev), row, row, row],
        out_shape=[SDS((T, XBC_W), bf16), SDS((T, LANES), f32)] + [SDS((1, LANES), f32)] * 3,
        scratch_shapes=[pltpu.VMEM((SSM_HEADS * P, N), f32), pltpu.VMEM((L, GW), bf16), pltpu.VMEM((L, GW), bf16)],
        compiler_params=_cp("arbitrary"),
    )(dy, xbc, small, dtb_row, alog_row, dskip_row, hstates)


def _loss_head(x, w, target, *, name, tt=256):
    T, D = x.shape
    tt = _pick(T, (tt, 128))

    def fn(x, w, tg):
        (y,) = _f_rms(x, w)
        return 0.5 * jnp.sum(jnp.sum(jnp.square(y - tg), axis=1, keepdims=True), axis=0, keepdims=True) / D

    def body(x_ref, w_ref, t_ref, loss_ref, dx_ref, dw_ref):
        i = pl.program_id(0)
        ls, vjp = jax.vjp(fn, x_ref[...], w_ref[...], t_ref[...])
        dx, dw, _ = vjp(jnp.ones((1, 1), f32))
        dx_ref[...] = dx

        @pl.when(i == 0)
        def _():
            loss_ref[...] = ls
            dw_ref[...] = dw

        @pl.when(i > 0)
        def _():
            loss_ref[...] += ls
            dw_ref[...] += dw

    blk = pl.BlockSpec((tt, D), lambda i: (i, 0))
    row = pl.BlockSpec((1, D), lambda i: (0, 0))
    return _pcall(
        body, name=name, grid=(T // tt,), in_specs=[blk, row, blk],
        out_specs=[pl.BlockSpec((1, 1), lambda i: (0, 0)), blk, row],
        out_shape=[SDS((1, 1), f32), SDS((T, D), f32), SDS((1, D), f32)],
        compiler_params=_cp("arbitrary"),
    )(x, w, target)


def _adamw(gs, w, m, v, *, name, tr=1024):
    ns, R, _ = gs.shape
    tr = _pick(R, (tr, 512, 256, 128, 64, 32, 16, 8))
    c1 = 1.0 / (1.0 - ADAM_B1 ** ADAM_STEP)
    c2 = 1.0 / (1.0 - ADAM_B2 ** ADAM_STEP)

    def body(g_ref, w_ref, m_ref, v_ref, go_ref, d_ref, mo_ref, vo_ref):
        g = g_ref[0].astype(f32)
        for s in range(1, ns):
            g = g + g_ref[s].astype(f32)
        mn = ADAM_B1 * m_ref[...] + (1.0 - ADAM_B1) * g
        vn = ADAM_B2 * v_ref[...] + (1.0 - ADAM_B2) * (g * g)
        go_ref[...] = g
        mo_ref[...] = mn
        vo_ref[...] = vn
        d_ref[...] = -ADAM_LR * ((mn * c1) / (jnp.sqrt(vn * c2) + ADAM_EPS) + ADAM_WD * w_ref[...])

    blk = pl.BlockSpec((tr, LANES), lambda i: (i, 0))
    return _pcall(
        body, name=name, grid=(R // tr,),
        in_specs=[pl.BlockSpec((ns, tr, LANES), lambda i: (0, i, 0)), blk, blk, blk],
        out_specs=[blk] * 4, out_shape=[SDS((R, LANES), f32)] * 4,
        compiler_params=_cp("parallel"),
    )(gs, w, m, v)


def _sum_slabs(gs, *, name, tr=1024):
    ns, R, _ = gs.shape
    tr = _pick(R, (tr, 512, 256, 128, 64, 32, 16, 8))

    def body(g_ref, o_ref):
        g = g_ref[0]
        for s in range(1, ns):
            g = g + g_ref[s]
        o_ref[...] = g

    return _pcall(
        body, name=name, grid=(R // tr,), in_specs=[pl.BlockSpec((ns, tr, LANES), lambda i: (0, i, 0))],
        out_specs=pl.BlockSpec((tr, LANES), lambda i: (i, 0)), out_shape=SDS((R, LANES), f32),
        compiler_params=_cp("parallel"),
    )(gs)


_MESH = pl.DeviceIdType.MESH
_HBM = pl.BlockSpec(memory_space=pltpu.HBM)


def _allgather(blob, *, name):
    R, W = blob.shape

    def body(x_ref, out_ref, send_sems, recv_sems, local_sem):
        x, y, c = lax.axis_index("x"), lax.axis_index("y"), lax.axis_index("c")
        me, sibling = (x, y, c), (x, y, 1 - c)
        chips = [(1 - x, y), (x, 1 - y), (1 - x, 1 - y)]

        def slot(px, py, pc):
            return out_ref.at[4 * px + 2 * py + pc]

        def copy(k, block, to, src=None):
            return pltpu.make_async_remote_copy(
                src_ref=slot(*block) if src is None else src, dst_ref=slot(*block),
                send_sem=send_sems.at[k], recv_sem=recv_sems.at[k], device_id=to, device_id_type=_MESH)

        mine = pltpu.make_async_copy(x_ref, slot(*me), local_sem)
        mine.start()
        first = [copy(0, me, sibling, src=x_ref)]
        first += [copy(1 + j, me, (*chip, c), src=x_ref) for j, chip in enumerate(chips)]
        for cp in first:
            cp.start()
        passed = [copy(4 + j, (*chip, c), sibling) for j, chip in enumerate(chips)]
        for j, chip in enumerate(chips):
            copy(1 + j, (*chip, c), me).wait_recv()
            passed[j].start()
        copy(0, sibling, me).wait_recv()
        for j, chip in enumerate(chips):
            copy(4 + j, (*chip, 1 - c), me).wait_recv()
        for cp in first + passed:
            cp.wait_send()
        mine.wait()

    return _pcall(
        body, name=name, in_specs=[_HBM], out_specs=_HBM, out_shape=SDS((N_DEV, R, W), blob.dtype),
        scratch_shapes=[pltpu.SemaphoreType.DMA((7,)), pltpu.SemaphoreType.DMA((7,)), pltpu.SemaphoreType.DMA],
    )(blob)


def _exchange(slabs, *, name):
    _, R, W = slabs.shape

    def body(in_ref, out_ref, send_sems, recv_sems, local_sem):
        x, y, c = lax.axis_index("x"), lax.axis_index("y"), lax.axis_index("c")
        me = 4 * x + 2 * y + c
        mine = pltpu.make_async_copy(in_ref.at[me], out_ref.at[me], local_sem)
        mine.start()
        sends, recvs = [], []
        for k in range(1, N_DEV):
            px, py, pc = x ^ ((k >> 2) & 1), y ^ ((k >> 1) & 1), c ^ (k & 1)
            peer = 4 * px + 2 * py + pc

            def copy(src_slot, dst_slot):
                return pltpu.make_async_remote_copy(
                    src_ref=in_ref.at[src_slot], dst_ref=out_ref.at[dst_slot],
                    send_sem=send_sems.at[k - 1], recv_sem=recv_sems.at[k - 1],
                    device_id=(px, py, pc), device_id_type=_MESH)

            sends.append(copy(peer, me))
            recvs.append(copy(me, peer))
            sends[-1].start()
        for cp in recvs:
            cp.wait_recv()
        for cp in sends:
            cp.wait_send()
        mine.wait()

    return _pcall(
        body, name=name, in_specs=[_HBM], out_specs=_HBM, out_shape=SDS(slabs.shape, slabs.dtype),
        scratch_shapes=[pltpu.SemaphoreType.DMA((7,)), pltpu.SemaphoreType.DMA((7,)), pltpu.SemaphoreType.DMA],
    )(slabs)


def _row128(vec, off):
    return jnp.pad(vec.astype(f32), (off, LANES - off - vec.shape[0]))[None, :]


def _cb(name, cw):
    return MAIN_OFF[name] // cw


def _layer_fwd(x, p, tag):
    s = {"x": x}
    (h,) = _rowwise(_f_rms, [(x, 0)], [p["norm_w"]], [bf16], cw=D_MODEL, ncol=1, name=f"rms_{tag}")
    u = _mm(h, p["w_main"], mode="nn", out_dtype=bf16, name=f"in_main_{tag}")
    small = _mm(h, p["w_small"], mode="nn", out_dtype=f32, name=f"in_small_{tag}")
    s.update(h=h, u=u, small=small)
    W = BW

    c = _fox_gate_fwd(small, p["fb_row"], name=f"fox_gate_{tag}")
    ct = c[:, :FOX_HEADS].T
    cq, ck = ct[:, :, None], ct[:, None, :]
    o, lse = _flash_fwd(u, cq, ck, name=f"flash_{tag}")
    (ya,) = _rowwise(_f_ya, [(o, 0), (u, _cb("ga", W))], [], [bf16], cw=W, ncol=1, name=f"ya_{tag}")
    s.update(cq=cq, ck=ck, o=o, lse=lse)

    pre_b, xbc_c = _conv_fwd(u, _cb("xbc", 512), XBC_W, p["ssm_conv_w"], p["ssm_conv_b"], with_silu=True,
                             name=f"ssm_conv_{tag}")
    y_ssd, hst = _ssd_fwd(xbc_c, small, p["dtb_row"], p["alog_row"], p["dskip_row"], name=f"ssd_{tag}")
    (yb,) = _rowwise(_f_yb, [(y_ssd, 0), (u, _cb("z", W))], [p["ssm_norm_w"]], [bf16], cw=W, ncol=1,
                     name=f"yb_{tag}")
    s.update(pre_b=pre_b, xbc_c=xbc_c, y_ssd=y_ssd, hst=hst)

    (pc,) = _rowwise(_f_mul, [(u, _cb("scc", 512)), (u, _cb("scx", 512))], [], [bf16], cw=512, ncol=2,
                     name=f"sc_pre_{tag}")
    (cv,) = _conv_fwd(pc, 0, W, p["sc_conv_w"], p["sc_conv_b"], with_silu=False, name=f"sc_conv_{tag}")
    (yc,) = _rowwise(_f_yc, [(u, _cb("scb", 512)), (cv, 0), (u, _cb("gc", 512))], [], [bf16], cw=512, ncol=2,
                     name=f"yc_{tag}")
    s.update(pc=pc, cv=cv)

    (pd,) = _rowwise(_f_glu, [(u, _cb("glua", 512)), (u, _cb("glug", 512))], [], [bf16], cw=512, ncol=2,
                     name=f"cf_pre_{tag}")
    (cf,) = _conv_fwd(pd, 0, W, p["cf_conv_w"], p["cf_conv_b"], with_silu=False, name=f"cf_conv_{tag}")
    (yd,) = _rowwise(_f_yd, [(cf, 0), (u, _cb("gd", W))], [p["cf_ln_w"], p["cf_ln_b"]], [bf16], cw=W, ncol=1,
                     name=f"yd_{tag}")
    s.update(pd=pd, cf=cf)

    ys = (ya, yb, yc, yd)
    gs = [_mm(h, p["w_gate"][i], mode="nn", out_dtype=bf16, name=f"gate{i}_{tag}") for i in range(4)]
    ps = [_mm(ys[i], p["w_branch"][i], mode="nn", out_dtype=bf16, name=f"branch{i}_{tag}") for i in range(4)]
    (merged,) = _rowwise(_f_merge, [(a, 0) for a in gs + ps], list(p["b_gate"]), [bf16], cw=512,
                         ncol=D_MODEL // 512, name=f"merge_{tag}")
    out = _mm(merged, p["w_out"], mode="nn", out_dtype=f32, add=x, name=f"out_{tag}")
    s.update(ys=ys, gs=gs, ps=ps, merged=merged)
    return out, s


def _layer_bwd(dout, p, s, tag):
    g = {}
    W = BW
    u, h, small = s["u"], s["h"], s["small"]
    dob = dout.astype(bf16)
    dmerged = _mm(dob, p["w_out"], mode="nt", out_dtype=bf16, name=f"d_merged_{tag}")
    g["w_out"] = _mm(s["merged"], dob, mode="tn", out_dtype=bf16, name=f"dw_out_{tag}")
    r = _rowwise_bwd(_f_merge, [(a, 0) for a in s["gs"] + s["ps"]], list(p["b_gate"]), [dmerged], [bf16] * 8,
                     cw=512, ncol=D_MODEL // 512, name=f"d_merge_{tag}")
    dgs, dps, g["b_gate"] = r[0:4], r[4:8], r[8:12]
    g["w_gate"] = [_mm(h, dgs[i], mode="tn", out_dtype=bf16, name=f"dw_gate{i}_{tag}") for i in range(4)]
    g["w_branch"] = [_mm(s["ys"][i], dps[i], mode="tn", out_dtype=bf16, name=f"dw_branch{i}_{tag}")
                     for i in range(4)]
    dh = None
    for i in range(4):
        dh = _mm(dgs[i], p["w_gate"][i], mode="nt", out_dtype=f32, add=dh, tn=1024, name=f"dh_gate{i}_{tag}")
    dya, dyb, dyc, dyd = [_mm(dps[i], p["w_branch"][i], mode="nt", out_dtype=bf16, name=f"d_y{i}_{tag}")
                          for i in range(4)]
    du = {}

    do, du["ga"] = _rowwise_bwd(_f_ya, [(s["o"], 0), (u, _cb("ga", W))], [], [dya], [bf16, bf16], cw=W, ncol=1,
                                name=f"d_ya_{tag}")
    delta = _flash_delta(do, s["o"], name=f"flash_delta_{tag}")
    du["q"], dcq = _flash_bwd_dq(u, do, s["cq"], s["ck"], s["lse"], delta, name=f"flash_dq_{tag}")
    du["k"], du["v"], dck = _flash_bwd_dkv(u, do, s["cq"], s["ck"], s["lse"], delta, name=f"flash_dkv_{tag}")
    dc = jnp.pad((dcq[:, :, 0] + dck[:, 0, :]).T, ((0, 0), (0, LANES - FOX_HEADS)))

    dy_ssd, du["z"], g["ssm_norm_w"] = _rowwise_bwd(
        _f_yb, [(s["y_ssd"], 0), (u, _cb("z", W))], [p["ssm_norm_w"]], [dyb], [bf16, bf16], cw=W, ncol=1,
        name=f"d_yb_{tag}")
    dxbc_c, dsmall_dt, g["dtb_row"], g["alog_row"], g["dskip_row"] = _ssd_bwd(
        dy_ssd, s["xbc_c"], small, p["dtb_row"], p["alog_row"], p["dskip_row"], s["hst"], name=f"d_ssd_{tag}")
    (dpre_b,) = _rowwise_bwd(_f_silu, [(s["pre_b"], 0)], [], [dxbc_c], [bf16], cw=512, ncol=3,
                             name=f"d_ssm_silu_{tag}")
    du["xbc"], g["ssm_conv_w"], g["ssm_conv_b"] = _conv_bwd(dpre_b, u, _cb("xbc", 512), XBC_W, p["ssm_conv_w"],
                                                            name=f"d_ssm_conv_{tag}")

    du["scb"], dcv, du["gc"] = _rowwise_bwd(
        _f_yc, [(u, _cb("scb", 512)), (s["cv"], 0), (u, _cb("gc", 512))], [], [dyc], [bf16] * 3, cw=512, ncol=2,
        name=f"d_yc_{tag}")
    dpc, g["sc_conv_w"], g["sc_conv_b"] = _conv_bwd(dcv, s["pc"], 0, W, p["sc_conv_w"], name=f"d_sc_conv_{tag}")
    du["scc"], du["scx"] = _rowwise_bwd(_f_mul, [(u, _cb("scc", 512)), (u, _cb("scx", 512))], [], [dpc],
                                        [bf16, bf16], cw=512, ncol=2, name=f"d_sc_pre_{tag}")

    dcf, du["gd"], g["cf_ln_w"], g["cf_ln_b"] = _rowwise_bwd(
        _f_yd, [(s["cf"], 0), (u, _cb("gd", W))], [p["cf_ln_w"], p["cf_ln_b"]], [dyd], [bf16, bf16], cw=W, ncol=1,
        name=f"d_yd_{tag}")
    dpd, g["cf_conv_w"], g["cf_conv_b"] = _conv_bwd(dcf, s["pd"], 0, W, p["cf_conv_w"], name=f"d_cf_conv_{tag}")
    du["glua"], du["glug"] = _rowwise_bwd(_f_glu, [(u, _cb("glua", 512)), (u, _cb("glug", 512))], [], [dpd],
                                          [bf16, bf16], cw=512, ncol=2, name=f"d_cf_pre_{tag}")

    dsmall, g["fb_row"] = _fox_gate_bwd(dc, small, p["fb_row"], dsmall_dt, name=f"d_fox_gate_{tag}")
    du_main = jnp.concatenate([du[n] for n in MAIN_NAMES], axis=1)
    dsb = dsmall.astype(bf16)
    g["w_main"] = _mm(h, du_main, mode="tn", out_dtype=bf16, name=f"dw_main_{tag}")
    g["w_small"] = _mm(h, dsb, mode="tn", out_dtype=bf16, name=f"dw_small_{tag}")
    dh = _mm(du_main, p["w_main"], mode="nt", out_dtype=f32, add=dh, tn=1024, name=f"dh_main_{tag}")
    dh = _mm(dsb, p["w_small"], mode="nt", out_dtype=f32, add=dh, tn=1024, name=f"dh_small_{tag}")
    dx, g["norm_w"] = _rowwise_bwd(_f_rms_res, [(s["x"], 0)], [p["norm_w"]], [dh, dout], [f32], cw=D_MODEL,
                                   ncol=1, name=f"d_rms_{tag}")
    return dx, g


_BIG = ("w_in", "w_gate", "w_branch", "w_out")
_REPL = ("norm_w", "fg_bias", "ssm_conv_b", "dt_bias", "a_log", "d_skip", "ssm_norm_w", "sc_conv_b", "cf_conv_b",
         "cf_ln_w", "cf_ln_b", "final_norm_w")
_SHSM = ("ssm_conv_w", "sc_conv_w", "cf_conv_w", "b_gate")


def _to_rows(a, dtype):
    flat = a.astype(dtype).reshape(-1)
    n = flat.shape[0]
    per = 8 * LANES
    flat = jnp.pad(flat, (0, (-n) % per))
    return flat.reshape(-1, LANES)


def _pack(arrs, dtype):
    rows = [_to_rows(a, dtype) for a in arrs]
    return jnp.concatenate(rows, axis=0), [r.shape[0] for r in rows]


def _unpack(blob, shapes):
    lead = blob.shape[:-2]
    out, r0 = [], 0
    for shp in shapes:
        n = 1
        for d in shp:
            n *= d
        nr = (-(-n // (8 * LANES))) * 8
        piece = blob[..., r0:r0 + nr, :].reshape(lead + (nr * LANES,))[..., :n]
        out.append(piece.reshape(lead + tuple(shp)))
        r0 += nr
    return out


def _main_cols(w):
    off, pieces = 0, {}
    for n, sz in zip(IN_NAMES, IN_SIZES):
        pieces[n] = w[..., off:off + sz]
        off += sz
    main = jnp.concatenate([pieces[n] for n in MAIN_NAMES], axis=-1)
    pad = jnp.zeros(w.shape[:-1] + (LANES - 24,), w.dtype)
    small = jnp.concatenate([pieces["f"], pieces["dt"], pad], axis=-1)
    return main, small


def _orig_cols(main, small):
    pieces = {}
    for n in MAIN_NAMES:
        sz = XBC_W if n == "xbc" else 1024
        pieces[n] = main[..., MAIN_OFF[n]:MAIN_OFF[n] + sz]
    pieces["f"] = small[..., 0:8]
    pieces["dt"] = small[..., 8:24]
    return jnp.concatenate([pieces[n] for n in IN_NAMES], axis=-1)


def kernel(*args):
    names = ["x", "norm_w", "w_in", "fg_bias", "ssm_conv_w", "ssm_conv_b", "dt_bias", "a_log", "d_skip",
             "ssm_norm_w", "sc_conv_w", "sc_conv_b", "cf_conv_w", "cf_conv_b", "cf_ln_w", "cf_ln_b", "w_gate",
             "b_gate", "w_branch", "w_out", "final_norm_w"]
    wnames = names[1:]
    a = dict(zip(names, args[:21]))
    target = args[21]
    mom = dict(zip(["m_" + n for n in wnames], args[22:42]))
    mom.update(zip(["v_" + n for n in wnames], args[42:62]))
    NL = a["w_in"].shape[0]
    x0 = a["x"][0]
    tgt = target[0]
    me = 4 * lax.axis_index("x") + 2 * lax.axis_index("y") + lax.axis_index("c")

    big_blob, _ = _pack([a[n] for n in _BIG], bf16)
    big_all = _allgather(big_blob, name="gather_big")
    w_in_g, w_gate_g, w_branch_g, w_out_g = _unpack(big_all, [a[n].shape for n in _BIG])
    sm_blob, _ = _pack([a[n] for n in _SHSM], f32)
    sm_all = _allgather(sm_blob, name="gather_small")
    ssm_cw_g, sc_cw_g, cf_cw_g, b_gate_g = _unpack(sm_all, [a[n].shape for n in _SHSM])

    def cat_last(gathered):
        return jnp.moveaxis(gathered, 0, -2).reshape(gathered.shape[1:-1] + (-1,))

    w_in_full = cat_last(w_in_g)
    w_main_all, w_small_all = _main_cols(w_in_full)
    w_gate_full = jnp.moveaxis(w_gate_g, 0, 2).reshape(NL, 4, D_MODEL, D_MODEL)
    w_branch_full = cat_last(w_branch_g)
    w_out_full = jnp.moveaxis(w_out_g, 0, 1).reshape(NL, D_MODEL, D_MODEL)
    ssm_cw, sc_cw, cf_cw, b_gate_full = cat_last(ssm_cw_g), cat_last(sc_cw_g), cat_last(cf_cw_g), cat_last(b_gate_g)

    layers = []
    for l in range(NL):
        layers.append(dict(
            norm_w=a["norm_w"][l][None], w_main=w_main_all[l], w_small=w_small_all[l],
            fb_row=_row128(a["fg_bias"][l], 0), dtb_row=_row128(a["dt_bias"][l], HL),
            alog_row=_row128(a["a_log"][l], HL), dskip_row=_row128(a["d_skip"][l], HL),
            ssm_conv_w=ssm_cw[l], ssm_conv_b=a["ssm_conv_b"][l][None], ssm_norm_w=a["ssm_norm_w"][l][None],
            sc_conv_w=sc_cw[l], sc_conv_b=a["sc_conv_b"][l][None],
            cf_conv_w=cf_cw[l], cf_conv_b=a["cf_conv_b"][l][None],
            cf_ln_w=a["cf_ln_w"][l][None], cf_ln_b=a["cf_ln_b"][l][None],
            w_gate=[w_gate_full[l, i] for i in range(4)], b_gate=[b_gate_full[l, i][None] for i in range(4)],
            w_branch=[w_branch_full[l, i] for i in range(4)], w_out=w_out_full[l]))

    xl, saved = x0, []
    for l in range(NL):
        xl, s = _layer_fwd(xl, layers[l], f"l{l}")
        saved.append(s)
    loss_part, dx, d_fnw = _loss_head(xl, a["final_norm_w"][None], tgt, name="loss_head")
    grads = [None] * NL
    for l in reversed(range(NL)):
        dx, grads[l] = _layer_bwd(dx, layers[l], saved[l], f"l{l}")

    def stack(key):
        return jnp.stack([grads[l][key] for l in range(NL)])

    d_w_in = _orig_cols(stack("w_main"), stack("w_small"))
    sh_in = a["w_in"].shape[-1]
    s_in = jnp.moveaxis(d_w_in.reshape(NL, D_MODEL, N_DEV, sh_in), 2, 0)
    d_w_gate = jnp.stack([jnp.stack(grads[l]["w_gate"]) for l in range(NL)])
    s_gate = jnp.moveaxis(d_w_gate.reshape(NL, 4, N_DEV, D_MODEL // N_DEV, D_MODEL), 2, 0)
    d_w_branch = jnp.stack([jnp.stack(grads[l]["w_branch"]) for l in range(NL)])
    s_branch = jnp.moveaxis(d_w_branch.reshape(NL, 4, BW, N_DEV, D_MODEL // N_DEV), 3, 0)
    s_out = jnp.moveaxis(stack("w_out").reshape(NL, N_DEV, D_MODEL // N_DEV, D_MODEL), 1, 0)
    slabs = jnp.concatenate([t.reshape(N_DEV, -1, LANES) for t in (s_in, s_gate, s_branch, s_out)], axis=1)
    got = _exchange(slabs, name="exchange_big")
    wb, _ = _pack([a[n] for n in _BIG], f32)
    mb, _ = _pack([mom["m_" + n] for n in _BIG], f32)
    vb, _ = _pack([mom["v_" + n] for n in _BIG], f32)
    big_out = _adamw(got, wb, mb, vb, name="adamw_big")
    big_res = [_unpack(o, [a[n].shape for n in _BIG]) for o in big_out]

    def vec(row, off, n):
        return row[0, off:off + n]

    small_grads = {
        "norm_w": jnp.concatenate([grads[l]["norm_w"] for l in range(NL)]),
        "fg_bias": jnp.stack([vec(grads[l]["fb_row"], 0, FOX_HEADS) for l in range(NL)]),
        "ssm_conv_b": jnp.concatenate([grads[l]["ssm_conv_b"] for l in range(NL)]),
        "dt_bias": jnp.stack([vec(grads[l]["dtb_row"], HL, SSM_HEADS) for l in range(NL)]),
        "a_log": jnp.stack([vec(grads[l]["alog_row"], HL, SSM_HEADS) for l in range(NL)]),
        "d_skip": jnp.stack([vec(grads[l]["dskip_row"], HL, SSM_HEADS) for l in range(NL)]),
        "ssm_norm_w": jnp.concatenate([grads[l]["ssm_norm_w"] for l in range(NL)]),
        "sc_conv_b": jnp.concatenate([grads[l]["sc_conv_b"] for l in range(NL)]),
        "cf_conv_b": jnp.concatenate([grads[l]["cf_conv_b"] for l in range(NL)]),
        "cf_ln_w": jnp.concatenate([grads[l]["cf_ln_w"] for l in range(NL)]),
        "cf_ln_b": jnp.concatenate([grads[l]["cf_ln_b"] for l in range(NL)]),
        "final_norm_w": d_fnw[0],
        "ssm_conv_w": stack("ssm_conv_w"), "sc_conv_w": stack("sc_conv_w"), "cf_conv_w": stack("cf_conv_w"),
        "b_gate": jnp.stack([jnp.concatenate(grads[l]["b_gate"]) for l in range(NL)]),
    }
    order = _REPL + _SHSM
    full_shapes = [small_grads[n].shape for n in order] + [(1,)]
    sg_blob, _ = _pack([small_grads[n] for n in order] + [loss_part.reshape(1)], f32)
    sg_all = _allgather(sg_blob, name="gather_small_grads")

    def mine(n, full):
        if n in _REPL:
            return full
        sz = a[n].shape[-1]
        return lax.dynamic_slice_in_dim(full, me * sz, sz, axis=full.ndim - 1)

    g_full = _unpack(_sum_slabs(sg_all, name="sum_small"), full_shapes)
    loss = g_full[-1][0]
    g_mine = {n: mine(n, gf).reshape(a[n].shape) for n, gf in zip(order, g_full[:-1])}
    gs1, _ = _pack([g_mine[n] for n in order], f32)
    ws1, _ = _pack([a[n] for n in order], f32)
    ms1, _ = _pack([mom["m_" + n] for n in order], f32)
    vs1, _ = _pack([mom["v_" + n] for n in order], f32)
    sm_out = _adamw(gs1[None], ws1, ms1, vs1, name="adamw_small")
    sm_res = [_unpack(o, [a[n].shape for n in order]) for o in sm_out]

    res = []
    for k in range(4):
        d = dict(zip(_BIG, big_res[k]))
        d.update(zip(order, sm_res[k]))
        res.append(d)
    outs = [loss, dx[None]]
    for k in range(4):
        outs += [res[k][n] for n in wnames]
    return tuple(outs)
```

```python
import functools

import jax
import jax.numpy as jnp
from jax import lax
from jax.experimental import pallas as pl
from jax.experimental.pallas import tpu as pltpu

f32 = jnp.float32
bf16 = jnp.bfloat16
SDS = jax.ShapeDtypeStruct

N_DEV = 8
LANES = 128
VMEM_LIMIT = 48 * 1024 * 1024
EPS = 1e-6
NEG = -1e30

D_MODEL = 2048
FOX_HEADS = 8
FOX_DH = 128
SSM_HEADS = 16
SSM_P = 64
SSM_N = 128
SSM_G = 2
SSM_W = 1024
XBC_W = 1536
SSM_K, SC_K, CF_K = 4, 3, 31
BW = 1024
SSD_L = 128
HL = 8
CONV_HALO = 32

ADAM_LR, ADAM_B1, ADAM_B2, ADAM_EPS, ADAM_WD, ADAM_STEP = 0.001, 0.9, 0.999, 1e-08, 0.01, 10

IN_NAMES = ("q", "k", "v", "f", "ga", "z", "xbc", "dt", "scb", "scc", "scx", "gc", "glua", "glug", "gd")
IN_SIZES = (1024, 1024, 1024, 8, 1024, 1024, 1536, 16, 1024, 1024, 1024, 1024, 1024, 1024, 1024)
N_IN = sum(IN_SIZES)
MAIN_NAMES = ("q", "k", "v", "ga", "z", "scb", "scc", "scx", "gc", "glua", "glug", "gd", "xbc")
MAIN_OFF = {n: 1024 * i for i, n in enumerate(MAIN_NAMES)}
N_MAIN = 12 * 1024 + XBC_W


def _pcall(body, **kw):
    return pl.pallas_call(body, **kw)


def _cp(*sem):
    return pltpu.CompilerParams(dimension_semantics=sem if sem else None, vmem_limit_bytes=VMEM_LIMIT)


def _pick(n, cands):
    for c in cands:
        if c <= n and n % c == 0:
            return c
    return n


_DIMS = {"nn": ((1,), (0,)), "nt": ((1,), (1,)), "tn": ((0,), (0,))}


def _mm(a, b, *, mode, out_dtype, name, add=None, tm=512, tn=None, tk=None):
    if mode == "nn":
        (M, K), (_, N) = a.shape, b.shape
    elif mode == "nt":
        (M, K), (N, _) = a.shape, b.shape
    else:
        (K, M), (_, N) = a.shape, b.shape
    tm = _pick(M, (tm, 256, 128))
    tn = _pick(N, (tn,) if tn else (1536, 1024, 512, 256, 128))
    tk = _pick(K, (tk,) if tk else (2048, 1536, 1024, 512, 256, 128))
    nk = K // tk
    a_spec = (pl.BlockSpec((tk, tm), lambda j, i, k: (k, i)) if mode == "tn"
              else pl.BlockSpec((tm, tk), lambda j, i, k: (i, k)))
    b_spec = (pl.BlockSpec((tn, tk), lambda j, i, k: (j, k)) if mode == "nt"
              else pl.BlockSpec((tk, tn), lambda j, i, k: (k, j)))
    o_spec = pl.BlockSpec((tm, tn), lambda j, i, k: (i, j))
    ins, specs = [a, b], [a_spec, b_spec]
    if add is not None:
        ins.append(add)
        specs.append(o_spec)
    dims = (_DIMS[mode], ((), ()))

    def body(*refs):
        a_ref, b_ref = refs[0], refs[1]
        add_ref = refs[2] if add is not None else None
        o_ref = refs[3] if add is not None else refs[2]
        p = lax.dot_general(a_ref[...], b_ref[...], dims, preferred_element_type=f32)

        def finish(v):
            if add_ref is not None:
                v = v + add_ref[...].astype(f32)
            o_ref[...] = v.astype(o_ref.dtype)

        if nk == 1:
            finish(p)
        else:
            acc = refs[-1]
            k = pl.program_id(2)

            @pl.when(k == 0)
            def _():
                acc[...] = p

            @pl.when(k > 0)
            def _():
                acc[...] += p

            @pl.when(k == nk - 1)
            def _():
                finish(acc[...])

    return _pcall(
        body, name=name, grid=(N // tn, M // tm, nk), in_specs=specs, out_specs=o_spec,
        out_shape=SDS((M, N), out_dtype),
        scratch_shapes=[pltpu.VMEM((tm, tn), f32)] if nk > 1 else [],
        compiler_params=_cp("parallel", "parallel", "arbitrary"),
    )(*ins)


def _silu(x):
    return x * jax.nn.sigmoid(x)


def _softplus(x):
    return jnp.maximum(x, 0.0) + jnp.log(1.0 + jnp.exp(-jnp.abs(x)))


def _mean(x):
    return jnp.mean(x, axis=-1, keepdims=True)


def _f_rms(x, w):
    return (x * lax.rsqrt(_mean(x * x) + EPS) * w,)


def _f_rms_res(x, w):
    return _f_rms(x, w) + (x,)


def _f_ya(o, g):
    return (o * _silu(g),)


def _f_yb(y, z, w):
    t = y * _silu(z)
    return (t * lax.rsqrt(_mean(t * t) + EPS) * w,)


def _f_mul(a, b):
    return (a * b,)


def _f_yc(b, cv, g):
    return (b * cv * _silu(g),)


def _f_glu(a, g):
    return (a * jax.nn.sigmoid(g),)


def _f_yd(cf, g, lw, lb):
    mu = _mean(cf)
    var = _mean(jnp.square(cf - mu))
    ln = (cf - mu) * lax.rsqrt(var + EPS) * lw + lb
    return (_silu(ln) * _silu(g),)


def _f_silu(x):
    return (_silu(x),)


def _f_merge(g0, g1, g2, g3, p0, p1, p2, p3, b0, b1, b2, b3):
    sg = jax.nn.sigmoid
    return (sg(g0 + b0) * p0 + sg(g1 + b1) * p1 + sg(g2 + b2) * p2 + sg(g3 + b3) * p3,)


def _row_specs(ins, params, tt, cw):
    specs = [pl.BlockSpec((tt, cw), functools.partial(lambda j, i, base: (i, base + j), base=base))
             for _, base in ins]
    specs += [pl.BlockSpec((p.shape[0], cw), lambda j, i: (0, j)) for p in params]
    return specs


def _rowwise(fn, ins, params, out_dtypes, *, cw, ncol, name, tt=256):
    T = ins[0][0].shape[0]
    tt = _pick(T, (tt, 128))
    ni, npar = len(ins), len(params)

    def body(*refs):
        xs = [r[...].astype(f32) for r in refs[:ni]]
        ps = [r[...] for r in refs[ni:ni + npar]]
        ys = fn(*xs, *ps)
        for o_ref, y in zip(refs[ni + npar:], ys):
            o_ref[...] = y.astype(o_ref.dtype)

    out_spec = pl.BlockSpec((tt, cw), lambda j, i: (i, j))
    outs = _pcall(
        body, name=name, grid=(ncol, T // tt), in_specs=_row_specs(ins, params, tt, cw),
        out_specs=[out_spec] * len(out_dtypes),
        out_shape=[SDS((T, cw * ncol), dt) for dt in out_dtypes],
        compiler_params=_cp("parallel", "parallel"),
    )(*[a for a, _ in ins], *params)
    return outs


def _rowwise_bwd(fn, ins, params, cots, din_dtypes, *, cw, ncol, name, tt=256):
    T = ins[0][0].shape[0]
    tt = _pick(T, (tt, 128))
    ni, npar, nc = len(ins), len(params), len(cots)
    keep = [k for k, dt in enumerate(din_dtypes) if dt is not None]

    def body(*refs):
        xs = [r[...].astype(f32) for r in refs[:ni]]
        ps = [r[...] for r in refs[ni:ni + npar]]
        cs = tuple(r[...].astype(f32) for r in refs[ni + npar:ni + npar + nc])
        outs = refs[ni + npar + nc:]
        _, vjp = jax.vjp(fn, *xs, *ps)
        g = vjp(cs)
        for o_ref, k in zip(outs[:len(keep)], keep):
            o_ref[...] = g[k].astype(o_ref.dtype)
        i = pl.program_id(1)
        for o_ref, gp in zip(outs[len(keep):], g[ni:]):
            @pl.when(i == 0)
            def _(o_ref=o_ref, gp=gp):
                o_ref[...] = gp

            @pl.when(i > 0)
            def _(o_ref=o_ref, gp=gp):
                o_ref[...] += gp

    row_spec = pl.BlockSpec((tt, cw), lambda j, i: (i, j))
    in_specs = _row_specs(ins, params, tt, cw) + [row_spec] * nc
    out_specs = [row_spec] * len(keep) + [pl.BlockSpec((p.shape[0], cw), lambda j, i: (0, j)) for p in params]
    out_shape = [SDS((T, cw * ncol), din_dtypes[k]) for k in keep] + [SDS(p.shape, f32) for p in params]
    return _pcall(
        body, name=name, grid=(ncol, T // tt), in_specs=in_specs, out_specs=out_specs, out_shape=out_shape,
        compiler_params=_cp("arbitrary", "arbitrary"),
    )(*[a for a, _ in ins], *params, *cots)


CONV_RB = 64
SUBLANES = 8


def _shifted_rows(buf, n):
    for sft in range(1, SUBLANES):
        buf[sft, 0:n, :] = buf[0, pl.ds(sft, n), :]


def _tap(buf, o, rb, shifted):
    if shifted:
        return buf[o % SUBLANES, o - o % SUBLANES:o - o % SUBLANES + rb, :]
    return buf[0, pl.ds(o, rb), :]


def _conv_fwd(x, xbase, C, w, b, *, with_silu, name, out_dtype=bf16, tt=512, cw=512):
    xa = x
    T = xa.shape[0]
    K = w.shape[0]
    tt = _pick(T, (tt,))
    hp = CONV_HALO
    r = tt // hp
    ncol = C // cw
    rb = min(CONV_RB, tt)
    n = hp + tt
    shifted = K > SUBLANES

    def body(cur_ref, halo_ref, w_ref, b_ref, *rest):
        outs, ext = rest[:-1], rest[-1]
        i = pl.program_id(1)
        ext[0, 0:hp, :] = jnp.where(i > 0, halo_ref[...].astype(f32), 0.0)
        ext[0, hp:n, :] = cur_ref[...].astype(f32)
        ext[0, n:, :] = jnp.zeros((SUBLANES, cw), f32)
        if shifted:
            _shifted_rows(ext, n)
        for r0 in range(0, tt, rb):
            acc = jnp.zeros((rb, cw), f32) + b_ref[...]
            for k in range(K):
                acc = acc + w_ref[k:k + 1, :] * _tap(ext, hp - (K - 1) + k + r0, rb, shifted)
            outs[0][r0:r0 + rb, :] = acc.astype(outs[0].dtype)
            if with_silu:
                outs[1][r0:r0 + rb, :] = _silu(acc).astype(outs[1].dtype)

    n_out = 2 if with_silu else 1
    o_spec = pl.BlockSpec((tt, cw), lambda j, i: (i, j))
    return _pcall(
        body, name=name, grid=(ncol, T // tt),
        in_specs=[pl.BlockSpec((tt, cw), lambda j, i: (i, xbase + j)),
                  pl.BlockSpec((hp, cw), lambda j, i: (jnp.maximum(i * r - 1, 0), xbase + j)),
                  pl.BlockSpec((K, cw), lambda j, i: (0, j)),
                  pl.BlockSpec((1, cw), lambda j, i: (0, j))],
        out_specs=[o_spec] * n_out, out_shape=[SDS((T, C), out_dtype)] * n_out,
        scratch_shapes=[pltpu.VMEM((SUBLANES if shifted else 1, n + SUBLANES, cw), f32)],
        compiler_params=_cp("parallel", "parallel"),
    )(xa, xa, w, b)


def _conv_bwd(dy, x, xbase, C, w, *, name, dx_dtype=bf16, tt=512, cw=256):
    T = dy.shape[0]
    K = w.shape[0]
    tt = _pick(T, (tt,))
    hp = CONV_HALO
    r = tt // hp
    nt = T // tt
    ncol = C // cw
    rb = min(CONV_RB, tt)
    n = tt + hp
    shifted = K > SUBLANES

    def fold8(v):
        out = v[0:SUBLANES]
        for g in range(1, rb // SUBLANES):
            out = out + v[SUBLANES * g:SUBLANES * (g + 1)]
        return out

    def body(dy_ref, dyn_ref, x_ref, xh_ref, w_ref, dx_ref, dw_ref, db_ref, exd, exx, dwp):
        i = pl.program_id(1)
        exd[0, 0:tt, :] = dy_ref[...].astype(f32)
        exd[0, tt:n, :] = jnp.where(i < nt - 1, dyn_ref[...].astype(f32), 0.0)
        exd[0, n:, :] = jnp.zeros((SUBLANES, cw), f32)
        exx[0, 0:hp, :] = jnp.where(i > 0, xh_ref[...].astype(f32), 0.0)
        exx[0, hp:n, :] = x_ref[...].astype(f32)
        exx[0, n:, :] = jnp.zeros((SUBLANES, cw), f32)
        if shifted:
            _shifted_rows(exd, n)
            _shifted_rows(exx, n)

        @pl.when(i == 0)
        def _():
            dwp[...] = jnp.zeros_like(dwp)

        for r0 in range(0, tt, rb):
            dyc = exd[0, r0:r0 + rb, :]
            acc = jnp.zeros((rb, cw), f32)
            for k in range(K):
                acc = acc + w_ref[k:k + 1, :] * _tap(exd, K - 1 - k + r0, rb, shifted)
                dwp[SUBLANES * k:SUBLANES * (k + 1), :] += fold8(dyc * _tap(exx, hp - (K - 1) + k + r0, rb, shifted))
            dx_ref[r0:r0 + rb, :] = acc.astype(dx_ref.dtype)
            dwp[SUBLANES * K:SUBLANES * (K + 1), :] += fold8(dyc)

        @pl.when(i == nt - 1)
        def _():
            for k in range(K):
                dw_ref[k:k + 1, :] = jnp.sum(dwp[SUBLANES * k:SUBLANES * (k + 1), :], axis=0, keepdims=True)
            db_ref[...] = jnp.sum(dwp[SUBLANES * K:SUBLANES * (K + 1), :], axis=0, keepdims=True)

    nsh = SUBLANES if shifted else 1
    return _pcall(
        body, name=name, grid=(ncol, nt),
        in_specs=[pl.BlockSpec((tt, cw), lambda j, i: (i, j)),
                  pl.BlockSpec((hp, cw), lambda j, i: (jnp.minimum((i + 1) * r, nt * r - 1), j)),
                  pl.BlockSpec((tt, cw), lambda j, i: (i, xbase + j)),
                  pl.BlockSpec((hp, cw), lambda j, i: (jnp.maximum(i * r - 1, 0), xbase + j)),
                  pl.BlockSpec((K, cw), lambda j, i: (0, j))],
        out_specs=[pl.BlockSpec((tt, cw), lambda j, i: (i, j)),
                   pl.BlockSpec((K, cw), lambda j, i: (0, j)),
                   pl.BlockSpec((1, cw), lambda j, i: (0, j))],
        out_shape=[SDS((T, C), dx_dtype), SDS((K, C), f32), SDS((1, C), f32)],
        scratch_shapes=[pltpu.VMEM((nsh, n + SUBLANES, cw), f32), pltpu.VMEM((nsh, n + SUBLANES, cw), f32),
                        pltpu.VMEM((SUBLANES * (K + 1), cw), f32)],
        compiler_params=_cp("arbitrary", "arbitrary"),
    )(dy, dy, x, x, w)


def _tri(n, lower):
    r = lax.broadcasted_iota(jnp.int32, (n, n), 0)
    c = lax.broadcasted_iota(jnp.int32, (n, n), 1)
    return jnp.where((r >= c) if lower else (r <= c), 1.0, 0.0).astype(f32)


def _hdot(a, b):
    return jnp.dot(a, b, precision=lax.Precision.HIGHEST, preferred_element_type=f32)


def _fox_gate_fwd(small, fb_row, *, name, tt=512):
    T = small.shape[0]
    tt = _pick(T, (tt,))

    def body(sm_ref, fb_ref, c_ref, carry):
        @pl.when(pl.program_id(0) == 0)
        def _():
            carry[...] = jnp.zeros_like(carry)

        logf = -_softplus(-(sm_ref[...] + fb_ref[...]))
        cs = _hdot(_tri(tt, True), logf) + carry[...]
        c_ref[...] = cs * LOG2E
        carry[...] = cs[tt - 1:tt, :]

    blk = pl.BlockSpec((tt, LANES), lambda i: (i, 0))
    return _pcall(
        body, name=name, grid=(T // tt,), in_specs=[blk, pl.BlockSpec((1, LANES), lambda i: (0, 0))],
        out_specs=blk, out_shape=SDS((T, LANES), f32), scratch_shapes=[pltpu.VMEM((1, LANES), f32)],
        compiler_params=_cp("arbitrary"),
    )(small, fb_row)


def _fox_gate_bwd(dc, small, fb_row, dsmall_dt, *, name, tt=512):
    T = small.shape[0]
    tt = _pick(T, (tt,))
    nt = T // tt

    def body(dc_ref, sm_ref, fb_ref, dd_ref, ds_ref, dfb_ref, carry):
        @pl.when(pl.program_id(0) == 0)
        def _():
            carry[...] = jnp.zeros_like(carry)
            dfb_ref[...] = jnp.zeros_like(dfb_ref)

        dl = _hdot(_tri(tt, False), dc_ref[...]) + carry[...]
        carry[...] = dl[0:1, :]
        lane = lax.broadcasted_iota(jnp.int32, (tt, LANES), 1)
        df = jnp.where(lane < FOX_HEADS, dl * jax.nn.sigmoid(-(sm_ref[...] + fb_ref[...])), 0.0)
        ds_ref[...] = jnp.where(lane < FOX_HEADS, df, dd_ref[...])
        dfb_ref[...] += jnp.sum(df, axis=0, keepdims=True)

    blk = pl.BlockSpec((tt, LANES), lambda i: (nt - 1 - i, 0))
    row = pl.BlockSpec((1, LANES), lambda i: (0, 0))
    return _pcall(
        body, name=name, grid=(nt,), in_specs=[blk, blk, row, blk], out_specs=[blk, row],
        out_shape=[SDS((T, LANES), f32), SDS((1, LANES), f32)], scratch_shapes=[pltpu.VMEM((1, LANES), f32)],
        compiler_params=_cp("arbitrary"),
    )(dc, small, fb_row, dsmall_dt)


_NT = (((1,), (1,)), ((), ()))
_TN = (((0,), (0,)), ((), ()))


def _causal(n):
    r = lax.broadcasted_iota(jnp.int32, (n, n), 0)
    c = lax.broadcasted_iota(jnp.int32, (n, n), 1)
    return r >= c


LOG2E = 1.4426950408889634
_SCALE = FOX_DH ** -0.5
_SCALE2 = _SCALE * LOG2E
CW = 256
FLASH_TQ = 1024
FLASH_CH = 1024
FLASH_TK = 1024


def _to_t(a, w):
    T = a.shape[0]
    return a.reshape(T // w, w, FOX_HEADS, FOX_DH).transpose(2, 0, 3, 1)


def _from_t(at):
    hh, n, dh, w = at.shape
    return at.transpose(1, 3, 0, 2).reshape(n * w, hh * dh)


def _flash_fwd(u, qT, vT, ck2, *, name):
    T = u.shape[0]
    H, dh = FOX_HEADS, FOX_DH
    TQ, CH = min(FLASH_TQ, T), min(FLASH_CH, T)
    NC, NQ, RQ, RC, R = T // CW, T // TQ, TQ // CW, CH // CW, TQ // CH
    ko = MAIN_OFF["k"] // dh

    def body(qT_ref, k_ref, vT_ref, ck_ref, oT_ref, lse_ref):
        i = pl.program_id(1)
        qt = jnp.concatenate([qT_ref[r] for r in range(RQ)], axis=1) if RQ > 1 else qT_ref[0]

        def scores(j):
            st = pl.multiple_of(j * CH, CH)
            kc = k_ref[pl.ds(st, CH), :]
            return jnp.dot(kc, qt, preferred_element_type=f32) * _SCALE2 - ck_ref[pl.ds(st, CH), :]

        def update(j, s, carry):
            m, l, acc = carry
            m_new = jnp.maximum(m, jnp.max(s, axis=0, keepdims=True))
            alpha = jnp.exp2(m - m_new)
            p = jnp.exp2(s - m_new)
            l = alpha * l + jnp.sum(p, axis=0, keepdims=True)
            pb = p.astype(bf16)
            pv = jnp.dot(vT_ref[j * RC], pb[0:CW], preferred_element_type=f32)
            for r in range(1, RC):
                pv = pv + jnp.dot(vT_ref[j * RC + r], pb[r * CW:(r + 1) * CW], preferred_element_type=f32)
            return m_new, l, alpha * acc + pv

        def band(s, d):
            r = lax.broadcasted_iota(jnp.int32, (CH, TQ), 0) + d * CH
            c = lax.broadcasted_iota(jnp.int32, (CH, TQ), 1)
            return jnp.where(r <= c, s, NEG)

        carry = (jnp.full((1, TQ), NEG, f32), jnp.zeros((1, TQ), f32), jnp.zeros((dh, TQ), f32))
        carry = lax.fori_loop(0, i * R, lambda j, c: update(j, scores(j), c), carry)
        for d in range(R):
            carry = update(i * R + d, band(scores(i * R + d), d), carry)
        m, l, acc = carry
        o = (acc / l).astype(oT_ref.dtype)
        lse = m + jnp.log2(l)
        for r in range(RQ):
            oT_ref[r] = o[:, r * CW:(r + 1) * CW]
            lse_ref[r] = lse[:, r * CW:(r + 1) * CW]

    return _pcall(
        body, name=name, grid=(H, NQ),
        in_specs=[pl.BlockSpec((None, RQ, dh, CW), lambda h, i: (h, i, 0, 0)),
                  pl.BlockSpec((T, dh), lambda h, i: (0, ko + h)),
                  pl.BlockSpec((None, NC, dh, CW), lambda h, i: (h, 0, 0, 0)),
                  pl.BlockSpec((None, T, 1), lambda h, i: (h, 0, 0))],
        out_specs=[pl.BlockSpec((None, RQ, dh, CW), lambda h, i: (h, i, 0, 0)),
                   pl.BlockSpec((None, RQ, 1, CW), lambda h, i: (h, i, 0, 0))],
        out_shape=[SDS((H, NC, dh, CW), bf16), SDS((H, NC, 1, CW), f32)],
        compiler_params=_cp("parallel", "arbitrary"),
    )(qT, u, vT, ck2)


def _flash_delta(doT, oT, *, name):
    H, NC, dh, _ = doT.shape

    def body(a_ref, b_ref, d_ref):
        d_ref[...] = jnp.sum(a_ref[...].astype(f32) * b_ref[...].astype(f32), axis=0, keepdims=True)

    blk = pl.BlockSpec((None, None, dh, CW), lambda h, i: (h, i, 0, 0))
    return _pcall(
        body, name=name, grid=(H, NC), in_specs=[blk, blk],
        out_specs=pl.BlockSpec((None, None, 1, CW), lambda h, i: (h, i, 0, 0)),
        out_shape=SDS((H, NC, 1, CW), f32), compiler_params=_cp("parallel", "parallel"),
    )(doT, oT)


def _flash_bwd(u, qT, kT, do, doT, ck2, lse2, delta, *, name):
    T = u.shape[0]
    H, dh = FOX_HEADS, FOX_DH
    TK = min(FLASH_TK, T)
    NC, NK, R = T // CW, T // TK, TK // CW
    qo, ko, vo = MAIN_OFF["q"] // dh, MAIN_OFF["k"] // dh, MAIN_OFF["v"] // dh

    def body(q_ref, k_ref, v_ref, qT_ref, kT_ref, do_ref, doT_ref, ck_ref, lse_ref, dl_ref,
             dk_ref, dv_ref, dck_ref, dqT_ref, dcq_ref, dk_s, dv_s, dc_s):
        j = pl.program_id(1)

        @pl.when(j == 0)
        def _():
            dqT_ref[...] = jnp.zeros_like(dqT_ref)
            dcq_ref[...] = jnp.zeros_like(dcq_ref)

        dk_s[...] = jnp.zeros_like(dk_s)
        dv_s[...] = jnp.zeros_like(dv_s)
        dc_s[...] = jnp.zeros_like(dc_s)
        k, v, kt = k_ref[...], v_ref[...], kT_ref[...]
        ckc = ck_ref[...]

        def chunk(i, d):
            st = pl.multiple_of(i * CW, CW)
            s = jnp.dot(k, qT_ref[i], preferred_element_type=f32) * _SCALE2 - ckc
            if d is not None:
                r = lax.broadcasted_iota(jnp.int32, (TK, CW), 0)
                c = lax.broadcasted_iota(jnp.int32, (TK, CW), 1) + d * CW
                s = jnp.where(r <= c, s, NEG)
            p = jnp.exp2(s - lse_ref[i])
            dp = jnp.dot(v, doT_ref[i], preferred_element_type=f32)
            ds = p * (dp - dl_ref[i])
            dsb = ds.astype(bf16)
            dv_s[...] += jnp.dot(p.astype(bf16), do_ref[pl.ds(st, CW), :], preferred_element_type=f32)
            dk_s[...] += jnp.dot(dsb, q_ref[pl.ds(st, CW), :], preferred_element_type=f32)
            dqT_ref[i] += jnp.dot(kt, dsb, preferred_element_type=f32)
            dcq_ref[i] += jnp.sum(ds, axis=0, keepdims=True)
            part = ds[:, 0:LANES]
            for t in range(1, CW // LANES):
                part = part + ds[:, t * LANES:(t + 1) * LANES]
            dc_s[...] += part

        for d in range(R):
            chunk(j * R + d, d)

        def rest(i, c):
            chunk(i, None)
            return c

        lax.fori_loop(j * R + R, NC, rest, 0)
        dk_ref[...] = (dk_s[...] * _SCALE).astype(dk_ref.dtype)
        dv_ref[...] = dv_s[...].astype(dv_ref.dtype)
        dck_ref[...] = -jnp.sum(dc_s[...], axis=1, keepdims=True)

        @pl.when(j == NK - 1)
        def _():
            dqT_ref[...] = dqT_ref[...] * _SCALE

    res = lambda h, j: (h, 0, 0, 0)
    return _pcall(
        body, name=name, grid=(H, NK),
        in_specs=[pl.BlockSpec((T, dh), lambda h, j: (0, qo + h)),
                  pl.BlockSpec((TK, dh), lambda h, j: (j, ko + h)),
                  pl.BlockSpec((TK, dh), lambda h, j: (j, vo + h)),
                  pl.BlockSpec((None, NC, dh, CW), res),
                  pl.BlockSpec((None, None, dh, TK), lambda h, j: (h, j, 0, 0)),
                  pl.BlockSpec((T, dh), lambda h, j: (0, h)),
                  pl.BlockSpec((None, NC, dh, CW), res),
                  pl.BlockSpec((None, TK, 1), lambda h, j: (h, j, 0)),
                  pl.BlockSpec((None, NC, 1, CW), res),
                  pl.BlockSpec((None, NC, 1, CW), res)],
        out_specs=[pl.BlockSpec((TK, dh), lambda h, j: (j, h)),
                   pl.BlockSpec((TK, dh), lambda h, j: (j, h)),
                   pl.BlockSpec((None, TK, 1), lambda h, j: (h, j, 0)),
                   pl.BlockSpec((None, NC, dh, CW), res),
                   pl.BlockSpec((None, NC, 1, CW), res)],
        out_shape=[SDS((T, H * dh), bf16), SDS((T, H * dh), bf16), SDS((H, T, 1), f32),
                   SDS((H, NC, dh, CW), f32), SDS((H, NC, 1, CW), f32)],
        scratch_shapes=[pltpu.VMEM((TK, dh), f32), pltpu.VMEM((TK, dh), f32), pltpu.VMEM((TK, LANES), f32)],
        compiler_params=_cp("parallel", "arbitrary"),
    )(u, u, u, qT, kT, do, doT, ck2, lse2, delta)


def _ssd_common(sm_ref, dtb_ref, alog_ref):
    L = SSD_L
    dt = _softplus(sm_ref[...] + dtb_ref[...])
    a = -jnp.exp(alog_ref[...])
    cs = _hdot(_tri(L, True), dt * a)
    return dt, a, cs, cs.T, cs[L - 1:L, :]


def _ssd_fwd(xbc, small, dtb_row, alog_row, dskip_row, *, name):
    T = xbc.shape[0]
    L, P, N = SSD_L, SSM_P, SSM_N
    nc = T // L
    hpg = SSM_HEADS // SSM_G

    def body(x_ref, sm_ref, dtb_ref, alog_ref, dsk_ref, y_ref, hs_ref, h_s):
        @pl.when(pl.program_id(0) == 0)
        def _():
            h_s[...] = jnp.zeros_like(h_s)

        dt, a, cs, csT, tot = _ssd_common(sm_ref, dtb_ref, alog_ref)
        hs_ref[...] = h_s[...]
        causal = _causal(L)
        for g in range(SSM_G):
            bg = x_ref[:, SSM_W + g * N:SSM_W + (g + 1) * N]
            cg = x_ref[:, SSM_W + SSM_G * N + g * N:SSM_W + SSM_G * N + (g + 1) * N]
            cb = lax.dot_general(cg, bg, _NT, preferred_element_type=f32)
            hg = h_s[g * hpg * P:(g + 1) * hpg * P, :]
            yoff = lax.dot_general(cg, hg.astype(bf16), _NT, preferred_element_type=f32)
            for hh in range(hpg):
                h = g * hpg + hh
                ln = HL + h
                acol, arow = cs[:, ln:ln + 1], csT[ln:ln + 1, :]
                w = jnp.where(causal, cb * jnp.exp(jnp.minimum(acol - arow, 0.0)), 0.0)
                xs = x_ref[:, h * P:(h + 1) * P].astype(f32)
                xh = xs * dt[:, ln:ln + 1]
                y = jnp.dot(w.astype(bf16), xh.astype(bf16), preferred_element_type=f32)
                y = y + jnp.exp(acol) * yoff[:, hh * P:(hh + 1) * P] + xs * dsk_ref[:, ln:ln + 1]
                y_ref[:, h * P:(h + 1) * P] = y.astype(y_ref.dtype)
                decay = jnp.exp(tot[:, ln:ln + 1] - acol)
                st = lax.dot_general((xh * decay).astype(bf16), bg, _TN, preferred_element_type=f32)
                h_s[h * P:(h + 1) * P, :] = jnp.exp(tot[:, ln:ln + 1]) * h_s[h * P:(h + 1) * P, :] + st

    row = pl.BlockSpec((1, LANES), lambda c: (0, 0))
    return _pcall(
        body, name=name, grid=(nc,),
        in_specs=[pl.BlockSpec((L, XBC_W), lambda c: (c, 0)), pl.BlockSpec((L, LANES), lambda c: (c, 0)),
                  row, row, row],
        out_specs=[pl.BlockSpec((L, SSM_W), lambda c: (c, 0)),
                   pl.BlockSpec((None, SSM_HEADS * P, N), lambda c: (c, 0, 0))],
        out_shape=[SDS((T, SSM_W), bf16), SDS((nc, SSM_HEADS * P, N), f32)],
        scratch_shapes=[pltpu.VMEM((SSM_HEADS * P, N), f32)],
        compiler_params=_cp("arbitrary"),
    )(xbc, small, dtb_row, alog_row, dskip_row)


def _ssd_bwd(dy, xbc, small, dtb_row, alog_row, dskip_row, hstates, *, name):
    T = xbc.shape[0]
    L, P, N = SSD_L, SSM_P, SSM_N
    nc = T // L
    hpg = SSM_HEADS // SSM_G
    GW = hpg * P

    def body(dy_ref, x_ref, sm_ref, dtb_ref, alog_ref, dsk_ref, hs_ref,
             dx_ref, dsm_ref, ddtb_ref, dalog_ref, ddsk_ref, dh_s, ea_s, dx_s):
        @pl.when(pl.program_id(0) == 0)
        def _():
            dh_s[...] = jnp.zeros_like(dh_s)
            ddtb_ref[...] = jnp.zeros_like(ddtb_ref)
            dalog_ref[...] = jnp.zeros_like(dalog_ref)
            ddsk_ref[...] = jnp.zeros_like(ddsk_ref)

        dt, a, cs, csT, tot = _ssd_common(sm_ref, dtb_ref, alog_ref)
        causal = _causal(L)
        lane = lax.broadcasted_iota(jnp.int32, (L, LANES), 1)
        sub = lax.broadcasted_iota(jnp.int32, (LANES, L), 0)
        lane1 = lax.broadcasted_iota(jnp.int32, (1, LANES), 1)
        da_col = jnp.zeros((L, LANES), f32)
        da_row = jnp.zeros((LANES, L), f32)
        da_tot = jnp.zeros((1, LANES), f32)
        ddt = jnp.zeros((L, LANES), f32)
        ddsk = jnp.zeros((1, LANES), f32)
        for g in range(SSM_G):
            bo, co = SSM_W + g * N, SSM_W + SSM_G * N + g * N
            bg, cg = x_ref[:, bo:bo + N], x_ref[:, co:co + N]
            cb = lax.dot_general(cg, bg, _NT, preferred_element_type=f32)
            hg = hs_ref[g * GW:(g + 1) * GW, :]
            hgb = hg.astype(bf16)
            dhn = dh_s[g * GW:(g + 1) * GW, :]
            dhnb = dhn.astype(bf16)
            yoff = lax.dot_general(cg, hgb, _NT, preferred_element_type=f32)
            bdh = lax.dot_general(bg, dhnb, _NT, preferred_element_type=f32)
            dcb = jnp.zeros((L, L), f32)
            for hh in range(hpg):
                h = g * hpg + hh
                ln = HL + h
                sl = slice(hh * P, (hh + 1) * P)
                acol, arow = cs[:, ln:ln + 1], csT[ln:ln + 1, :]
                e = jnp.where(causal, jnp.exp(jnp.minimum(acol - arow, 0.0)), 0.0)
                w = cb * e
                xs = x_ref[:, h * P:(h + 1) * P].astype(f32)
                dth = dt[:, ln:ln + 1]
                xh = xs * dth
                dyh = dy_ref[:, h * P:(h + 1) * P].astype(f32)
                dyb = dyh.astype(bf16)
                ea = jnp.exp(acol)
                decay = jnp.exp(tot[:, ln:ln + 1] - acol)
                etot = jnp.exp(tot[:, ln:ln + 1])
                dw = lax.dot_general(dyb, xh.astype(bf16), _NT, preferred_element_type=f32)
                m = dw * w
                dcb = dcb + dw * e
                dxs_state = decay * bdh[:, sl]
                dxh = lax.dot_general(w.astype(bf16), dyb, _TN, preferred_element_type=f32) + dxs_state
                r = jnp.sum(xh * dxs_state, axis=1, keepdims=True)
                col = (jnp.sum(m, axis=1, keepdims=True) - r
                       + jnp.sum(dyh * (ea * yoff[:, sl]), axis=1, keepdims=True))
                da_col = da_col + jnp.where(lane == ln, col, 0.0)
                da_row = da_row + jnp.where(sub == ln, jnp.sum(m, axis=0, keepdims=True), 0.0)
                hprod = jnp.sum(jnp.sum(dhn[sl, :] * hg[sl, :], axis=1, keepdims=True), axis=0, keepdims=True)
                t_h = jnp.sum(r, axis=0, keepdims=True) + etot * hprod
                da_tot = da_tot + jnp.where(lane1 == ln, t_h, 0.0)
                ea_s[:, sl] = (ea * dyh).astype(bf16)
                dx_s[:, sl] = (decay * xh).astype(bf16)
                dx_ref[:, h * P:(h + 1) * P] = (dxh * dth + dyh * dsk_ref[:, ln:ln + 1]).astype(dx_ref.dtype)
                ddt = ddt + jnp.where(lane == ln, jnp.sum(dxh * xs, axis=1, keepdims=True), 0.0)
                ddsk = ddsk + jnp.where(
                    lane1 == ln, jnp.sum(jnp.sum(dyh * xs, axis=1, keepdims=True), axis=0, keepdims=True), 0.0)
                dh_s[g * GW + hh * P:g * GW + (hh + 1) * P, :] = etot * dhn[sl, :]
            dcbb = dcb.astype(bf16)
            eab, dxb = ea_s[...], dx_s[...]
            dcg = (jnp.dot(dcbb, bg, preferred_element_type=f32)
                   + jnp.dot(eab, hgb, preferred_element_type=f32))
            dbg = (lax.dot_general(dcbb, cg, _TN, preferred_element_type=f32)
                   + jnp.dot(dxb, dhnb, preferred_element_type=f32))
            dx_ref[:, bo:bo + N] = dbg.astype(dx_ref.dtype)
            dx_ref[:, co:co + N] = dcg.astype(dx_ref.dtype)
            dh_s[g * GW:(g + 1) * GW, :] += lax.dot_general(eab, cg, _TN, preferred_element_type=f32)

        rowi = lax.broadcasted_iota(jnp.int32, (L, LANES), 0)
        da_cs = da_col - da_row.T + jnp.where(rowi == L - 1, da_tot, 0.0)
        dda = _hdot(_tri(L, False), da_cs)
        ddt = ddt + dda * a
        heads = (lane >= HL) & (lane < HL + SSM_HEADS)
        draw = jnp.where(heads, ddt * jax.nn.sigmoid(sm_ref[...] + dtb_ref[...]), 0.0)
        dsm_ref[...] = draw
        ddtb_ref[...] += jnp.sum(draw, axis=0, keepdims=True)
        dalog_ref[...] += jnp.where(lane1 >= HL, jnp.sum(dda * dt, axis=0, keepdims=True) * a, 0.0)
        ddsk_ref[...] += ddsk

    rev = lambda c: (nc - 1 - c, 0)
    row = pl.BlockSpec((1, LANES), lambda c: (0, 0))
    return _pcall(
        body, name=name, grid=(nc,),
        in_specs=[pl.BlockSpec((L, SSM_W), rev), pl.BlockSpec((L, XBC_W), rev), pl.BlockSpec((L, LANES), rev),
                  row, row, row, pl.BlockSpec((None, SSM_HEADS * P, N), lambda c: (nc - 1 - c, 0, 0))],
        out_specs=[pl.BlockSpec((L, XBC_W), rev), pl.BlockSpec((L, LANES), rev), row, row, row],
        out_shape=[SDS((T, XBC_W), bf16), SDS((T, LANES), f32)] + [SDS((1, LANES), f32)] * 3,
        scratch_shapes=[pltpu.VMEM((SSM_HEADS * P, N), f32), pltpu.VMEM((L, GW), bf16), pltpu.VMEM((L, GW), bf16)],
        compiler_params=_cp("arbitrary"),
    )(dy, xbc, small, dtb_row, alog_row, dskip_row, hstates)


def _loss_head(x, w, target, *, name, tt=256):
    T, D = x.shape
    tt = _pick(T, (tt, 128))

    def fn(x, w, tg):
        (y,) = _f_rms(x, w)
        return 0.5 * jnp.sum(jnp.sum(jnp.square(y - tg), axis=1, keepdims=True), axis=0, keepdims=True) / D

    def body(x_ref, w_ref, t_ref, loss_ref, dx_ref, dw_ref):
        i = pl.program_id(0)
        ls, vjp = jax.vjp(fn, x_ref[...], w_ref[...], t_ref[...])
        dx, dw, _ = vjp(jnp.ones((1, 1), f32))
        dx_ref[...] = dx

        @pl.when(i == 0)
        def _():
            loss_ref[...] = ls
            dw_ref[...] = dw

        @pl.when(i > 0)
        def _():
            loss_ref[...] += ls
            dw_ref[...] += dw

    blk = pl.BlockSpec((tt, D), lambda i: (i, 0))
    row = pl.BlockSpec((1, D), lambda i: (0, 0))
    return _pcall(
        body, name=name, grid=(T // tt,), in_specs=[blk, row, blk],
        out_specs=[pl.BlockSpec((1, 1), lambda i: (0, 0)), blk, row],
        out_shape=[SDS((1, 1), f32), SDS((T, D), f32), SDS((1, D), f32)],
        compiler_params=_cp("arbitrary"),
    )(x, w, target)


ADAM_BLOCK_ELEMS = 256 * 1024


def _row_tile(rows, cols):
    for t in (2048, 1024, 512, 256, 128, 64, 32, 16, 8):
        if t * cols <= ADAM_BLOCK_ELEMS and rows % t == 0:
            return t
    return rows


def _adamw(gs, w, m, v, *, name):
    ns, R, C = gs.shape
    tr = _row_tile(R, C)
    c1 = 1.0 / (1.0 - ADAM_B1 ** ADAM_STEP)
    c2 = 1.0 / (1.0 - ADAM_B2 ** ADAM_STEP)

    def body(g_ref, w_ref, m_ref, v_ref, go_ref, d_ref, mo_ref, vo_ref):
        g = g_ref[0].astype(f32)
        for s in range(1, ns):
            g = g + g_ref[s].astype(f32)
        mn = ADAM_B1 * m_ref[...] + (1.0 - ADAM_B1) * g
        vn = ADAM_B2 * v_ref[...] + (1.0 - ADAM_B2) * (g * g)
        go_ref[...] = g
        mo_ref[...] = mn
        vo_ref[...] = vn
        d_ref[...] = -ADAM_LR * ((mn * c1) / (jnp.sqrt(vn * c2) + ADAM_EPS) + ADAM_WD * w_ref[...])

    blk = pl.BlockSpec((tr, C), lambda i: (i, 0))
    return _pcall(
        body, name=name, grid=(R // tr,),
        in_specs=[pl.BlockSpec((ns, tr, C), lambda i: (0, i, 0)), blk, blk, blk],
        out_specs=[blk] * 4, out_shape=[SDS((R, C), f32)] * 4,
        compiler_params=_cp("parallel"),
    )(gs, w, m, v)


def _sum_slabs(gs, *, name):
    ns, R, C = gs.shape
    tr = _row_tile(R, C)

    def body(g_ref, o_ref):
        g = g_ref[0]
        for s in range(1, ns):
            g = g + g_ref[s]
        o_ref[...] = g

    return _pcall(
        body, name=name, grid=(R // tr,), in_specs=[pl.BlockSpec((ns, tr, C), lambda i: (0, i, 0))],
        out_specs=pl.BlockSpec((tr, C), lambda i: (i, 0)), out_shape=SDS((R, C), f32),
        compiler_params=_cp("parallel"),
    )(gs)


_MESH = pl.DeviceIdType.MESH
_HBM = pl.BlockSpec(memory_space=pltpu.HBM)


def _allgather(arrs, *, name):
    n = len(arrs)

    def body(*refs):
        x_refs, out_refs = refs[:n], refs[n:2 * n]
        send_sems, recv_sems, local_sems = refs[2 * n:]
        x, y, c = lax.axis_index("x"), lax.axis_index("y"), lax.axis_index("c")
        me, sibling = (x, y, c), (x, y, 1 - c)
        chips = [(1 - x, y), (x, 1 - y), (1 - x, 1 - y)]

        def copy(a, k, block, to, src=None):
            px, py, pc = block
            slot = out_refs[a].at[4 * px + 2 * py + pc]
            return pltpu.make_async_remote_copy(
                src_ref=slot if src is None else src, dst_ref=slot,
                send_sem=send_sems.at[7 * a + k], recv_sem=recv_sems.at[7 * a + k],
                device_id=to, device_id_type=_MESH)

        mine, first, passed = [], [], []
        for a in range(n):
            cp = pltpu.make_async_copy(x_refs[a], out_refs[a].at[4 * x + 2 * y + c], local_sems.at[a])
            cp.start()
            mine.append(cp)
            first.append(copy(a, 0, me, sibling, src=x_refs[a]))
            first += [copy(a, 1 + j, me, (*chip, c), src=x_refs[a]) for j, chip in enumerate(chips)]
        for cp in first:
            cp.start()
        for j, chip in enumerate(chips):
            for a in range(n):
                copy(a, 1 + j, (*chip, c), me).wait_recv()
                cp = copy(a, 4 + j, (*chip, c), sibling)
                cp.start()
                passed.append(cp)
        for a in range(n):
            copy(a, 0, sibling, me).wait_recv()
            for j, chip in enumerate(chips):
                copy(a, 4 + j, (*chip, 1 - c), me).wait_recv()
        for cp in first + passed:
            cp.wait_send()
        for cp in mine:
            cp.wait()

    return _pcall(
        body, name=name, in_specs=[_HBM] * n, out_specs=[_HBM] * n,
        out_shape=[SDS((N_DEV,) + a.shape, a.dtype) for a in arrs],
        scratch_shapes=[pltpu.SemaphoreType.DMA((7 * n,)), pltpu.SemaphoreType.DMA((7 * n,)),
                        pltpu.SemaphoreType.DMA((n,))],
    )(*arrs)


def _exchange(arrs, *, name):
    n = len(arrs)

    def body(*refs):
        in_refs, out_refs = refs[:n], refs[n:2 * n]
        send_sems, recv_sems, local_sems = refs[2 * n:]
        x, y, c = lax.axis_index("x"), lax.axis_index("y"), lax.axis_index("c")
        me = 4 * x + 2 * y + c
        mine, sends, recvs = [], [], []
        for a in range(n):
            cp = pltpu.make_async_copy(in_refs[a].at[me], out_refs[a].at[me], local_sems.at[a])
            cp.start()
            mine.append(cp)
        for k in range(1, N_DEV):
            px, py, pc = x ^ ((k >> 2) & 1), y ^ ((k >> 1) & 1), c ^ (k & 1)
            peer = 4 * px + 2 * py + pc
            for a in range(n):
                def copy(src_slot, dst_slot):
                    return pltpu.make_async_remote_copy(
                        src_ref=in_refs[a].at[src_slot], dst_ref=out_refs[a].at[dst_slot],
                        send_sem=send_sems.at[7 * a + k - 1], recv_sem=recv_sems.at[7 * a + k - 1],
                        device_id=(px, py, pc), device_id_type=_MESH)

                sends.append(copy(peer, me))
                recvs.append(copy(me, peer))
                sends[-1].start()
        for cp in recvs:
            cp.wait_recv()
        for cp in sends:
            cp.wait_send()
        for cp in mine:
            cp.wait()

    return _pcall(
        body, name=name, in_specs=[_HBM] * n, out_specs=[_HBM] * n,
        out_shape=[SDS(a.shape, a.dtype) for a in arrs],
        scratch_shapes=[pltpu.SemaphoreType.DMA((7 * n,)), pltpu.SemaphoreType.DMA((7 * n,)),
                        pltpu.SemaphoreType.DMA((n,))],
    )(*arrs)


def _row128(vec, off):
    return jnp.pad(vec.astype(f32), (off, LANES - off - vec.shape[0]))[None, :]


def _cb(name, cw):
    return MAIN_OFF[name] // cw


def _layer_fwd(x, p, tag):
    s = {"x": x}
    (h,) = _rowwise(_f_rms, [(x, 0)], [p["norm_w"]], [bf16], cw=D_MODEL, ncol=1, name=f"rms_{tag}")
    u = _mm(h, p["w_main"], mode="nn", out_dtype=bf16, name=f"in_main_{tag}")
    small = _mm(h, p["w_small"], mode="nn", out_dtype=f32, name=f"in_small_{tag}")
    s.update(h=h, u=u, small=small)
    W = BW

    c2 = _fox_gate_fwd(small, p["fb_row"], name=f"fox_gate_{tag}")
    ck2 = c2[:, :FOX_HEADS].T[:, :, None]
    qT = _to_t(u[:, MAIN_OFF["q"]:MAIN_OFF["q"] + W], CW)
    vT = _to_t(u[:, MAIN_OFF["v"]:MAIN_OFF["v"] + W], CW)
    oT, lse2 = _flash_fwd(u, qT, vT, ck2, name=f"flash_{tag}")
    o = _from_t(oT)
    (ya,) = _rowwise(_f_ya, [(o, 0), (u, _cb("ga", W))], [], [bf16], cw=W, ncol=1, name=f"ya_{tag}")
    s.update(ck2=ck2, qT=qT, o=o, oT=oT, lse2=lse2)

    pre_b, xbc_c = _conv_fwd(u, _cb("xbc", 512), XBC_W, p["ssm_conv_w"], p["ssm_conv_b"], with_silu=True,
                             name=f"ssm_conv_{tag}")
    y_ssd, hst = _ssd_fwd(xbc_c, small, p["dtb_row"], p["alog_row"], p["dskip_row"], name=f"ssd_{tag}")
    (yb,) = _rowwise(_f_yb, [(y_ssd, 0), (u, _cb("z", W))], [p["ssm_norm_w"]], [bf16], cw=W, ncol=1,
                     name=f"yb_{tag}")
    s.update(pre_b=pre_b, xbc_c=xbc_c, y_ssd=y_ssd, hst=hst)

    (pc,) = _rowwise(_f_mul, [(u, _cb("scc", 512)), (u, _cb("scx", 512))], [], [bf16], cw=512, ncol=2,
                     name=f"sc_pre_{tag}")
    (cv,) = _conv_fwd(pc, 0, W, p["sc_conv_w"], p["sc_conv_b"], with_silu=False, name=f"sc_conv_{tag}")
    (yc,) = _rowwise(_f_yc, [(u, _cb("scb", 512)), (cv, 0), (u, _cb("gc", 512))], [], [bf16], cw=512, ncol=2,
                     name=f"yc_{tag}")
    s.update(pc=pc, cv=cv)

    (pd,) = _rowwise(_f_glu, [(u, _cb("glua", 512)), (u, _cb("glug", 512))], [], [bf16], cw=512, ncol=2,
                     name=f"cf_pre_{tag}")
    (cf,) = _conv_fwd(pd, 0, W, p["cf_conv_w"], p["cf_conv_b"], with_silu=False, name=f"cf_conv_{tag}")
    (yd,) = _rowwise(_f_yd, [(cf, 0), (u, _cb("gd", W))], [p["cf_ln_w"], p["cf_ln_b"]], [bf16], cw=W, ncol=1,
                     name=f"yd_{tag}")
    s.update(pd=pd, cf=cf)

    ys = (ya, yb, yc, yd)
    gs = [_mm(h, p["w_gate"][i], mode="nn", out_dtype=bf16, name=f"gate{i}_{tag}") for i in range(4)]
    ps = [_mm(ys[i], p["w_branch"][i], mode="nn", out_dtype=bf16, name=f"branch{i}_{tag}") for i in range(4)]
    (merged,) = _rowwise(_f_merge, [(a, 0) for a in gs + ps], list(p["b_gate"]), [bf16], cw=512,
                         ncol=D_MODEL // 512, name=f"merge_{tag}")
    out = _mm(merged, p["w_out"], mode="nn", out_dtype=f32, add=x, name=f"out_{tag}")
    s.update(ys=ys, gs=gs, ps=ps, merged=merged)
    return out, s


def _layer_bwd(dout, p, s, tag):
    g = {}
    W = BW
    u, h, small = s["u"], s["h"], s["small"]
    dob = dout.astype(bf16)
    dmerged = _mm(dob, p["w_out"], mode="nt", out_dtype=bf16, name=f"d_merged_{tag}")
    g["w_out"] = _mm(s["merged"], dob, mode="tn", out_dtype=bf16, name=f"dw_out_{tag}")
    r = _rowwise_bwd(_f_merge, [(a, 0) for a in s["gs"] + s["ps"]], list(p["b_gate"]), [dmerged], [bf16] * 8,
                     cw=512, ncol=D_MODEL // 512, name=f"d_merge_{tag}")
    dgs, dps, g["b_gate"] = r[0:4], r[4:8], r[8:12]
    g["w_gate"] = [_mm(h, dgs[i], mode="tn", out_dtype=bf16, name=f"dw_gate{i}_{tag}") for i in range(4)]
    g["w_branch"] = [_mm(s["ys"][i], dps[i], mode="tn", out_dtype=bf16, name=f"dw_branch{i}_{tag}")
                     for i in range(4)]
    dh = None
    for i in range(4):
        dh = _mm(dgs[i], p["w_gate"][i], mode="nt", out_dtype=f32, add=dh, tn=1024, name=f"dh_gate{i}_{tag}")
    dya, dyb, dyc, dyd = [_mm(dps[i], p["w_branch"][i], mode="nt", out_dtype=bf16, name=f"d_y{i}_{tag}")
                          for i in range(4)]
    du = {}

    do, du["ga"] = _rowwise_bwd(_f_ya, [(s["o"], 0), (u, _cb("ga", W))], [], [dya], [bf16, bf16], cw=W, ncol=1,
                                name=f"d_ya_{tag}")
    doT = _to_t(do, CW)
    delta = _flash_delta(doT, s["oT"], name=f"flash_delta_{tag}")
    kT = _to_t(u[:, MAIN_OFF["k"]:MAIN_OFF["k"] + W], min(FLASH_TK, u.shape[0]))
    du["k"], du["v"], dck, dqT, dcq = _flash_bwd(u, s["qT"], kT, do, doT, s["ck2"], s["lse2"], delta,
                                                 name=f"flash_bwd_{tag}")
    du["q"] = _from_t(dqT).astype(bf16)
    dc = jnp.pad((dcq.reshape(FOX_HEADS, -1) + dck[:, :, 0]).T, ((0, 0), (0, LANES - FOX_HEADS)))

    dy_ssd, du["z"], g["ssm_norm_w"] = _rowwise_bwd(
        _f_yb, [(s["y_ssd"], 0), (u, _cb("z", W))], [p["ssm_norm_w"]], [dyb], [bf16, bf16], cw=W, ncol=1,
        name=f"d_yb_{tag}")
    dxbc_c, dsmall_dt, g["dtb_row"], g["alog_row"], g["dskip_row"] = _ssd_bwd(
        dy_ssd, s["xbc_c"], small, p["dtb_row"], p["alog_row"], p["dskip_row"], s["hst"], name=f"d_ssd_{tag}")
    (dpre_b,) = _rowwise_bwd(_f_silu, [(s["pre_b"], 0)], [], [dxbc_c], [bf16], cw=512, ncol=3,
                             name=f"d_ssm_silu_{tag}")
    du["xbc"], g["ssm_conv_w"], g["ssm_conv_b"] = _conv_bwd(dpre_b, u, _cb("xbc", 256), XBC_W, p["ssm_conv_w"],
                                                            name=f"d_ssm_conv_{tag}")

    du["scb"], dcv, du["gc"] = _rowwise_bwd(
        _f_yc, [(u, _cb("scb", 512)), (s["cv"], 0), (u, _cb("gc", 512))], [], [dyc], [bf16] * 3, cw=512, ncol=2,
        name=f"d_yc_{tag}")
    dpc, g["sc_conv_w"], g["sc_conv_b"] = _conv_bwd(dcv, s["pc"], 0, W, p["sc_conv_w"], name=f"d_sc_conv_{tag}")
    du["scc"], du["scx"] = _rowwise_bwd(_f_mul, [(u, _cb("scc", 512)), (u, _cb("scx", 512))], [], [dpc],
                                        [bf16, bf16], cw=512, ncol=2, name=f"d_sc_pre_{tag}")

    dcf, du["gd"], g["cf_ln_w"], g["cf_ln_b"] = _rowwise_bwd(
        _f_yd, [(s["cf"], 0), (u, _cb("gd", W))], [p["cf_ln_w"], p["cf_ln_b"]], [dyd], [bf16, bf16], cw=W, ncol=1,
        name=f"d_yd_{tag}")
    dpd, g["cf_conv_w"], g["cf_conv_b"] = _conv_bwd(dcf, s["pd"], 0, W, p["cf_conv_w"], name=f"d_cf_conv_{tag}")
    du["glua"], du["glug"] = _rowwise_bwd(_f_glu, [(u, _cb("glua", 512)), (u, _cb("glug", 512))], [], [dpd],
                                          [bf16, bf16], cw=512, ncol=2, name=f"d_cf_pre_{tag}")

    dsmall, g["fb_row"] = _fox_gate_bwd(dc, small, p["fb_row"], dsmall_dt, name=f"d_fox_gate_{tag}")
    du_main = jnp.concatenate([du[n] for n in MAIN_NAMES], axis=1)
    dsb = dsmall.astype(bf16)
    g["w_main"] = _mm(h, du_main, mode="tn", out_dtype=bf16, name=f"dw_main_{tag}")
    g["w_small"] = _mm(h, dsb, mode="tn", out_dtype=bf16, name=f"dw_small_{tag}")
    dh = _mm(du_main, p["w_main"], mode="nt", out_dtype=f32, add=dh, tn=1024, name=f"dh_main_{tag}")
    dh = _mm(dsb, p["w_small"], mode="nt", out_dtype=f32, add=dh, tn=1024, name=f"dh_small_{tag}")
    dx, g["norm_w"] = _rowwise_bwd(_f_rms_res, [(s["x"], 0)], [p["norm_w"]], [dh, dout], [f32], cw=D_MODEL,
                                   ncol=1, name=f"d_rms_{tag}")
    return dx, g


_BIG = ("w_in", "w_gate", "w_branch", "w_out")
_REPL = ("norm_w", "fg_bias", "ssm_conv_b", "dt_bias", "a_log", "d_skip", "ssm_norm_w", "sc_conv_b", "cf_conv_b",
         "cf_ln_w", "cf_ln_b", "final_norm_w")
_SHSM = ("ssm_conv_w", "sc_conv_w", "cf_conv_w", "b_gate")


def _to_rows(a, dtype):
    flat = a.astype(dtype).reshape(-1)
    n = flat.shape[0]
    per = 8 * LANES
    flat = jnp.pad(flat, (0, (-n) % per))
    return flat.reshape(-1, LANES)


def _pack(arrs, dtype):
    rows = [_to_rows(a, dtype) for a in arrs]
    n = sum(r.shape[0] for r in rows)
    rows.append(jnp.zeros(((-n) % 256, LANES), dtype))
    return jnp.concatenate(rows, axis=0)


def _unpack(blob, shapes):
    lead = blob.shape[:-2]
    out, r0 = [], 0
    for shp in shapes:
        n = 1
        for d in shp:
            n *= d
        nr = (-(-n // (8 * LANES))) * 8
        piece = blob[..., r0:r0 + nr, :].reshape(lead + (nr * LANES,))[..., :n]
        out.append(piece.reshape(lead + tuple(shp)))
        r0 += nr
    return out


def _main_cols(w):
    off, pieces = 0, {}
    for n, sz in zip(IN_NAMES, IN_SIZES):
        pieces[n] = w[..., off:off + sz]
        off += sz
    main = jnp.concatenate([pieces[n] for n in MAIN_NAMES], axis=-1)
    pad = jnp.zeros(w.shape[:-1] + (LANES - 24,), w.dtype)
    small = jnp.concatenate([pieces["f"], pieces["dt"], pad], axis=-1)
    return main, small


def _orig_cols(main, small):
    pieces = {}
    for n in MAIN_NAMES:
        sz = XBC_W if n == "xbc" else 1024
        pieces[n] = main[..., MAIN_OFF[n]:MAIN_OFF[n] + sz]
    pieces["f"] = small[..., 0:8]
    pieces["dt"] = small[..., 8:24]
    return jnp.concatenate([pieces[n] for n in IN_NAMES], axis=-1)


def kernel(*args):
    names = ["x", "norm_w", "w_in", "fg_bias", "ssm_conv_w", "ssm_conv_b", "dt_bias", "a_log", "d_skip",
             "ssm_norm_w", "sc_conv_w", "sc_conv_b", "cf_conv_w", "cf_conv_b", "cf_ln_w", "cf_ln_b", "w_gate",
             "b_gate", "w_branch", "w_out", "final_norm_w"]
    wnames = names[1:]
    a = dict(zip(names, args[:21]))
    target = args[21]
    mom = dict(zip(["m_" + n for n in wnames], args[22:42]))
    mom.update(zip(["v_" + n for n in wnames], args[42:62]))
    NL = a["w_in"].shape[0]
    x0 = a["x"][0]
    tgt = target[0]
    me = 4 * lax.axis_index("x") + 2 * lax.axis_index("y") + lax.axis_index("c")

    def rows2d(t):
        return t.reshape(-1, t.shape[-1])

    big_g = _allgather([rows2d(a[n]).astype(bf16) for n in _BIG], name="gather_big")
    w_in_g, w_gate_g, w_branch_g, w_out_g = [t.reshape((N_DEV,) + a[n].shape) for t, n in zip(big_g, _BIG)]
    (sm_all,) = _allgather([_pack([a[n] for n in _SHSM], f32)], name="gather_small")
    ssm_cw_g, sc_cw_g, cf_cw_g, b_gate_g = _unpack(sm_all, [a[n].shape for n in _SHSM])

    def cat_last(gathered):
        return jnp.moveaxis(gathered, 0, -2).reshape(gathered.shape[1:-1] + (-1,))

    w_in_full = cat_last(w_in_g)
    w_main_all, w_small_all = _main_cols(w_in_full)
    w_gate_full = jnp.moveaxis(w_gate_g, 0, 2).reshape(NL, 4, D_MODEL, D_MODEL)
    w_branch_full = cat_last(w_branch_g)
    w_out_full = jnp.moveaxis(w_out_g, 0, 1).reshape(NL, D_MODEL, D_MODEL)
    ssm_cw, sc_cw, cf_cw, b_gate_full = cat_last(ssm_cw_g), cat_last(sc_cw_g), cat_last(cf_cw_g), cat_last(b_gate_g)

    layers = []
    for l in range(NL):
        layers.append(dict(
            norm_w=a["norm_w"][l][None], w_main=w_main_all[l], w_small=w_small_all[l],
            fb_row=_row128(a["fg_bias"][l], 0), dtb_row=_row128(a["dt_bias"][l], HL),
            alog_row=_row128(a["a_log"][l], HL), dskip_row=_row128(a["d_skip"][l], HL),
            ssm_conv_w=ssm_cw[l], ssm_conv_b=a["ssm_conv_b"][l][None], ssm_norm_w=a["ssm_norm_w"][l][None],
            sc_conv_w=sc_cw[l], sc_conv_b=a["sc_conv_b"][l][None],
            cf_conv_w=cf_cw[l], cf_conv_b=a["cf_conv_b"][l][None],
            cf_ln_w=a["cf_ln_w"][l][None], cf_ln_b=a["cf_ln_b"][l][None],
            w_gate=[w_gate_full[l, i] for i in range(4)], b_gate=[b_gate_full[l, i][None] for i in range(4)],
            w_branch=[w_branch_full[l, i] for i in range(4)], w_out=w_out_full[l]))

    xl, saved = x0, []
    for l in range(NL):
        xl, s = _layer_fwd(xl, layers[l], f"l{l}")
        saved.append(s)
    loss_part, dx, d_fnw = _loss_head(xl, a["final_norm_w"][None], tgt, name="loss_head")
    grads = [None] * NL
    for l in reversed(range(NL)):
        dx, grads[l] = _layer_bwd(dx, layers[l], saved[l], f"l{l}")

    def stack(key):
        return jnp.stack([grads[l][key] for l in range(NL)])

    d_w_in = _orig_cols(stack("w_main"), stack("w_small"))
    sh_in = a["w_in"].shape[-1]
    s_in = jnp.moveaxis(d_w_in.reshape(NL, D_MODEL, N_DEV, sh_in), 2, 0)
    d_w_gate = jnp.stack([jnp.stack(grads[l]["w_gate"]) for l in range(NL)])
    s_gate = jnp.moveaxis(d_w_gate.reshape(NL, 4, N_DEV, D_MODEL // N_DEV, D_MODEL), 2, 0)
    d_w_branch = jnp.stack([jnp.stack(grads[l]["w_branch"]) for l in range(NL)])
    s_branch = jnp.moveaxis(d_w_branch.reshape(NL, 4, BW, N_DEV, D_MODEL // N_DEV), 3, 0)
    s_out = jnp.moveaxis(stack("w_out").reshape(NL, N_DEV, D_MODEL // N_DEV, D_MODEL), 1, 0)
    got = _exchange([t.reshape(N_DEV, -1, t.shape[-1]) for t in (s_in, s_gate, s_branch, s_out)],
                    name="exchange_big")
    big_res = [[], [], [], []]
    for n, gsl in zip(_BIG, got):
        outs = _adamw(gsl, rows2d(a[n]), rows2d(mom["m_" + n]), rows2d(mom["v_" + n]), name=f"adamw_{n}")
        for k in range(4):
            big_res[k].append(outs[k].reshape(a[n].shape))

    def vec(row, off, n):
        return row[0, off:off + n]

    small_grads = {
        "norm_w": jnp.concatenate([grads[l]["norm_w"] for l in range(NL)]),
        "fg_bias": jnp.stack([vec(grads[l]["fb_row"], 0, FOX_HEADS) for l in range(NL)]),
        "ssm_conv_b": jnp.concatenate([grads[l]["ssm_conv_b"] for l in range(NL)]),
        "dt_bias": jnp.stack([vec(grads[l]["dtb_row"], HL, SSM_HEADS) for l in range(NL)]),
        "a_log": jnp.stack([vec(grads[l]["alog_row"], HL, SSM_HEADS) for l in range(NL)]),
        "d_skip": jnp.stack([vec(grads[l]["dskip_row"], HL, SSM_HEADS) for l in range(NL)]),
        "ssm_norm_w": jnp.concatenate([grads[l]["ssm_norm_w"] for l in range(NL)]),
        "sc_conv_b": jnp.concatenate([grads[l]["sc_conv_b"] for l in range(NL)]),
        "cf_conv_b": jnp.concatenate([grads[l]["cf_conv_b"] for l in range(NL)]),
        "cf_ln_w": jnp.concatenate([grads[l]["cf_ln_w"] for l in range(NL)]),
        "cf_ln_b": jnp.concatenate([grads[l]["cf_ln_b"] for l in range(NL)]),
        "final_norm_w": d_fnw[0],
        "ssm_conv_w": stack("ssm_conv_w"), "sc_conv_w": stack("sc_conv_w"), "cf_conv_w": stack("cf_conv_w"),
        "b_gate": jnp.stack([jnp.concatenate(grads[l]["b_gate"]) for l in range(NL)]),
    }
    order = _REPL + _SHSM
    full_shapes = [small_grads[n].shape for n in order] + [(1,)]
    (sg_all,) = _allgather([_pack([small_grads[n] for n in order] + [loss_part.reshape(1)], f32)],
                           name="gather_small_grads")

    def mine(n, full):
        if n in _REPL:
            return full
        sz = a[n].shape[-1]
        return lax.dynamic_slice_in_dim(full, me * sz, sz, axis=full.ndim - 1)

    g_full = _unpack(_sum_slabs(sg_all, name="sum_small"), full_shapes)
    loss = g_full[-1][0]
    g_mine = {n: mine(n, gf).reshape(a[n].shape) for n, gf in zip(order, g_full[:-1])}
    gs1 = _pack([g_mine[n] for n in order], f32)
    ws1 = _pack([a[n] for n in order], f32)
    ms1 = _pack([mom["m_" + n] for n in order], f32)
    vs1 = _pack([mom["v_" + n] for n in order], f32)
    sm_out = _adamw(gs1[None], ws1, ms1, vs1, name="adamw_small")
    sm_res = [_unpack(o, [a[n].shape for n in order]) for o in sm_out]

    res = []
    for k in range(4):
        d = dict(zip(_BIG, big_res[k]))
        d.update(zip(order, sm_res[k]))
        res.append(d)
    outs = [loss, dx[None]]
    for k in range(4):
        outs += [res[k][n] for n in wnames]
    return tuple(outs)
```

```python
import functools

import jax
import jax.numpy as jnp
from jax import lax
from jax.experimental import pallas as pl
from jax.experimental.pallas import tpu as pltpu

f32 = jnp.float32
bf16 = jnp.bfloat16
SDS = jax.ShapeDtypeStruct

N_DEV = 8
LANES = 128
VMEM_LIMIT = 48 * 1024 * 1024
EPS = 1e-6
NEG = -1e30

D_MODEL = 2048
FOX_HEADS = 8
FOX_DH = 128
SSM_HEADS = 16
SSM_P = 64
SSM_N = 128
SSM_G = 2
SSM_W = 1024
XBC_W = 1536
SSM_K, SC_K, CF_K = 4, 3, 31
BW = 1024
SSD_L = 128
HL = 8
CONV_HALO = 32

ADAM_LR, ADAM_B1, ADAM_B2, ADAM_EPS, ADAM_WD, ADAM_STEP = 0.001, 0.9, 0.999, 1e-08, 0.01, 10

IN_NAMES = ("q", "k", "v", "f", "ga", "z", "xbc", "dt", "scb", "scc", "scx", "gc", "glua", "glug", "gd")
IN_SIZES = (1024, 1024, 1024, 8, 1024, 1024, 1536, 16, 1024, 1024, 1024, 1024, 1024, 1024, 1024)
N_IN = sum(IN_SIZES)
MAIN_NAMES = ("q", "k", "v", "ga", "z", "scb", "scc", "scx", "gc", "glua", "glug", "gd", "xbc")
MAIN_OFF = {n: 1024 * i for i, n in enumerate(MAIN_NAMES)}
N_MAIN = 12 * 1024 + XBC_W


def _pcall(body, **kw):
    return pl.pallas_call(body, **kw)


def _cp(*sem):
    return pltpu.CompilerParams(dimension_semantics=sem if sem else None, vmem_limit_bytes=VMEM_LIMIT)


def _pick(n, cands):
    for c in cands:
        if c <= n and n % c == 0:
            return c
    return n


_DIMS = {"nn": ((1,), (0,)), "nt": ((1,), (1,)), "tn": ((0,), (0,))}
_MESH = pl.DeviceIdType.MESH
_HBM = pl.BlockSpec(memory_space=pltpu.HBM)


def _side_specs(side):
    kind, arrs = side
    n = len(arrs)
    shapes = [SDS(((N_DEV,) + a.shape) if kind == "gather" else a.shape, a.dtype) for a in arrs]
    sems = [pltpu.SemaphoreType.DMA((7 * n,)), pltpu.SemaphoreType.DMA((7 * n,)), pltpu.SemaphoreType.DMA((n,))]
    return [_HBM] * n, [_HBM] * n, shapes, sems


def _side_copies(kind, in_refs, out_refs, send_sems, recv_sems, local_sems):
    x, y, c = lax.axis_index("x"), lax.axis_index("y"), lax.axis_index("c")
    me = 4 * x + 2 * y + c
    sends, recvs = [], []
    for a in range(len(in_refs)):
        src_mine = in_refs[a] if kind == "gather" else in_refs[a].at[me]
        sends.append(pltpu.make_async_copy(src_mine, out_refs[a].at[me], local_sems.at[a]))
    for k in range(1, N_DEV):
        px, py, pc = x ^ ((k >> 2) & 1), y ^ ((k >> 1) & 1), c ^ (k & 1)
        peer = 4 * px + 2 * py + pc
        for a in range(len(in_refs)):
            src = in_refs[a] if kind == "gather" else in_refs[a].at[peer]

            def copy(src_ref, dst_slot):
                return pltpu.make_async_remote_copy(
                    src_ref=src_ref, dst_ref=out_refs[a].at[dst_slot],
                    send_sem=send_sems.at[7 * a + k - 1], recv_sem=recv_sems.at[7 * a + k - 1],
                    device_id=(px, py, pc), device_id_type=_MESH)

            sends.append(copy(src, me))
            recvs.append(copy(src, peer))
    return sends, recvs


def _side_run(side, refs, first, last):
    n = len(side[1])
    sends, recvs = _side_copies(side[0], refs[:n], refs[n:2 * n], *refs[2 * n:])

    @pl.when(first)
    def _():
        for cp in sends:
            cp.start()

    @pl.when(last)
    def _():
        for cp in recvs:
            cp.wait_recv()
        for cp in sends[n:]:
            cp.wait_send()
        for cp in sends[:n]:
            cp.wait()


def _mm(a, b, *, mode, out_dtype, name, add=None, tm=512, tn=None, tk=None, side=None):
    if mode == "nn":
        (M, K), (_, N) = a.shape, b.shape
    elif mode == "nt":
        (M, K), (N, _) = a.shape, b.shape
    else:
        (K, M), (_, N) = a.shape, b.shape
    tm = _pick(M, (tm, 256, 128))
    tn = _pick(N, (tn,) if tn else (1536, 1024, 512, 256, 128))
    tk = _pick(K, (tk,) if tk else (2048, 1536, 1024, 512, 256, 128))
    nk = K // tk
    a_spec = (pl.BlockSpec((tk, tm), lambda j, i, k: (k, i)) if mode == "tn"
              else pl.BlockSpec((tm, tk), lambda j, i, k: (i, k)))
    b_spec = (pl.BlockSpec((tn, tk), lambda j, i, k: (j, k)) if mode == "nt"
              else pl.BlockSpec((tk, tn), lambda j, i, k: (k, j)))
    o_spec = pl.BlockSpec((tm, tn), lambda j, i, k: (i, j))
    ins, specs = [a, b], [a_spec, b_spec]
    if add is not None:
        ins.append(add)
        specs.append(o_spec)
    n_in = len(ins)
    dims = (_DIMS[mode], ((), ()))
    out_specs, out_shape = [o_spec], [SDS((M, N), out_dtype)]
    scratch = [pltpu.VMEM((tm, tn), f32)] if nk > 1 else []
    ns = 0
    if side is not None:
        ns = len(side[1])
        s_in, s_out, s_shapes, s_sems = _side_specs(side)
        ins, specs = ins + list(side[1]), specs + s_in
        out_specs, out_shape, scratch = out_specs + s_out, out_shape + s_shapes, scratch + s_sems
    grid = (N // tn, M // tm, nk)

    def body(*refs):
        a_ref, b_ref = refs[0], refs[1]
        add_ref = refs[2] if add is not None else None
        o_ref = refs[n_in + ns]
        rest = refs[n_in + 2 * ns + 1:]
        jj, ii, k = pl.program_id(0), pl.program_id(1), pl.program_id(2)
        if side is not None:
            side_refs = refs[n_in:n_in + ns] + refs[n_in + ns + 1:n_in + 2 * ns + 1] + rest[-3:]
            _side_run(side, side_refs, (jj == 0) & (ii == 0) & (k == 0),
                      (jj == grid[0] - 1) & (ii == grid[1] - 1) & (k == nk - 1))
        p = lax.dot_general(a_ref[...], b_ref[...], dims, preferred_element_type=f32)

        def finish(v):
            if add_ref is not None:
                v = v + add_ref[...].astype(f32)
            o_ref[...] = v.astype(o_ref.dtype)

        if nk == 1:
            finish(p)
        else:
            acc = rest[0]

            @pl.when(k == 0)
            def _():
                acc[...] = p

            @pl.when(k > 0)
            def _():
                acc[...] += p

            @pl.when(k == nk - 1)
            def _():
                finish(acc[...])

    res = _pcall(
        body, name=name, grid=grid, in_specs=specs, out_specs=out_specs, out_shape=out_shape,
        scratch_shapes=scratch,
        compiler_params=_cp("parallel", "parallel", "arbitrary") if side is None else _cp(*(("arbitrary",) * 3)),
    )(*ins)
    return res[0] if side is None else res


def _silu(x):
    return x * jax.nn.sigmoid(x)


def _softplus(x):
    return jnp.maximum(x, 0.0) + jnp.log(1.0 + jnp.exp(-jnp.abs(x)))


def _mean(x):
    return jnp.mean(x, axis=-1, keepdims=True)


def _f_rms(x, w):
    return (x * lax.rsqrt(_mean(x * x) + EPS) * w,)


def _f_rms_res(x, w):
    return _f_rms(x, w) + (x,)


def _f_ya(o, g):
    return (o * _silu(g),)


def _f_yb(y, z, w):
    t = y * _silu(z)
    return (t * lax.rsqrt(_mean(t * t) + EPS) * w,)


def _f_mul(a, b):
    return (a * b,)


def _f_yc(b, cv, g):
    return (b * cv * _silu(g),)


def _f_glu(a, g):
    return (a * jax.nn.sigmoid(g),)


def _f_yd(cf, g, lw, lb):
    mu = _mean(cf)
    var = _mean(jnp.square(cf - mu))
    ln = (cf - mu) * lax.rsqrt(var + EPS) * lw + lb
    return (_silu(ln) * _silu(g),)


def _f_silu(x):
    return (_silu(x),)


def _f_merge(g0, g1, g2, g3, p0, p1, p2, p3, b0, b1, b2, b3):
    sg = jax.nn.sigmoid
    return (sg(g0 + b0) * p0 + sg(g1 + b1) * p1 + sg(g2 + b2) * p2 + sg(g3 + b3) * p3,)


def _row_specs(ins, params, tt, cw):
    specs = [pl.BlockSpec((tt, cw), functools.partial(lambda j, i, base: (i, base + j), base=base))
             for _, base in ins]
    specs += [pl.BlockSpec((p.shape[0], cw), lambda j, i: (0, j)) for p in params]
    return specs


def _rowwise(fn, ins, params, out_dtypes, *, cw, ncol, name, tt=256):
    T = ins[0][0].shape[0]
    tt = _pick(T, (tt, 128))
    ni, npar = len(ins), len(params)

    def body(*refs):
        xs = [r[...].astype(f32) for r in refs[:ni]]
        ps = [r[...] for r in refs[ni:ni + npar]]
        ys = fn(*xs, *ps)
        for o_ref, y in zip(refs[ni + npar:], ys):
            o_ref[...] = y.astype(o_ref.dtype)

    out_spec = pl.BlockSpec((tt, cw), lambda j, i: (i, j))
    outs = _pcall(
        body, name=name, grid=(ncol, T // tt), in_specs=_row_specs(ins, params, tt, cw),
        out_specs=[out_spec] * len(out_dtypes),
        out_shape=[SDS((T, cw * ncol), dt) for dt in out_dtypes],
        compiler_params=_cp("parallel", "parallel"),
    )(*[a for a, _ in ins], *params)
    return outs


def _rowwise_bwd(fn, ins, params, cots, din_dtypes, *, cw, ncol, name, tt=256):
    T = ins[0][0].shape[0]
    tt = _pick(T, (tt, 128))
    ni, npar, nc = len(ins), len(params), len(cots)
    keep = [k for k, dt in enumerate(din_dtypes) if dt is not None]

    def body(*refs):
        xs = [r[...].astype(f32) for r in refs[:ni]]
        ps = [r[...] for r in refs[ni:ni + npar]]
        cs = tuple(r[...].astype(f32) for r in refs[ni + npar:ni + npar + nc])
        outs = refs[ni + npar + nc:]
        _, vjp = jax.vjp(fn, *xs, *ps)
        g = vjp(cs)
        for o_ref, k in zip(outs[:len(keep)], keep):
            o_ref[...] = g[k].astype(o_ref.dtype)
        i = pl.program_id(1)
        for o_ref, gp in zip(outs[len(keep):], g[ni:]):
            @pl.when(i == 0)
            def _(o_ref=o_ref, gp=gp):
                o_ref[...] = gp

            @pl.when(i > 0)
            def _(o_ref=o_ref, gp=gp):
                o_ref[...] += gp

    row_spec = pl.BlockSpec((tt, cw), lambda j, i: (i, j))
    in_specs = _row_specs(ins, params, tt, cw) + [row_spec] * nc
    out_specs = [row_spec] * len(keep) + [pl.BlockSpec((p.shape[0], cw), lambda j, i: (0, j)) for p in params]
    out_shape = [SDS((T, cw * ncol), din_dtypes[k]) for k in keep] + [SDS(p.shape, f32) for p in params]
    return _pcall(
        body, name=name, grid=(ncol, T // tt), in_specs=in_specs, out_specs=out_specs, out_shape=out_shape,
        compiler_params=_cp("arbitrary", "arbitrary"),
    )(*[a for a, _ in ins], *params, *cots)


CONV_RB = 64
SUBLANES = 8


def _shifted_rows(buf, n):
    for sft in range(1, SUBLANES):
        buf[sft, 0:n, :] = buf[0, pl.ds(sft, n), :]


def _tap(buf, o, rb, shifted):
    if shifted:
        return buf[o % SUBLANES, o - o % SUBLANES:o - o % SUBLANES + rb, :]
    return buf[0, pl.ds(o, rb), :]


def _conv_fwd(x, xbase, C, w, b, *, with_silu, name, out_dtype=bf16, tt=512, cw=512):
    xa = x
    T = xa.shape[0]
    K = w.shape[0]
    tt = _pick(T, (tt,))
    hp = CONV_HALO
    r = tt // hp
    ncol = C // cw
    rb = min(CONV_RB, tt)
    n = hp + tt
    shifted = K > SUBLANES

    def body(cur_ref, halo_ref, w_ref, b_ref, *rest):
        outs, ext = rest[:-1], rest[-1]
        i = pl.program_id(1)
        ext[0, 0:hp, :] = jnp.where(i > 0, halo_ref[...].astype(f32), 0.0)
        ext[0, hp:n, :] = cur_ref[...].astype(f32)
        ext[0, n:, :] = jnp.zeros((SUBLANES, cw), f32)
        if shifted:
            _shifted_rows(ext, n)
        for r0 in range(0, tt, rb):
            acc = jnp.zeros((rb, cw), f32) + b_ref[...]
            for k in range(K):
                acc = acc + w_ref[k:k + 1, :] * _tap(ext, hp - (K - 1) + k + r0, rb, shifted)
            outs[0][r0:r0 + rb, :] = acc.astype(outs[0].dtype)
            if with_silu:
                outs[1][r0:r0 + rb, :] = _silu(acc).astype(outs[1].dtype)

    n_out = 2 if with_silu else 1
    o_spec = pl.BlockSpec((tt, cw), lambda j, i: (i, j))
    return _pcall(
        body, name=name, grid=(ncol, T // tt),
        in_specs=[pl.BlockSpec((tt, cw), lambda j, i: (i, xbase + j)),
                  pl.BlockSpec((hp, cw), lambda j, i: (jnp.maximum(i * r - 1, 0), xbase + j)),
                  pl.BlockSpec((K, cw), lambda j, i: (0, j)),
                  pl.BlockSpec((1, cw), lambda j, i: (0, j))],
        out_specs=[o_spec] * n_out, out_shape=[SDS((T, C), out_dtype)] * n_out,
        scratch_shapes=[pltpu.VMEM((SUBLANES if shifted else 1, n + SUBLANES, cw), f32)],
        compiler_params=_cp("parallel", "parallel"),
    )(xa, xa, w, b)


def _conv_bwd(dy, x, xbase, C, w, *, name, dx_dtype=bf16, tt=512, cw=256):
    T = dy.shape[0]
    K = w.shape[0]
    tt = _pick(T, (tt,))
    hp = CONV_HALO
    r = tt // hp
    nt = T // tt
    ncol = C // cw
    rb = min(CONV_RB, tt)
    n = tt + hp
    shifted = K > SUBLANES

    def fold8(v):
        out = v[0:SUBLANES]
        for g in range(1, rb // SUBLANES):
            out = out + v[SUBLANES * g:SUBLANES * (g + 1)]
        return out

    def body(dy_ref, dyn_ref, x_ref, xh_ref, w_ref, dx_ref, dw_ref, db_ref, exd, exx, dwp):
        i = pl.program_id(1)
        exd[0, 0:tt, :] = dy_ref[...].astype(f32)
        exd[0, tt:n, :] = jnp.where(i < nt - 1, dyn_ref[...].astype(f32), 0.0)
        exd[0, n:, :] = jnp.zeros((SUBLANES, cw), f32)
        exx[0, 0:hp, :] = jnp.where(i > 0, xh_ref[...].astype(f32), 0.0)
        exx[0, hp:n, :] = x_ref[...].astype(f32)
        exx[0, n:, :] = jnp.zeros((SUBLANES, cw), f32)
        if shifted:
            _shifted_rows(exd, n)
            _shifted_rows(exx, n)

        @pl.when(i == 0)
        def _():
            dwp[...] = jnp.zeros_like(dwp)

        for r0 in range(0, tt, rb):
            dyc = exd[0, r0:r0 + rb, :]
            acc = jnp.zeros((rb, cw), f32)
            for k in range(K):
                acc = acc + w_ref[k:k + 1, :] * _tap(exd, K - 1 - k + r0, rb, shifted)
                dwp[SUBLANES * k:SUBLANES * (k + 1), :] += fold8(dyc * _tap(exx, hp - (K - 1) + k + r0, rb, shifted))
            dx_ref[r0:r0 + rb, :] = acc.astype(dx_ref.dtype)
            dwp[SUBLANES * K:SUBLANES * (K + 1), :] += fold8(dyc)

        @pl.when(i == nt - 1)
        def _():
            for k in range(K):
                dw_ref[k:k + 1, :] = jnp.sum(dwp[SUBLANES * k:SUBLANES * (k + 1), :], axis=0, keepdims=True)
            db_ref[...] = jnp.sum(dwp[SUBLANES * K:SUBLANES * (K + 1), :], axis=0, keepdims=True)

    nsh = SUBLANES if shifted else 1
    return _pcall(
        body, name=name, grid=(ncol, nt),
        in_specs=[pl.BlockSpec((tt, cw), lambda j, i: (i, j)),
                  pl.BlockSpec((hp, cw), lambda j, i: (jnp.minimum((i + 1) * r, nt * r - 1), j)),
                  pl.BlockSpec((tt, cw), lambda j, i: (i, xbase + j)),
                  pl.BlockSpec((hp, cw), lambda j, i: (jnp.maximum(i * r - 1, 0), xbase + j)),
                  pl.BlockSpec((K, cw), lambda j, i: (0, j))],
        out_specs=[pl.BlockSpec((tt, cw), lambda j, i: (i, j)),
                   pl.BlockSpec((K, cw), lambda j, i: (0, j)),
                   pl.BlockSpec((1, cw), lambda j, i: (0, j))],
        out_shape=[SDS((T, C), dx_dtype), SDS((K, C), f32), SDS((1, C), f32)],
        scratch_shapes=[pltpu.VMEM((nsh, n + SUBLANES, cw), f32), pltpu.VMEM((nsh, n + SUBLANES, cw), f32),
                        pltpu.VMEM((SUBLANES * (K + 1), cw), f32)],
        compiler_params=_cp("arbitrary", "arbitrary"),
    )(dy, dy, x, x, w)


def _tri(n, lower):
    r = lax.broadcasted_iota(jnp.int32, (n, n), 0)
    c = lax.broadcasted_iota(jnp.int32, (n, n), 1)
    return jnp.where((r >= c) if lower else (r <= c), 1.0, 0.0).astype(f32)


def _hdot(a, b):
    return jnp.dot(a, b, precision=lax.Precision.HIGHEST, preferred_element_type=f32)


def _fox_gate_fwd(small, fb_row, *, name, tt=512):
    T = small.shape[0]
    tt = _pick(T, (tt,))

    def body(sm_ref, fb_ref, c_ref, carry):
        @pl.when(pl.program_id(0) == 0)
        def _():
            carry[...] = jnp.zeros_like(carry)

        logf = -_softplus(-(sm_ref[...] + fb_ref[...]))
        cs = _hdot(_tri(tt, True), logf) + carry[...]
        c_ref[...] = cs * LOG2E
        carry[...] = cs[tt - 1:tt, :]

    blk = pl.BlockSpec((tt, LANES), lambda i: (i, 0))
    return _pcall(
        body, name=name, grid=(T // tt,), in_specs=[blk, pl.BlockSpec((1, LANES), lambda i: (0, 0))],
        out_specs=blk, out_shape=SDS((T, LANES), f32), scratch_shapes=[pltpu.VMEM((1, LANES), f32)],
        compiler_params=_cp("arbitrary"),
    )(small, fb_row)


def _fox_gate_bwd(dc, small, fb_row, dsmall_dt, *, name, tt=512):
    T = small.shape[0]
    tt = _pick(T, (tt,))
    nt = T // tt

    def body(dc_ref, sm_ref, fb_ref, dd_ref, ds_ref, dfb_ref, carry):
        @pl.when(pl.program_id(0) == 0)
        def _():
            carry[...] = jnp.zeros_like(carry)
            dfb_ref[...] = jnp.zeros_like(dfb_ref)

        dl = _hdot(_tri(tt, False), dc_ref[...]) + carry[...]
        carry[...] = dl[0:1, :]
        lane = lax.broadcasted_iota(jnp.int32, (tt, LANES), 1)
        df = jnp.where(lane < FOX_HEADS, dl * jax.nn.sigmoid(-(sm_ref[...] + fb_ref[...])), 0.0)
        ds_ref[...] = jnp.where(lane < FOX_HEADS, df, dd_ref[...])
        dfb_ref[...] += jnp.sum(df, axis=0, keepdims=True)

    blk = pl.BlockSpec((tt, LANES), lambda i: (nt - 1 - i, 0))
    row = pl.BlockSpec((1, LANES), lambda i: (0, 0))
    return _pcall(
        body, name=name, grid=(nt,), in_specs=[blk, blk, row, blk], out_specs=[blk, row],
        out_shape=[SDS((T, LANES), f32), SDS((1, LANES), f32)], scratch_shapes=[pltpu.VMEM((1, LANES), f32)],
        compiler_params=_cp("arbitrary"),
    )(dc, small, fb_row, dsmall_dt)


_NT = (((1,), (1,)), ((), ()))
_TN = (((0,), (0,)), ((), ()))


def _causal(n):
    r = lax.broadcasted_iota(jnp.int32, (n, n), 0)
    c = lax.broadcasted_iota(jnp.int32, (n, n), 1)
    return r >= c


LOG2E = 1.4426950408889634
_SCALE = FOX_DH ** -0.5
_SCALE2 = _SCALE * LOG2E
CW = 256
FLASH_TQ = 1024
FLASH_CH = 1024
FLASH_TK = 1024


def _to_t(a, w):
    T = a.shape[0]
    return a.reshape(T // w, w, FOX_HEADS, FOX_DH).transpose(2, 0, 3, 1)


def _from_t(at):
    hh, n, dh, w = at.shape
    return at.transpose(1, 3, 0, 2).reshape(n * w, hh * dh)


def _flash_fwd(u, qT, vT, ck2, *, name, side=None):
    T = u.shape[0]
    H, dh = FOX_HEADS, FOX_DH
    TQ, CH = min(FLASH_TQ, T), min(FLASH_CH, T)
    NC, NQ, RQ, RC, R = T // CW, T // TQ, TQ // CW, CH // CW, TQ // CH
    ko = MAIN_OFF["k"] // dh
    ns = len(side[1]) if side is not None else 0
    s_in, s_out, s_shapes, s_sems = _side_specs(side) if side is not None else ([], [], [], [])

    def body(qT_ref, k_ref, vT_ref, ck_ref, *rest):
        oT_ref, lse_ref = rest[ns], rest[ns + 1]
        i = pl.program_id(1)
        if side is not None:
            hh = pl.program_id(0)
            _side_run(side, rest[:ns] + rest[ns + 2:], (hh == 0) & (i == 0), (hh == H - 1) & (i == NQ - 1))
        qt = jnp.concatenate([qT_ref[r] for r in range(RQ)], axis=1) if RQ > 1 else qT_ref[0]

        def scores(j):
            st = pl.multiple_of(j * CH, CH)
            kc = k_ref[pl.ds(st, CH), :]
            return jnp.dot(kc, qt, preferred_element_type=f32) * _SCALE2 - ck_ref[pl.ds(st, CH), :]

        def update(j, s, carry):
            m, l, acc = carry
            m_new = jnp.maximum(m, jnp.max(s, axis=0, keepdims=True))
            alpha = jnp.exp2(m - m_new)
            p = jnp.exp2(s - m_new)
            l = alpha * l + jnp.sum(p, axis=0, keepdims=True)
            pb = p.astype(bf16)
            pv = jnp.dot(vT_ref[j * RC], pb[0:CW], preferred_element_type=f32)
            for r in range(1, RC):
                pv = pv + jnp.dot(vT_ref[j * RC + r], pb[r * CW:(r + 1) * CW], preferred_element_type=f32)
            return m_new, l, alpha * acc + pv

        def band(s, d):
            r = lax.broadcasted_iota(jnp.int32, (CH, TQ), 0) + d * CH
            c = lax.broadcasted_iota(jnp.int32, (CH, TQ), 1)
            return jnp.where(r <= c, s, NEG)

        carry = (jnp.full((1, TQ), NEG, f32), jnp.zeros((1, TQ), f32), jnp.zeros((dh, TQ), f32))
        carry = lax.fori_loop(0, i * R, lambda j, c: update(j, scores(j), c), carry)
        for d in range(R):
            carry = update(i * R + d, band(scores(i * R + d), d), carry)
        m, l, acc = carry
        o = (acc / l).astype(oT_ref.dtype)
        lse = m + jnp.log2(l)
        for r in range(RQ):
            oT_ref[r] = o[:, r * CW:(r + 1) * CW]
            lse_ref[r] = lse[:, r * CW:(r + 1) * CW]

    return _pcall(
        body, name=name, grid=(H, NQ),
        in_specs=[pl.BlockSpec((None, RQ, dh, CW), lambda h, i: (h, i, 0, 0)),
                  pl.BlockSpec((T, dh), lambda h, i: (0, ko + h)),
                  pl.BlockSpec((None, NC, dh, CW), lambda h, i: (h, 0, 0, 0)),
                  pl.BlockSpec((None, T, 1), lambda h, i: (h, 0, 0))] + s_in,
        out_specs=[pl.BlockSpec((None, RQ, dh, CW), lambda h, i: (h, i, 0, 0)),
                   pl.BlockSpec((None, RQ, 1, CW), lambda h, i: (h, i, 0, 0))] + s_out,
        out_shape=[SDS((H, NC, dh, CW), bf16), SDS((H, NC, 1, CW), f32)] + s_shapes,
        scratch_shapes=s_sems,
        compiler_params=_cp("parallel", "arbitrary") if side is None else _cp("arbitrary", "arbitrary"),
    )(qT, u, vT, ck2, *(side[1] if side is not None else []))


def _flash_delta(doT, oT, *, name):
    H, NC, dh, _ = doT.shape

    def body(a_ref, b_ref, d_ref):
        d_ref[...] = jnp.sum(a_ref[...].astype(f32) * b_ref[...].astype(f32), axis=0, keepdims=True)

    blk = pl.BlockSpec((None, None, dh, CW), lambda h, i: (h, i, 0, 0))
    return _pcall(
        body, name=name, grid=(H, NC), in_specs=[blk, blk],
        out_specs=pl.BlockSpec((None, None, 1, CW), lambda h, i: (h, i, 0, 0)),
        out_shape=SDS((H, NC, 1, CW), f32), compiler_params=_cp("parallel", "parallel"),
    )(doT, oT)


def _flash_bwd(u, qT, kT, do, doT, ck2, lse2, delta, *, name):
    T = u.shape[0]
    H, dh = FOX_HEADS, FOX_DH
    TK = min(FLASH_TK, T)
    NC, NK, R = T // CW, T // TK, TK // CW
    qo, ko, vo = MAIN_OFF["q"] // dh, MAIN_OFF["k"] // dh, MAIN_OFF["v"] // dh

    def body(q_ref, k_ref, v_ref, qT_ref, kT_ref, do_ref, doT_ref, ck_ref, lse_ref, dl_ref,
             dk_ref, dv_ref, dck_ref, dqT_ref, dcq_ref, dk_s, dv_s, dc_s):
        j = pl.program_id(1)

        @pl.when(j == 0)
        def _():
            dqT_ref[...] = jnp.zeros_like(dqT_ref)
            dcq_ref[...] = jnp.zeros_like(dcq_ref)

        dk_s[...] = jnp.zeros_like(dk_s)
        dv_s[...] = jnp.zeros_like(dv_s)
        dc_s[...] = jnp.zeros_like(dc_s)
        k, v, kt = k_ref[...], v_ref[...], kT_ref[...]
        ckc = ck_ref[...]

        def chunk(i, d):
            st = pl.multiple_of(i * CW, CW)
            s = jnp.dot(k, qT_ref[i], preferred_element_type=f32) * _SCALE2 - ckc
            if d is not None:
                r = lax.broadcasted_iota(jnp.int32, (TK, CW), 0)
                c = lax.broadcasted_iota(jnp.int32, (TK, CW), 1) + d * CW
                s = jnp.where(r <= c, s, NEG)
            p = jnp.exp2(s - lse_ref[i])
            dp = jnp.dot(v, doT_ref[i], preferred_element_type=f32)
            ds = p * (dp - dl_ref[i])
            dsb = ds.astype(bf16)
            dv_s[...] += jnp.dot(p.astype(bf16), do_ref[pl.ds(st, CW), :], preferred_element_type=f32)
            dk_s[...] += jnp.dot(dsb, q_ref[pl.ds(st, CW), :], preferred_element_type=f32)
            dqT_ref[i] += jnp.dot(kt, dsb, preferred_element_type=f32)
            dcq_ref[i] += jnp.sum(ds, axis=0, keepdims=True)
            part = ds[:, 0:LANES]
            for t in range(1, CW // LANES):
                part = part + ds[:, t * LANES:(t + 1) * LANES]
            dc_s[...] += part

        for d in range(R):
            chunk(j * R + d, d)

        def rest(i, c):
            chunk(i, None)
            return c

        lax.fori_loop(j * R + R, NC, rest, 0)
        dk_ref[...] = (dk_s[...] * _SCALE).astype(dk_ref.dtype)
        dv_ref[...] = dv_s[...].astype(dv_ref.dtype)
        dck_ref[...] = -jnp.sum(dc_s[...], axis=1, keepdims=True)

        @pl.when(j == NK - 1)
        def _():
            dqT_ref[...] = dqT_ref[...] * _SCALE

    res = lambda h, j: (h, 0, 0, 0)
    return _pcall(
        body, name=name, grid=(H, NK),
        in_specs=[pl.BlockSpec((T, dh), lambda h, j: (0, qo + h)),
                  pl.BlockSpec((TK, dh), lambda h, j: (j, ko + h)),
                  pl.BlockSpec((TK, dh), lambda h, j: (j, vo + h)),
                  pl.BlockSpec((None, NC, dh, CW), res),
                  pl.BlockSpec((None, None, dh, TK), lambda h, j: (h, j, 0, 0)),
                  pl.BlockSpec((T, dh), lambda h, j: (0, h)),
                  pl.BlockSpec((None, NC, dh, CW), res),
                  pl.BlockSpec((None, TK, 1), lambda h, j: (h, j, 0)),
                  pl.BlockSpec((None, NC, 1, CW), res),
                  pl.BlockSpec((None, NC, 1, CW), res)],
        out_specs=[pl.BlockSpec((TK, dh), lambda h, j: (j, h)),
                   pl.BlockSpec((TK, dh), lambda h, j: (j, h)),
                   pl.BlockSpec((None, TK, 1), lambda h, j: (h, j, 0)),
                   pl.BlockSpec((None, NC, dh, CW), res),
                   pl.BlockSpec((None, NC, 1, CW), res)],
        out_shape=[SDS((T, H * dh), bf16), SDS((T, H * dh), bf16), SDS((H, T, 1), f32),
                   SDS((H, NC, dh, CW), f32), SDS((H, NC, 1, CW), f32)],
        scratch_shapes=[pltpu.VMEM((TK, dh), f32), pltpu.VMEM((TK, dh), f32), pltpu.VMEM((TK, LANES), f32)],
        compiler_params=_cp("parallel", "arbitrary"),
    )(u, u, u, qT, kT, do, doT, ck2, lse2, delta)


def _ssd_common(sm_ref, dtb_ref, alog_ref):
    L = SSD_L
    dt = _softplus(sm_ref[...] + dtb_ref[...])
    a = -jnp.exp(alog_ref[...])
    cs = _hdot(_tri(L, True), dt * a)
    return dt, a, cs, cs.T, cs[L - 1:L, :]


def _ssd_fwd(xbc, small, dtb_row, alog_row, dskip_row, *, name):
    T = xbc.shape[0]
    L, P, N = SSD_L, SSM_P, SSM_N
    nc = T // L
    hpg = SSM_HEADS // SSM_G

    def body(x_ref, sm_ref, dtb_ref, alog_ref, dsk_ref, y_ref, hs_ref, h_s):
        @pl.when(pl.program_id(0) == 0)
        def _():
            h_s[...] = jnp.zeros_like(h_s)

        dt, a, cs, csT, tot = _ssd_common(sm_ref, dtb_ref, alog_ref)
        hs_ref[...] = h_s[...]
        causal = _causal(L)
        for g in range(SSM_G):
            bg = x_ref[:, SSM_W + g * N:SSM_W + (g + 1) * N]
            cg = x_ref[:, SSM_W + SSM_G * N + g * N:SSM_W + SSM_G * N + (g + 1) * N]
            cb = lax.dot_general(cg, bg, _NT, preferred_element_type=f32)
            hg = h_s[g * hpg * P:(g + 1) * hpg * P, :]
            yoff = lax.dot_general(cg, hg.astype(bf16), _NT, preferred_element_type=f32)
            for hh in range(hpg):
                h = g * hpg + hh
                ln = HL + h
                acol, arow = cs[:, ln:ln + 1], csT[ln:ln + 1, :]
                w = jnp.where(causal, cb * jnp.exp(jnp.minimum(acol - arow, 0.0)), 0.0)
                xs = x_ref[:, h * P:(h + 1) * P].astype(f32)
                xh = xs * dt[:, ln:ln + 1]
                y = jnp.dot(w.astype(bf16), xh.astype(bf16), preferred_element_type=f32)
                y = y + jnp.exp(acol) * yoff[:, hh * P:(hh + 1) * P] + xs * dsk_ref[:, ln:ln + 1]
                y_ref[:, h * P:(h + 1) * P] = y.astype(y_ref.dtype)
                decay = jnp.exp(tot[:, ln:ln + 1] - acol)
                st = lax.dot_general((xh * decay).astype(bf16), bg, _TN, preferred_element_type=f32)
                h_s[h * P:(h + 1) * P, :] = jnp.exp(tot[:, ln:ln + 1]) * h_s[h * P:(h + 1) * P, :] + st

    row = pl.BlockSpec((1, LANES), lambda c: (0, 0))
    return _pcall(
        body, name=name, grid=(nc,),
        in_specs=[pl.BlockSpec((L, XBC_W), lambda c: (c, 0)), pl.BlockSpec((L, LANES), lambda c: (c, 0)),
                  row, row, row],
        out_specs=[pl.BlockSpec((L, SSM_W), lambda c: (c, 0)),
                   pl.BlockSpec((None, SSM_HEADS * P, N), lambda c: (c, 0, 0))],
        out_shape=[SDS((T, SSM_W), bf16), SDS((nc, SSM_HEADS * P, N), f32)],
        scratch_shapes=[pltpu.VMEM((SSM_HEADS * P, N), f32)],
        compiler_params=_cp("arbitrary"),
    )(xbc, small, dtb_row, alog_row, dskip_row)


def _ssd_bwd(dy, xbc, small, dtb_row, alog_row, dskip_row, hstates, *, name):
    T = xbc.shape[0]
    L, P, N = SSD_L, SSM_P, SSM_N
    nc = T // L
    hpg = SSM_HEADS // SSM_G
    GW = hpg * P

    def body(dy_ref, x_ref, sm_ref, dtb_ref, alog_ref, dsk_ref, hs_ref,
             dx_ref, dsm_ref, ddtb_ref, dalog_ref, ddsk_ref, dh_s, ea_s, dx_s):
        @pl.when(pl.program_id(0) == 0)
        def _():
            dh_s[...] = jnp.zeros_like(dh_s)
            ddtb_ref[...] = jnp.zeros_like(ddtb_ref)
            dalog_ref[...] = jnp.zeros_like(dalog_ref)
            ddsk_ref[...] = jnp.zeros_like(ddsk_ref)

        dt, a, cs, csT, tot = _ssd_common(sm_ref, dtb_ref, alog_ref)
        causal = _causal(L)
        lane = lax.broadcasted_iota(jnp.int32, (L, LANES), 1)
        sub = lax.broadcasted_iota(jnp.int32, (LANES, L), 0)
        lane1 = lax.broadcasted_iota(jnp.int32, (1, LANES), 1)
        da_col = jnp.zeros((L, LANES), f32)
        da_row = jnp.zeros((LANES, L), f32)
        da_tot = jnp.zeros((1, LANES), f32)
        ddt = jnp.zeros((L, LANES), f32)
        ddsk = jnp.zeros((1, LANES), f32)
        for g in range(SSM_G):
            bo, co = SSM_W + g * N, SSM_W + SSM_G * N + g * N
            bg, cg = x_ref[:, bo:bo + N], x_ref[:, co:co + N]
            cb = lax.dot_general(cg, bg, _NT, preferred_element_type=f32)
            hg = hs_ref[g * GW:(g + 1) * GW, :]
            hgb = hg.astype(bf16)
            dhn = dh_s[g * GW:(g + 1) * GW, :]
            dhnb = dhn.astype(bf16)
            yoff = lax.dot_general(cg, hgb, _NT, preferred_element_type=f32)
            bdh = lax.dot_general(bg, dhnb, _NT, preferred_element_type=f32)
            dcb = jnp.zeros((L, L), f32)
            for hh in range(hpg):
                h = g * hpg + hh
                ln = HL + h
                sl = slice(hh * P, (hh + 1) * P)
                acol, arow = cs[:, ln:ln + 1], csT[ln:ln + 1, :]
                e = jnp.where(causal, jnp.exp(jnp.minimum(acol - arow, 0.0)), 0.0)
                w = cb * e
                xs = x_ref[:, h * P:(h + 1) * P].astype(f32)
                dth = dt[:, ln:ln + 1]
                xh = xs * dth
                dyh = dy_ref[:, h * P:(h + 1) * P].astype(f32)
                dyb = dyh.astype(bf16)
                ea = jnp.exp(acol)
                decay = jnp.exp(tot[:, ln:ln + 1] - acol)
                etot = jnp.exp(tot[:, ln:ln + 1])
                dw = lax.dot_general(dyb, xh.astype(bf16), _NT, preferred_element_type=f32)
                m = dw * w
                dcb = dcb + dw * e
                dxs_state = decay * bdh[:, sl]
                dxh = lax.dot_general(w.astype(bf16), dyb, _TN, preferred_element_type=f32) + dxs_state
                r = jnp.sum(xh * dxs_state, axis=1, keepdims=True)
                col = (jnp.sum(m, axis=1, keepdims=True) - r
                       + jnp.sum(dyh * (ea * yoff[:, sl]), axis=1, keepdims=True))
                da_col = da_col + jnp.where(lane == ln, col, 0.0)
                da_row = da_row + jnp.where(sub == ln, jnp.sum(m, axis=0, keepdims=True), 0.0)
                hprod = jnp.sum(jnp.sum(dhn[sl, :] * hg[sl, :], axis=1, keepdims=True), axis=0, keepdims=True)
                t_h = jnp.sum(r, axis=0, keepdims=True) + etot * hprod
                da_tot = da_tot + jnp.where(lane1 == ln, t_h, 0.0)
                ea_s[:, sl] = (ea * dyh).astype(bf16)
                dx_s[:, sl] = (decay * xh).astype(bf16)
                dx_ref[:, h * P:(h + 1) * P] = (dxh * dth + dyh * dsk_ref[:, ln:ln + 1]).astype(dx_ref.dtype)
                ddt = ddt + jnp.where(lane == ln, jnp.sum(dxh * xs, axis=1, keepdims=True), 0.0)
                ddsk = ddsk + jnp.where(
                    lane1 == ln, jnp.sum(jnp.sum(dyh * xs, axis=1, keepdims=True), axis=0, keepdims=True), 0.0)
                dh_s[g * GW + hh * P:g * GW + (hh + 1) * P, :] = etot * dhn[sl, :]
            dcbb = dcb.astype(bf16)
            eab, dxb = ea_s[...], dx_s[...]
            dcg = (jnp.dot(dcbb, bg, preferred_element_type=f32)
                   + jnp.dot(eab, hgb, preferred_element_type=f32))
            dbg = (lax.dot_general(dcbb, cg, _TN, preferred_element_type=f32)
                   + jnp.dot(dxb, dhnb, preferred_element_type=f32))
            dx_ref[:, bo:bo + N] = dbg.astype(dx_ref.dtype)
            dx_ref[:, co:co + N] = dcg.astype(dx_ref.dtype)
            dh_s[g * GW:(g + 1) * GW, :] += lax.dot_general(eab, cg, _TN, preferred_element_type=f32)

        rowi = lax.broadcasted_iota(jnp.int32, (L, LANES), 0)
        da_cs = da_col - da_row.T + jnp.where(rowi == L - 1, da_tot, 0.0)
        dda = _hdot(_tri(L, False), da_cs)
        ddt = ddt + dda * a
        heads = (lane >= HL) & (lane < HL + SSM_HEADS)
        draw = jnp.where(heads, ddt * jax.nn.sigmoid(sm_ref[...] + dtb_ref[...]), 0.0)
        dsm_ref[...] = draw
        ddtb_ref[...] += jnp.sum(draw, axis=0, keepdims=True)
        dalog_ref[...] += jnp.where(lane1 >= HL, jnp.sum(dda * dt, axis=0, keepdims=True) * a, 0.0)
        ddsk_ref[...] += ddsk

    rev = lambda c: (nc - 1 - c, 0)
    row = pl.BlockSpec((1, LANES), lambda c: (0, 0))
    return _pcall(
        body, name=name, grid=(nc,),
        in_specs=[pl.BlockSpec((L, SSM_W), rev), pl.BlockSpec((L, XBC_W), rev), pl.BlockSpec((L, LANES), rev),
                  row, row, row, pl.BlockSpec((None, SSM_HEADS * P, N), lambda c: (nc - 1 - c, 0, 0))],
        out_specs=[pl.BlockSpec((L, XBC_W), rev), pl.BlockSpec((L, LANES), rev), row, row, row],
        out_shape=[SDS((T, XBC_W), bf16), SDS((T, LANES), f32)] + [SDS((1, LANES), f32)] * 3,
        scratch_shapes=[pltpu.VMEM((SSM_HEADS * P, N), f32), pltpu.VMEM((L, GW), bf16), pltpu.VMEM((L, GW), bf16)],
        compiler_params=_cp("arbitrary"),
    )(dy, xbc, small, dtb_row, alog_row, dskip_row, hstates)


def _loss_head(x, w, target, *, name, tt=256):
    T, D = x.shape
    tt = _pick(T, (tt, 128))

    def fn(x, w, tg):
        (y,) = _f_rms(x, w)
        return 0.5 * jnp.sum(jnp.sum(jnp.square(y - tg), axis=1, keepdims=True), axis=0, keepdims=True) / D

    def body(x_ref, w_ref, t_ref, loss_ref, dx_ref, dw_ref):
        i = pl.program_id(0)
        ls, vjp = jax.vjp(fn, x_ref[...], w_ref[...], t_ref[...])
        dx, dw, _ = vjp(jnp.ones((1, 1), f32))
        dx_ref[...] = dx

        @pl.when(i == 0)
        def _():
            loss_ref[...] = ls
            dw_ref[...] = dw

        @pl.when(i > 0)
        def _():
            loss_ref[...] += ls
            dw_ref[...] += dw

    blk = pl.BlockSpec((tt, D), lambda i: (i, 0))
    row = pl.BlockSpec((1, D), lambda i: (0, 0))
    return _pcall(
        body, name=name, grid=(T // tt,), in_specs=[blk, row, blk],
        out_specs=[pl.BlockSpec((1, 1), lambda i: (0, 0)), blk, row],
        out_shape=[SDS((1, 1), f32), SDS((T, D), f32), SDS((1, D), f32)],
        compiler_params=_cp("arbitrary"),
    )(x, w, target)


ADAM_BLOCK_ELEMS = 256 * 1024


def _row_tile(rows, cols, nbuf=1):
    for t in (2048, 1024, 512, 256, 128, 64, 32, 16, 8):
        if t * cols * max(1, nbuf // 2) <= ADAM_BLOCK_ELEMS and rows % t == 0:
            return t
    return rows


def _adamw(gss, w, m, v, *, name):
    ns, Rp, C = gss[0].shape
    npart = len(gss)
    tr = _row_tile(Rp, C, npart)
    per = Rp // tr
    c1 = 1.0 / (1.0 - ADAM_B1 ** ADAM_STEP)
    c2 = 1.0 / (1.0 - ADAM_B2 ** ADAM_STEP)

    def body(*refs):
        g_refs = refs[:npart]
        w_ref, m_ref, v_ref, go_ref, d_ref, mo_ref, vo_ref = refs[npart:]
        part = pl.program_id(0) // per

        def update(g_ref):
            g = g_ref[0].astype(f32)
            for s in range(1, ns):
                g = g + g_ref[s].astype(f32)
            mn = ADAM_B1 * m_ref[...] + (1.0 - ADAM_B1) * g
            vn = ADAM_B2 * v_ref[...] + (1.0 - ADAM_B2) * (g * g)
            go_ref[...] = g
            mo_ref[...] = mn
            vo_ref[...] = vn
            d_ref[...] = -ADAM_LR * ((mn * c1) / (jnp.sqrt(vn * c2) + ADAM_EPS) + ADAM_WD * w_ref[...])

        for p in range(npart):
            @pl.when(part == p)
            def _(p=p):
                update(g_refs[p])

    blk = pl.BlockSpec((tr, C), lambda i: (i, 0))
    g_specs = [pl.BlockSpec((ns, tr, C), functools.partial(
        lambda i, p: (0, jnp.clip(i - p * per, 0, per - 1), 0), p=p)) for p in range(npart)]
    return _pcall(
        body, name=name, grid=(npart * per,), in_specs=g_specs + [blk, blk, blk],
        out_specs=[blk] * 4, out_shape=[SDS((npart * Rp, C), f32)] * 4,
        compiler_params=_cp("parallel"),
    )(*gss, w, m, v)


def _sum_slabs(gs, *, name):
    ns, R, C = gs.shape
    tr = _row_tile(R, C)

    def body(g_ref, o_ref):
        g = g_ref[0]
        for s in range(1, ns):
            g = g + g_ref[s]
        o_ref[...] = g

    return _pcall(
        body, name=name, grid=(R // tr,), in_specs=[pl.BlockSpec((ns, tr, C), lambda i: (0, i, 0))],
        out_specs=pl.BlockSpec((tr, C), lambda i: (i, 0)), out_shape=SDS((R, C), f32),
        compiler_params=_cp("parallel"),
    )(gs)


def _allgather(arrs, *, name):
    n = len(arrs)

    def body(*refs):
        x_refs, out_refs = refs[:n], refs[n:2 * n]
        send_sems, recv_sems, local_sems = refs[2 * n:]
        x, y, c = lax.axis_index("x"), lax.axis_index("y"), lax.axis_index("c")
        me, sibling = (x, y, c), (x, y, 1 - c)
        chips = [(1 - x, y), (x, 1 - y), (1 - x, 1 - y)]

        def copy(a, k, block, to, src=None):
            px, py, pc = block
            slot = out_refs[a].at[4 * px + 2 * py + pc]
            return pltpu.make_async_remote_copy(
                src_ref=slot if src is None else src, dst_ref=slot,
                send_sem=send_sems.at[7 * a + k], recv_sem=recv_sems.at[7 * a + k],
                device_id=to, device_id_type=_MESH)

        mine, first, passed = [], [], []
        for a in range(n):
            cp = pltpu.make_async_copy(x_refs[a], out_refs[a].at[4 * x + 2 * y + c], local_sems.at[a])
            cp.start()
            mine.append(cp)
            first.append(copy(a, 0, me, sibling, src=x_refs[a]))
            first += [copy(a, 1 + j, me, (*chip, c), src=x_refs[a]) for j, chip in enumerate(chips)]
        for cp in first:
            cp.start()
        for j, chip in enumerate(chips):
            for a in range(n):
                copy(a, 1 + j, (*chip, c), me).wait_recv()
                cp = copy(a, 4 + j, (*chip, c), sibling)
                cp.start()
                passed.append(cp)
        for a in range(n):
            copy(a, 0, sibling, me).wait_recv()
            for j, chip in enumerate(chips):
                copy(a, 4 + j, (*chip, 1 - c), me).wait_recv()
        for cp in first + passed:
            cp.wait_send()
        for cp in mine:
            cp.wait()

    return _pcall(
        body, name=name, in_specs=[_HBM] * n, out_specs=[_HBM] * n,
        out_shape=[SDS((N_DEV,) + a.shape, a.dtype) for a in arrs],
        scratch_shapes=[pltpu.SemaphoreType.DMA((7 * n,)), pltpu.SemaphoreType.DMA((7 * n,)),
                        pltpu.SemaphoreType.DMA((n,))],
    )(*arrs)


def _exchange(arrs, *, name):
    n = len(arrs)

    def body(*refs):
        in_refs, out_refs = refs[:n], refs[n:2 * n]
        send_sems, recv_sems, local_sems = refs[2 * n:]
        x, y, c = lax.axis_index("x"), lax.axis_index("y"), lax.axis_index("c")
        me = 4 * x + 2 * y + c
        mine, sends, recvs = [], [], []
        for a in range(n):
            cp = pltpu.make_async_copy(in_refs[a].at[me], out_refs[a].at[me], local_sems.at[a])
            cp.start()
            mine.append(cp)
        for k in range(1, N_DEV):
            px, py, pc = x ^ ((k >> 2) & 1), y ^ ((k >> 1) & 1), c ^ (k & 1)
            peer = 4 * px + 2 * py + pc
            for a in range(n):
                def copy(src_slot, dst_slot):
                    return pltpu.make_async_remote_copy(
                        src_ref=in_refs[a].at[src_slot], dst_ref=out_refs[a].at[dst_slot],
                        send_sem=send_sems.at[7 * a + k - 1], recv_sem=recv_sems.at[7 * a + k - 1],
                        device_id=(px, py, pc), device_id_type=_MESH)

                sends.append(copy(peer, me))
                recvs.append(copy(me, peer))
                sends[-1].start()
        for cp in recvs:
            cp.wait_recv()
        for cp in sends:
            cp.wait_send()
        for cp in mine:
            cp.wait()

    return _pcall(
        body, name=name, in_specs=[_HBM] * n, out_specs=[_HBM] * n,
        out_shape=[SDS(a.shape, a.dtype) for a in arrs],
        scratch_shapes=[pltpu.SemaphoreType.DMA((7 * n,)), pltpu.SemaphoreType.DMA((7 * n,)),
                        pltpu.SemaphoreType.DMA((n,))],
    )(*arrs)


def _row128(vec, off):
    return jnp.pad(vec.astype(f32), (off, LANES - off - vec.shape[0]))[None, :]


def _cb(name, cw):
    return MAIN_OFF[name] // cw


def _layer_fwd(x, p, tag, sides=(None, None)):
    s = {"x": x}
    (h,) = _rowwise(_f_rms, [(x, 0)], [p["norm_w"]], [bf16], cw=D_MODEL, ncol=1, name=f"rms_{tag}")
    u = _mm(h, p["w_main"], mode="nn", out_dtype=bf16, name=f"in_main_{tag}", side=sides[0])
    got0 = []
    if sides[0] is not None:
        u, got0 = u[0], list(u[1:])
    small = _mm(h, p["w_small"], mode="nn", out_dtype=f32, name=f"in_small_{tag}")
    s.update(h=h, u=u, small=small)
    W = BW

    c2 = _fox_gate_fwd(small, p["fb_row"], name=f"fox_gate_{tag}")
    ck2 = c2[:, :FOX_HEADS].T[:, :, None]
    qT = _to_t(u[:, MAIN_OFF["q"]:MAIN_OFF["q"] + W], CW)
    vT = _to_t(u[:, MAIN_OFF["v"]:MAIN_OFF["v"] + W], CW)
    oT, lse2, *got1 = _flash_fwd(u, qT, vT, ck2, name=f"flash_{tag}", side=sides[1])
    o = _from_t(oT)
    (ya,) = _rowwise(_f_ya, [(o, 0), (u, _cb("ga", W))], [], [bf16], cw=W, ncol=1, name=f"ya_{tag}")
    s.update(ck2=ck2, qT=qT, o=o, oT=oT, lse2=lse2)

    pre_b, xbc_c = _conv_fwd(u, _cb("xbc", 512), XBC_W, p["ssm_conv_w"], p["ssm_conv_b"], with_silu=True,
                             name=f"ssm_conv_{tag}")
    y_ssd, hst = _ssd_fwd(xbc_c, small, p["dtb_row"], p["alog_row"], p["dskip_row"], name=f"ssd_{tag}")
    (yb,) = _rowwise(_f_yb, [(y_ssd, 0), (u, _cb("z", W))], [p["ssm_norm_w"]], [bf16], cw=W, ncol=1,
                     name=f"yb_{tag}")
    s.update(pre_b=pre_b, xbc_c=xbc_c, y_ssd=y_ssd, hst=hst)

    (pc,) = _rowwise(_f_mul, [(u, _cb("scc", 512)), (u, _cb("scx", 512))], [], [bf16], cw=512, ncol=2,
                     name=f"sc_pre_{tag}")
    (cv,) = _conv_fwd(pc, 0, W, p["sc_conv_w"], p["sc_conv_b"], with_silu=False, name=f"sc_conv_{tag}")
    (yc,) = _rowwise(_f_yc, [(u, _cb("scb", 512)), (cv, 0), (u, _cb("gc", 512))], [], [bf16], cw=512, ncol=2,
                     name=f"yc_{tag}")
    s.update(pc=pc, cv=cv)

    (pd,) = _rowwise(_f_glu, [(u, _cb("glua", 512)), (u, _cb("glug", 512))], [], [bf16], cw=512, ncol=2,
                     name=f"cf_pre_{tag}")
    (cf,) = _conv_fwd(pd, 0, W, p["cf_conv_w"], p["cf_conv_b"], with_silu=False, name=f"cf_conv_{tag}")
    (yd,) = _rowwise(_f_yd, [(cf, 0), (u, _cb("gd", W))], [p["cf_ln_w"], p["cf_ln_b"]], [bf16], cw=W, ncol=1,
                     name=f"yd_{tag}")
    s.update(pd=pd, cf=cf)

    ys = (ya, yb, yc, yd)
    gs = [_mm(h, p["w_gate"][i], mode="nn", out_dtype=bf16, name=f"gate{i}_{tag}") for i in range(4)]
    ps = [_mm(ys[i], p["w_branch"][i], mode="nn", out_dtype=bf16, name=f"branch{i}_{tag}") for i in range(4)]
    (merged,) = _rowwise(_f_merge, [(a, 0) for a in gs + ps], list(p["b_gate"]), [bf16], cw=512,
                         ncol=D_MODEL // 512, name=f"merge_{tag}")
    out = _mm(merged, p["w_out"], mode="nn", out_dtype=f32, add=x, name=f"out_{tag}")
    s.update(ys=ys, gs=gs, ps=ps, merged=merged)
    return out, s, got0 + got1


def _layer_bwd(dout, p, s, tag, sides=(None, None)):
    g = {}
    W = BW
    u, h, small = s["u"], s["h"], s["small"]
    dob = dout.astype(bf16)
    dmerged = _mm(dob, p["w_out"], mode="nt", out_dtype=bf16, name=f"d_merged_{tag}")
    g["w_out"] = _mm(s["merged"], dob, mode="tn", out_dtype=bf16, name=f"dw_out_{tag}")
    r = _rowwise_bwd(_f_merge, [(a, 0) for a in s["gs"] + s["ps"]], list(p["b_gate"]), [dmerged], [bf16] * 8,
                     cw=512, ncol=D_MODEL // 512, name=f"d_merge_{tag}")
    dgs, dps, g["b_gate"] = r[0:4], r[4:8], r[8:12]
    g["w_gate"] = [_mm(h, dgs[i], mode="tn", out_dtype=bf16, name=f"dw_gate{i}_{tag}") for i in range(4)]
    g["w_branch"] = [_mm(s["ys"][i], dps[i], mode="tn", out_dtype=bf16, name=f"dw_branch{i}_{tag}")
                     for i in range(4)]
    dh = None
    for i in range(4):
        dh = _mm(dgs[i], p["w_gate"][i], mode="nt", out_dtype=f32, add=dh, tn=1024, name=f"dh_gate{i}_{tag}")
    dya, dyb, dyc, dyd = [_mm(dps[i], p["w_branch"][i], mode="nt", out_dtype=bf16, name=f"d_y{i}_{tag}")
                          for i in range(4)]
    du = {}

    do, du["ga"] = _rowwise_bwd(_f_ya, [(s["o"], 0), (u, _cb("ga", W))], [], [dya], [bf16, bf16], cw=W, ncol=1,
                                name=f"d_ya_{tag}")
    doT = _to_t(do, CW)
    delta = _flash_delta(doT, s["oT"], name=f"flash_delta_{tag}")
    kT = _to_t(u[:, MAIN_OFF["k"]:MAIN_OFF["k"] + W], min(FLASH_TK, u.shape[0]))
    du["k"], du["v"], dck, dqT, dcq = _flash_bwd(u, s["qT"], kT, do, doT, s["ck2"], s["lse2"], delta,
                                                 name=f"flash_bwd_{tag}")
    du["q"] = _from_t(dqT).astype(bf16)
    dc = jnp.pad((dcq.reshape(FOX_HEADS, -1) + dck[:, :, 0]).T, ((0, 0), (0, LANES - FOX_HEADS)))

    dy_ssd, du["z"], g["ssm_norm_w"] = _rowwise_bwd(
        _f_yb, [(s["y_ssd"], 0), (u, _cb("z", W))], [p["ssm_norm_w"]], [dyb], [bf16, bf16], cw=W, ncol=1,
        name=f"d_yb_{tag}")
    dxbc_c, dsmall_dt, g["dtb_row"], g["alog_row"], g["dskip_row"] = _ssd_bwd(
        dy_ssd, s["xbc_c"], small, p["dtb_row"], p["alog_row"], p["dskip_row"], s["hst"], name=f"d_ssd_{tag}")
    (dpre_b,) = _rowwise_bwd(_f_silu, [(s["pre_b"], 0)], [], [dxbc_c], [bf16], cw=512, ncol=3,
                             name=f"d_ssm_silu_{tag}")
    du["xbc"], g["ssm_conv_w"], g["ssm_conv_b"] = _conv_bwd(dpre_b, u, _cb("xbc", 256), XBC_W, p["ssm_conv_w"],
                                                            name=f"d_ssm_conv_{tag}")

    du["scb"], dcv, du["gc"] = _rowwise_bwd(
        _f_yc, [(u, _cb("scb", 512)), (s["cv"], 0), (u, _cb("gc", 512))], [], [dyc], [bf16] * 3, cw=512, ncol=2,
        name=f"d_yc_{tag}")
    dpc, g["sc_conv_w"], g["sc_conv_b"] = _conv_bwd(dcv, s["pc"], 0, W, p["sc_conv_w"], name=f"d_sc_conv_{tag}")
    du["scc"], du["scx"] = _rowwise_bwd(_f_mul, [(u, _cb("scc", 512)), (u, _cb("scx", 512))], [], [dpc],
                                        [bf16, bf16], cw=512, ncol=2, name=f"d_sc_pre_{tag}")

    dcf, du["gd"], g["cf_ln_w"], g["cf_ln_b"] = _rowwise_bwd(
        _f_yd, [(s["cf"], 0), (u, _cb("gd", W))], [p["cf_ln_w"], p["cf_ln_b"]], [dyd], [bf16, bf16], cw=W, ncol=1,
        name=f"d_yd_{tag}")
    dpd, g["cf_conv_w"], g["cf_conv_b"] = _conv_bwd(dcf, s["pd"], 0, W, p["cf_conv_w"], name=f"d_cf_conv_{tag}")
    du["glua"], du["glug"] = _rowwise_bwd(_f_glu, [(u, _cb("glua", 512)), (u, _cb("glug", 512))], [], [dpd],
                                          [bf16, bf16], cw=512, ncol=2, name=f"d_cf_pre_{tag}")

    dsmall, g["fb_row"] = _fox_gate_bwd(dc, small, p["fb_row"], dsmall_dt, name=f"d_fox_gate_{tag}")
    du_main = jnp.concatenate([du[n] for n in MAIN_NAMES], axis=1)
    dsb = dsmall.astype(bf16)
    got0, got1 = [], []
    g["w_main"] = _mm(h, du_main, mode="tn", out_dtype=bf16, name=f"dw_main_{tag}", side=sides[0])
    if sides[0] is not None:
        g["w_main"], got0 = g["w_main"][0], list(g["w_main"][1:])
    g["w_small"] = _mm(h, dsb, mode="tn", out_dtype=bf16, name=f"dw_small_{tag}")
    dh = _mm(du_main, p["w_main"], mode="nt", out_dtype=f32, add=dh, tn=1024, name=f"dh_main_{tag}", side=sides[1])
    if sides[1] is not None:
        dh, got1 = dh[0], list(dh[1:])
    dh = _mm(dsb, p["w_small"], mode="nt", out_dtype=f32, add=dh, tn=1024, name=f"dh_small_{tag}")
    dx, g["norm_w"] = _rowwise_bwd(_f_rms_res, [(s["x"], 0)], [p["norm_w"]], [dh, dout], [f32], cw=D_MODEL,
                                   ncol=1, name=f"d_rms_{tag}")
    return dx, g, got0 + got1


_BIG = ("w_in", "w_gate", "w_branch", "w_out")
_REPL = ("norm_w", "fg_bias", "ssm_conv_b", "dt_bias", "a_log", "d_skip", "ssm_norm_w", "sc_conv_b", "cf_conv_b",
         "cf_ln_w", "cf_ln_b", "final_norm_w")
_SHSM = ("ssm_conv_w", "sc_conv_w", "cf_conv_w", "b_gate")


def _to_rows(a, dtype):
    flat = a.astype(dtype).reshape(-1)
    n = flat.shape[0]
    per = 8 * LANES
    flat = jnp.pad(flat, (0, (-n) % per))
    return flat.reshape(-1, LANES)


def _pack(arrs, dtype):
    rows = [_to_rows(a, dtype) for a in arrs]
    n = sum(r.shape[0] for r in rows)
    rows.append(jnp.zeros(((-n) % 256, LANES), dtype))
    return jnp.concatenate(rows, axis=0)


def _unpack(blob, shapes):
    lead = blob.shape[:-2]
    out, r0 = [], 0
    for shp in shapes:
        n = 1
        for d in shp:
            n *= d
        nr = (-(-n // (8 * LANES))) * 8
        piece = blob[..., r0:r0 + nr, :].reshape(lead + (nr * LANES,))[..., :n]
        out.append(piece.reshape(lead + tuple(shp)))
        r0 += nr
    return out


def _main_cols(w):
    off, pieces = 0, {}
    for n, sz in zip(IN_NAMES, IN_SIZES):
        pieces[n] = w[..., off:off + sz]
        off += sz
    main = jnp.concatenate([pieces[n] for n in MAIN_NAMES], axis=-1)
    pad = jnp.zeros(w.shape[:-1] + (LANES - 24,), w.dtype)
    small = jnp.concatenate([pieces["f"], pieces["dt"], pad], axis=-1)
    return main, small


def _orig_cols(main, small):
    pieces = {}
    for n in MAIN_NAMES:
        sz = XBC_W if n == "xbc" else 1024
        pieces[n] = main[..., MAIN_OFF[n]:MAIN_OFF[n] + sz]
    pieces["f"] = small[..., 0:8]
    pieces["dt"] = small[..., 8:24]
    return jnp.concatenate([pieces[n] for n in IN_NAMES], axis=-1)


def kernel(*args):
    names = ["x", "norm_w", "w_in", "fg_bias", "ssm_conv_w", "ssm_conv_b", "dt_bias", "a_log", "d_skip",
             "ssm_norm_w", "sc_conv_w", "sc_conv_b", "cf_conv_w", "cf_conv_b", "cf_ln_w", "cf_ln_b", "w_gate",
             "b_gate", "w_branch", "w_out", "final_norm_w"]
    wnames = names[1:]
    a = dict(zip(names, args[:21]))
    target = args[21]
    mom = dict(zip(["m_" + n for n in wnames], args[22:42]))
    mom.update(zip(["v_" + n for n in wnames], args[42:62]))
    NL = a["w_in"].shape[0]
    x0 = a["x"][0]
    tgt = target[0]
    me = 4 * lax.axis_index("x") + 2 * lax.axis_index("y") + lax.axis_index("c")

    def rows2d(t):
        return t.reshape(-1, t.shape[-1])

    def shards(l):
        return [rows2d(a[n][l]).astype(bf16) for n in _BIG]

    (sm_all,) = _allgather([_pack([a[n] for n in _SHSM], f32)], name="gather_small")
    ssm_cw_g, sc_cw_g, cf_cw_g, b_gate_g = _unpack(sm_all, [a[n].shape for n in _SHSM])

    def cat_last(gathered):
        return jnp.moveaxis(gathered, 0, -2).reshape(gathered.shape[1:-1] + (-1,))

    ssm_cw, sc_cw, cf_cw, b_gate_full = cat_last(ssm_cw_g), cat_last(sc_cw_g), cat_last(cf_cw_g), cat_last(b_gate_g)
    SH = D_MODEL // N_DEV

    def layer_params(l, gathered):
        g_in, g_gate, g_branch, g_out = gathered
        w_main, w_small = _main_cols(cat_last(g_in))
        w_gate = jnp.moveaxis(g_gate.reshape(N_DEV, 4, SH, D_MODEL), 0, 1).reshape(4, D_MODEL, D_MODEL)
        w_branch = cat_last(g_branch.reshape(N_DEV, 4, BW, SH))
        return dict(
            norm_w=a["norm_w"][l][None], w_main=w_main, w_small=w_small,
            fb_row=_row128(a["fg_bias"][l], 0), dtb_row=_row128(a["dt_bias"][l], HL),
            alog_row=_row128(a["a_log"][l], HL), dskip_row=_row128(a["d_skip"][l], HL),
            ssm_conv_w=ssm_cw[l], ssm_conv_b=a["ssm_conv_b"][l][None], ssm_norm_w=a["ssm_norm_w"][l][None],
            sc_conv_w=sc_cw[l], sc_conv_b=a["sc_conv_b"][l][None],
            cf_conv_w=cf_cw[l], cf_conv_b=a["cf_conv_b"][l][None],
            cf_ln_w=a["cf_ln_w"][l][None], cf_ln_b=a["cf_ln_b"][l][None],
            w_gate=[w_gate[i] for i in range(4)], b_gate=[b_gate_full[l, i][None] for i in range(4)],
            w_branch=[w_branch[i] for i in range(4)], w_out=g_out.reshape(D_MODEL, D_MODEL))

    def slabs(g):
        d_in = _orig_cols(g["w_main"], g["w_small"])
        s_in = jnp.moveaxis(d_in.reshape(D_MODEL, N_DEV, -1), 1, 0)
        s_gate = jnp.moveaxis(jnp.stack(g["w_gate"]).reshape(4, N_DEV, SH, D_MODEL), 1, 0)
        s_branch = jnp.moveaxis(jnp.stack(g["w_branch"]).reshape(4, BW, N_DEV, SH), 2, 0)
        return [s_in, s_gate.reshape(N_DEV, 4 * SH, D_MODEL), s_branch.reshape(N_DEV, 4 * BW, SH),
                g["w_out"].reshape(N_DEV, SH, D_MODEL)]

    gathered = _allgather(shards(0), name="gather_l0")
    xl, saved, layers = x0, [], []
    for l in range(NL):
        layers.append(layer_params(l, gathered))
        nxt = shards(l + 1) if l + 1 < NL else None
        sides = (("gather", nxt[:1]), ("gather", nxt[1:])) if nxt is not None else (None, None)
        xl, s, gathered = _layer_fwd(xl, layers[l], f"l{l}", sides)
        saved.append(s)
    loss_part, dx, d_fnw = _loss_head(xl, a["final_norm_w"][None], tgt, name="loss_head")

    grads, got = [None] * NL, [None] * NL
    for l in reversed(range(NL)):
        prev = slabs(grads[l + 1]) if l + 1 < NL else None
        sides = (("exchange", prev[1:]), ("exchange", prev[:1])) if prev is not None else (None, None)
        dx, grads[l], res = _layer_bwd(dx, layers[l], saved[l], f"l{l}", sides)
        if prev is not None:
            got[l + 1] = res[-1:] + res[:-1]
    got[0] = _exchange(slabs(grads[0]), name="exchange_l0")

    def stack(key):
        return jnp.stack([grads[l][key] for l in range(NL)])

    big_res = [[], [], [], []]
    for j, n in enumerate(_BIG):
        outs = _adamw([got[l][j] for l in range(NL)], rows2d(a[n]), rows2d(mom["m_" + n]),
                      rows2d(mom["v_" + n]), name=f"adamw_{n}")
        for k in range(4):
            big_res[k].append(outs[k].reshape(a[n].shape))

    def vec(row, off, n):
        return row[0, off:off + n]

    small_grads = {
        "norm_w": jnp.concatenate([grads[l]["norm_w"] for l in range(NL)]),
        "fg_bias": jnp.stack([vec(grads[l]["fb_row"], 0, FOX_HEADS) for l in range(NL)]),
        "ssm_conv_b": jnp.concatenate([grads[l]["ssm_conv_b"] for l in range(NL)]),
        "dt_bias": jnp.stack([vec(grads[l]["dtb_row"], HL, SSM_HEADS) for l in range(NL)]),
        "a_log": jnp.stack([vec(grads[l]["alog_row"], HL, SSM_HEADS) for l in range(NL)]),
        "d_skip": jnp.stack([vec(grads[l]["dskip_row"], HL, SSM_HEADS) for l in range(NL)]),
        "ssm_norm_w": jnp.concatenate([grads[l]["ssm_norm_w"] for l in range(NL)]),
        "sc_conv_b": jnp.concatenate([grads[l]["sc_conv_b"] for l in range(NL)]),
        "cf_conv_b": jnp.concatenate([grads[l]["cf_conv_b"] for l in range(NL)]),
        "cf_ln_w": jnp.concatenate([grads[l]["cf_ln_w"] for l in range(NL)]),
        "cf_ln_b": jnp.concatenate([grads[l]["cf_ln_b"] for l in range(NL)]),
        "final_norm_w": d_fnw[0],
        "ssm_conv_w": stack("ssm_conv_w"), "sc_conv_w": stack("sc_conv_w"), "cf_conv_w": stack("cf_conv_w"),
        "b_gate": jnp.stack([jnp.concatenate(grads[l]["b_gate"]) for l in range(NL)]),
    }
    order = _REPL + _SHSM
    full_shapes = [small_grads[n].shape for n in order] + [(1,)]
    (sg_all,) = _allgather([_pack([small_grads[n] for n in order] + [loss_part.reshape(1)], f32)],
                           name="gather_small_grads")

    def mine(n, full):
        if n in _REPL:
            return full
        sz = a[n].shape[-1]
        return lax.dynamic_slice_in_dim(full, me * sz, sz, axis=full.ndim - 1)

    g_full = _unpack(_sum_slabs(sg_all, name="sum_small"), full_shapes)
    loss = g_full[-1][0]
    g_mine = {n: mine(n, gf).reshape(a[n].shape) for n, gf in zip(order, g_full[:-1])}
    gs1 = _pack([g_mine[n] for n in order], f32)
    ws1 = _pack([a[n] for n in order], f32)
    ms1 = _pack([mom["m_" + n] for n in order], f32)
    vs1 = _pack([mom["v_" + n] for n in order], f32)
    sm_out = _adamw([gs1[None]], ws1, ms1, vs1, name="adamw_small")
    sm_res = [_unpack(o, [a[n].shape for n in order]) for o in sm_out]

    res = []
    for k in range(4):
        d = dict(zip(_BIG, big_res[k]))
        d.update(zip(order, sm_res[k]))
        res.append(d)
    outs = [loss, dx[None]]
    for k in range(4):
        outs += [res[k][n] for n in wnames]
    return tuple(outs)
```

```python
import functools

import jax
import jax.numpy as jnp
from jax import lax
from jax.experimental import pallas as pl
from jax.experimental.pallas import tpu as pltpu

f32 = jnp.float32
bf16 = jnp.bfloat16
SDS = jax.ShapeDtypeStruct

N_DEV = 8
LANES = 128
VMEM_LIMIT = 48 * 1024 * 1024
EPS = 1e-6
NEG = -1e30

D_MODEL = 2048
FOX_HEADS = 8
FOX_DH = 128
SSM_HEADS = 16
SSM_P = 64
SSM_N = 128
SSM_G = 2
SSM_W = 1024
XBC_W = 1536
SSM_K, SC_K, CF_K = 4, 3, 31
BW = 1024
SSD_L = 128
HL = 8
CONV_HALO = 32

ADAM_LR, ADAM_B1, ADAM_B2, ADAM_EPS, ADAM_WD, ADAM_STEP = 0.001, 0.9, 0.999, 1e-08, 0.01, 10

IN_NAMES = ("q", "k", "v", "f", "ga", "z", "xbc", "dt", "scb", "scc", "scx", "gc", "glua", "glug", "gd")
IN_SIZES = (1024, 1024, 1024, 8, 1024, 1024, 1536, 16, 1024, 1024, 1024, 1024, 1024, 1024, 1024)
N_IN = sum(IN_SIZES)
MAIN_NAMES = ("q", "k", "v", "ga", "z", "scb", "scc", "scx", "gc", "glua", "glug", "gd", "xbc")
MAIN_OFF = {n: 1024 * i for i, n in enumerate(MAIN_NAMES)}
N_MAIN = 12 * 1024 + XBC_W


def _pcall(body, **kw):
    return pl.pallas_call(body, **kw)


def _cp(*sem):
    return pltpu.CompilerParams(dimension_semantics=sem if sem else None, vmem_limit_bytes=VMEM_LIMIT)


def _pick(n, cands):
    for c in cands:
        if c <= n and n % c == 0:
            return c
    return n


_DIMS = {"nn": ((1,), (0,)), "nt": ((1,), (1,)), "tn": ((0,), (0,))}
_MESH = pl.DeviceIdType.MESH
_HBM = pl.BlockSpec(memory_space=pltpu.HBM)


def _side_specs(side):
    kind, arrs = side
    n = len(arrs)
    shapes = [SDS(((N_DEV,) + a.shape) if kind == "gather" else a.shape, a.dtype) for a in arrs]
    sems = [pltpu.SemaphoreType.DMA((7 * n,)), pltpu.SemaphoreType.DMA((7 * n,)), pltpu.SemaphoreType.DMA((n,))]
    return [_HBM] * n, [_HBM] * n, shapes, sems


def _side_copies(kind, in_refs, out_refs, send_sems, recv_sems, local_sems):
    x, y, c = lax.axis_index("x"), lax.axis_index("y"), lax.axis_index("c")
    me = 4 * x + 2 * y + c
    sends, recvs = [], []
    for a in range(len(in_refs)):
        src_mine = in_refs[a] if kind == "gather" else in_refs[a].at[me]
        sends.append(pltpu.make_async_copy(src_mine, out_refs[a].at[me], local_sems.at[a]))
    for k in range(1, N_DEV):
        px, py, pc = x ^ ((k >> 2) & 1), y ^ ((k >> 1) & 1), c ^ (k & 1)
        peer = 4 * px + 2 * py + pc
        for a in range(len(in_refs)):
            src = in_refs[a] if kind == "gather" else in_refs[a].at[peer]

            def copy(src_ref, dst_slot):
                return pltpu.make_async_remote_copy(
                    src_ref=src_ref, dst_ref=out_refs[a].at[dst_slot],
                    send_sem=send_sems.at[7 * a + k - 1], recv_sem=recv_sems.at[7 * a + k - 1],
                    device_id=(px, py, pc), device_id_type=_MESH)

            sends.append(copy(src, me))
            recvs.append(copy(src, peer))
    return sends, recvs


def _side_run(side, refs, first, last):
    n = len(side[1])
    sends, recvs = _side_copies(side[0], refs[:n], refs[n:2 * n], *refs[2 * n:])

    @pl.when(first)
    def _():
        for cp in sends:
            cp.start()

    @pl.when(last)
    def _():
        for cp in recvs:
            cp.wait_recv()
        for cp in sends[n:]:
            cp.wait_send()
        for cp in sends[:n]:
            cp.wait()


def _mm(a, b, *, mode, out_dtype, name, add=None, tm=512, tn=None, tk=None, side=None):
    if mode == "nn":
        (M, K), (_, N) = a.shape, b.shape
    elif mode == "nt":
        (M, K), (N, _) = a.shape, b.shape
    else:
        (K, M), (_, N) = a.shape, b.shape
    tm = _pick(M, (tm, 256, 128))
    tn = _pick(N, (tn,) if tn else (1536, 1024, 512, 256, 128))
    tk = _pick(K, (tk,) if tk else (2048, 1536, 1024, 512, 256, 128))
    nk = K // tk
    a_spec = (pl.BlockSpec((tk, tm), lambda j, i, k: (k, i)) if mode == "tn"
              else pl.BlockSpec((tm, tk), lambda j, i, k: (i, k)))
    b_spec = (pl.BlockSpec((tn, tk), lambda j, i, k: (j, k)) if mode == "nt"
              else pl.BlockSpec((tk, tn), lambda j, i, k: (k, j)))
    o_spec = pl.BlockSpec((tm, tn), lambda j, i, k: (i, j))
    ins, specs = [a, b], [a_spec, b_spec]
    if add is not None:
        ins.append(add)
        specs.append(o_spec)
    n_in = len(ins)
    dims = (_DIMS[mode], ((), ()))
    out_specs, out_shape = [o_spec], [SDS((M, N), out_dtype)]
    scratch = [pltpu.VMEM((tm, tn), f32)] if nk > 1 else []
    ns = 0
    if side is not None:
        ns = len(side[1])
        s_in, s_out, s_shapes, s_sems = _side_specs(side)
        ins, specs = ins + list(side[1]), specs + s_in
        out_specs, out_shape, scratch = out_specs + s_out, out_shape + s_shapes, scratch + s_sems
    grid = (N // tn, M // tm, nk)

    def body(*refs):
        a_ref, b_ref = refs[0], refs[1]
        add_ref = refs[2] if add is not None else None
        o_ref = refs[n_in + ns]
        rest = refs[n_in + 2 * ns + 1:]
        jj, ii, k = pl.program_id(0), pl.program_id(1), pl.program_id(2)
        if side is not None:
            side_refs = refs[n_in:n_in + ns] + refs[n_in + ns + 1:n_in + 2 * ns + 1] + rest[-3:]
            _side_run(side, side_refs, (jj == 0) & (ii == 0) & (k == 0),
                      (jj == grid[0] - 1) & (ii == grid[1] - 1) & (k == nk - 1))
        p = lax.dot_general(a_ref[...], b_ref[...], dims, preferred_element_type=f32)

        def finish(v):
            if add_ref is not None:
                v = v + add_ref[...].astype(f32)
            o_ref[...] = v.astype(o_ref.dtype)

        if nk == 1:
            finish(p)
        else:
            acc = rest[0]

            @pl.when(k == 0)
            def _():
                acc[...] = p

            @pl.when(k > 0)
            def _():
                acc[...] += p

            @pl.when(k == nk - 1)
            def _():
                finish(acc[...])

    res = _pcall(
        body, name=name, grid=grid, in_specs=specs, out_specs=out_specs, out_shape=out_shape,
        scratch_shapes=scratch,
        compiler_params=_cp("parallel", "parallel", "arbitrary") if side is None else _cp(*(("arbitrary",) * 3)),
    )(*ins)
    return res[0] if side is None else res


def _silu(x):
    return x * jax.nn.sigmoid(x)


def _softplus(x):
    return jnp.maximum(x, 0.0) + jnp.log(1.0 + jnp.exp(-jnp.abs(x)))


def _mean(x):
    return jnp.mean(x, axis=-1, keepdims=True)


def _f_rms(x, w):
    return (x * lax.rsqrt(_mean(x * x) + EPS) * w,)


def _f_rms_res(x, w):
    return _f_rms(x, w) + (x,)


def _f_ya(o, g):
    return (o * _silu(g),)


def _f_yb(y, z, w):
    t = y * _silu(z)
    return (t * lax.rsqrt(_mean(t * t) + EPS) * w,)


def _f_mul(a, b):
    return (a * b,)


def _f_yc(b, cv, g):
    return (b * cv * _silu(g),)


def _f_glu(a, g):
    return (a * jax.nn.sigmoid(g),)


def _f_yd(cf, g, lw, lb):
    mu = _mean(cf)
    var = _mean(jnp.square(cf - mu))
    ln = (cf - mu) * lax.rsqrt(var + EPS) * lw + lb
    return (_silu(ln) * _silu(g),)


def _f_silu(x):
    return (_silu(x),)


def _f_merge(g0, g1, g2, g3, p0, p1, p2, p3, b0, b1, b2, b3):
    sg = jax.nn.sigmoid
    return (sg(g0 + b0) * p0 + sg(g1 + b1) * p1 + sg(g2 + b2) * p2 + sg(g3 + b3) * p3,)


def _row_specs(ins, params, tt, cw):
    specs = [pl.BlockSpec((tt, cw), functools.partial(lambda j, i, base: (i, base + j), base=base))
             for _, base in ins]
    specs += [pl.BlockSpec((p.shape[0], cw), lambda j, i: (0, j)) for p in params]
    return specs


def _rowwise(fn, ins, params, out_dtypes, *, cw, ncol, name, tt=256):
    T = ins[0][0].shape[0]
    tt = _pick(T, (tt, 128))
    ni, npar = len(ins), len(params)

    def body(*refs):
        xs = [r[...].astype(f32) for r in refs[:ni]]
        ps = [r[...] for r in refs[ni:ni + npar]]
        ys = fn(*xs, *ps)
        for o_ref, y in zip(refs[ni + npar:], ys):
            o_ref[...] = y.astype(o_ref.dtype)

    out_spec = pl.BlockSpec((tt, cw), lambda j, i: (i, j))
    outs = _pcall(
        body, name=name, grid=(ncol, T // tt), in_specs=_row_specs(ins, params, tt, cw),
        out_specs=[out_spec] * len(out_dtypes),
        out_shape=[SDS((T, cw * ncol), dt) for dt in out_dtypes],
        compiler_params=_cp("parallel", "parallel"),
    )(*[a for a, _ in ins], *params)
    return outs


def _rowwise_bwd(fn, ins, params, cots, din_dtypes, *, cw, ncol, name, tt=256):
    T = ins[0][0].shape[0]
    tt = _pick(T, (tt, 128))
    ni, npar, nc = len(ins), len(params), len(cots)
    keep = [k for k, dt in enumerate(din_dtypes) if dt is not None]

    def body(*refs):
        xs = [r[...].astype(f32) for r in refs[:ni]]
        ps = [r[...] for r in refs[ni:ni + npar]]
        cs = tuple(r[...].astype(f32) for r in refs[ni + npar:ni + npar + nc])
        outs = refs[ni + npar + nc:]
        _, vjp = jax.vjp(fn, *xs, *ps)
        g = vjp(cs)
        for o_ref, k in zip(outs[:len(keep)], keep):
            o_ref[...] = g[k].astype(o_ref.dtype)
        i = pl.program_id(1)
        for o_ref, gp in zip(outs[len(keep):], g[ni:]):
            @pl.when(i == 0)
            def _(o_ref=o_ref, gp=gp):
                o_ref[...] = gp

            @pl.when(i > 0)
            def _(o_ref=o_ref, gp=gp):
                o_ref[...] += gp

    row_spec = pl.BlockSpec((tt, cw), lambda j, i: (i, j))
    in_specs = _row_specs(ins, params, tt, cw) + [row_spec] * nc
    out_specs = [row_spec] * len(keep) + [pl.BlockSpec((p.shape[0], cw), lambda j, i: (0, j)) for p in params]
    out_shape = [SDS((T, cw * ncol), din_dtypes[k]) for k in keep] + [SDS(p.shape, f32) for p in params]
    return _pcall(
        body, name=name, grid=(ncol, T // tt), in_specs=in_specs, out_specs=out_specs, out_shape=out_shape,
        compiler_params=_cp("arbitrary", "arbitrary"),
    )(*[a for a, _ in ins], *params, *cots)


CONV_RB = 64
SUBLANES = 8


def _shifted_rows(buf, n):
    for sft in range(1, SUBLANES):
        buf[sft, 0:n, :] = buf[0, pl.ds(sft, n), :]


def _tap(buf, o, rb, shifted):
    if shifted:
        return buf[o % SUBLANES, o - o % SUBLANES:o - o % SUBLANES + rb, :]
    return buf[0, pl.ds(o, rb), :]


def _conv_fwd(x, xbase, C, w, b, *, with_silu, name, out_dtype=bf16, tt=512, cw=512):
    xa = x
    T = xa.shape[0]
    K = w.shape[0]
    tt = _pick(T, (tt,))
    hp = CONV_HALO
    r = tt // hp
    ncol = C // cw
    rb = min(CONV_RB, tt)
    n = hp + tt
    shifted = K > SUBLANES

    def body(cur_ref, halo_ref, w_ref, b_ref, *rest):
        outs, ext = rest[:-1], rest[-1]
        i = pl.program_id(1)
        ext[0, 0:hp, :] = jnp.where(i > 0, halo_ref[...].astype(f32), 0.0)
        ext[0, hp:n, :] = cur_ref[...].astype(f32)
        ext[0, n:, :] = jnp.zeros((SUBLANES, cw), f32)
        if shifted:
            _shifted_rows(ext, n)
        for r0 in range(0, tt, rb):
            acc = jnp.zeros((rb, cw), f32) + b_ref[...]
            for k in range(K):
                acc = acc + w_ref[k:k + 1, :] * _tap(ext, hp - (K - 1) + k + r0, rb, shifted)
            outs[0][r0:r0 + rb, :] = acc.astype(outs[0].dtype)
            if with_silu:
                outs[1][r0:r0 + rb, :] = _silu(acc).astype(outs[1].dtype)

    n_out = 2 if with_silu else 1
    o_spec = pl.BlockSpec((tt, cw), lambda j, i: (i, j))
    return _pcall(
        body, name=name, grid=(ncol, T // tt),
        in_specs=[pl.BlockSpec((tt, cw), lambda j, i: (i, xbase + j)),
                  pl.BlockSpec((hp, cw), lambda j, i: (jnp.maximum(i * r - 1, 0), xbase + j)),
                  pl.BlockSpec((K, cw), lambda j, i: (0, j)),
                  pl.BlockSpec((1, cw), lambda j, i: (0, j))],
        out_specs=[o_spec] * n_out, out_shape=[SDS((T, C), out_dtype)] * n_out,
        scratch_shapes=[pltpu.VMEM((SUBLANES if shifted else 1, n + SUBLANES, cw), f32)],
        compiler_params=_cp("parallel", "parallel"),
    )(xa, xa, w, b)


def _conv_bwd(dy, x, xbase, C, w, *, name, dx_dtype=bf16, tt=512, cw=256):
    T = dy.shape[0]
    K = w.shape[0]
    tt = _pick(T, (tt,))
    hp = CONV_HALO
    r = tt // hp
    nt = T // tt
    ncol = C // cw
    rb = min(CONV_RB, tt)
    n = tt + hp
    shifted = K > SUBLANES

    def fold8(v):
        out = v[0:SUBLANES]
        for g in range(1, rb // SUBLANES):
            out = out + v[SUBLANES * g:SUBLANES * (g + 1)]
        return out

    def body(dy_ref, dyn_ref, x_ref, xh_ref, w_ref, dx_ref, dw_ref, db_ref, exd, exx, dwp):
        i = pl.program_id(1)
        exd[0, 0:tt, :] = dy_ref[...].astype(f32)
        exd[0, tt:n, :] = jnp.where(i < nt - 1, dyn_ref[...].astype(f32), 0.0)
        exd[0, n:, :] = jnp.zeros((SUBLANES, cw), f32)
        exx[0, 0:hp, :] = jnp.where(i > 0, xh_ref[...].astype(f32), 0.0)
        exx[0, hp:n, :] = x_ref[...].astype(f32)
        exx[0, n:, :] = jnp.zeros((SUBLANES, cw), f32)
        if shifted:
            _shifted_rows(exd, n)
            _shifted_rows(exx, n)

        @pl.when(i == 0)
        def _():
            dwp[...] = jnp.zeros_like(dwp)

        for r0 in range(0, tt, rb):
            dyc = exd[0, r0:r0 + rb, :]
            acc = jnp.zeros((rb, cw), f32)
            for k in range(K):
                acc = acc + w_ref[k:k + 1, :] * _tap(exd, K - 1 - k + r0, rb, shifted)
                dwp[SUBLANES * k:SUBLANES * (k + 1), :] += fold8(dyc * _tap(exx, hp - (K - 1) + k + r0, rb, shifted))
            dx_ref[r0:r0 + rb, :] = acc.astype(dx_ref.dtype)
            dwp[SUBLANES * K:SUBLANES * (K + 1), :] += fold8(dyc)

        @pl.when(i == nt - 1)
        def _():
            for k in range(K):
                dw_ref[k:k + 1, :] = jnp.sum(dwp[SUBLANES * k:SUBLANES * (k + 1), :], axis=0, keepdims=True)
            db_ref[...] = jnp.sum(dwp[SUBLANES * K:SUBLANES * (K + 1), :], axis=0, keepdims=True)

    nsh = SUBLANES if shifted else 1
    return _pcall(
        body, name=name, grid=(ncol, nt),
        in_specs=[pl.BlockSpec((tt, cw), lambda j, i: (i, j)),
                  pl.BlockSpec((hp, cw), lambda j, i: (jnp.minimum((i + 1) * r, nt * r - 1), j)),
                  pl.BlockSpec((tt, cw), lambda j, i: (i, xbase + j)),
                  pl.BlockSpec((hp, cw), lambda j, i: (jnp.maximum(i * r - 1, 0), xbase + j)),
                  pl.BlockSpec((K, cw), lambda j, i: (0, j))],
        out_specs=[pl.BlockSpec((tt, cw), lambda j, i: (i, j)),
                   pl.BlockSpec((K, cw), lambda j, i: (0, j)),
                   pl.BlockSpec((1, cw), lambda j, i: (0, j))],
        out_shape=[SDS((T, C), dx_dtype), SDS((K, C), f32), SDS((1, C), f32)],
        scratch_shapes=[pltpu.VMEM((nsh, n + SUBLANES, cw), f32), pltpu.VMEM((nsh, n + SUBLANES, cw), f32),
                        pltpu.VMEM((SUBLANES * (K + 1), cw), f32)],
        compiler_params=_cp("arbitrary", "arbitrary"),
    )(dy, dy, x, x, w)


def _tri(n, lower):
    r = lax.broadcasted_iota(jnp.int32, (n, n), 0)
    c = lax.broadcasted_iota(jnp.int32, (n, n), 1)
    return jnp.where((r >= c) if lower else (r <= c), 1.0, 0.0).astype(f32)


def _hdot(a, b):
    return jnp.dot(a, b, precision=lax.Precision.HIGHEST, preferred_element_type=f32)


def _fox_gate_fwd(small, fb_row, *, name, tt=512):
    T = small.shape[0]
    tt = _pick(T, (tt,))

    def body(sm_ref, fb_ref, ck_ref, carry):
        @pl.when(pl.program_id(0) == 0)
        def _():
            carry[...] = jnp.zeros_like(carry)

        logf = -_softplus(-(sm_ref[...] + fb_ref[...]))
        cs = _hdot(_tri(tt, True), logf) + carry[...]
        c2 = cs * LOG2E
        for h in range(FOX_HEADS):
            ck_ref[h] = c2[:, h:h + 1]
        carry[...] = cs[tt - 1:tt, :]

    blk = pl.BlockSpec((tt, LANES), lambda i: (i, 0))
    return _pcall(
        body, name=name, grid=(T // tt,), in_specs=[blk, pl.BlockSpec((1, LANES), lambda i: (0, 0))],
        out_specs=pl.BlockSpec((FOX_HEADS, tt, 1), lambda i: (0, i, 0)),
        out_shape=SDS((FOX_HEADS, T, 1), f32), scratch_shapes=[pltpu.VMEM((1, LANES), f32)],
        compiler_params=_cp("arbitrary"),
    )(small, fb_row)


def _fox_gate_bwd(dc, small, fb_row, dsmall_dt, *, name, tt=512):
    T = small.shape[0]
    tt = _pick(T, (tt,))
    nt = T // tt

    def body(dc_ref, sm_ref, fb_ref, dd_ref, ds_ref, dfb_ref, carry):
        @pl.when(pl.program_id(0) == 0)
        def _():
            carry[...] = jnp.zeros_like(carry)
            dfb_ref[...] = jnp.zeros_like(dfb_ref)

        dl = _hdot(_tri(tt, False), dc_ref[...]) + carry[...]
        carry[...] = dl[0:1, :]
        lane = lax.broadcasted_iota(jnp.int32, (tt, LANES), 1)
        df = jnp.where(lane < FOX_HEADS, dl * jax.nn.sigmoid(-(sm_ref[...] + fb_ref[...])), 0.0)
        ds_ref[...] = jnp.where(lane < FOX_HEADS, df, dd_ref[...])
        dfb_ref[...] += jnp.sum(df, axis=0, keepdims=True)

    blk = pl.BlockSpec((tt, LANES), lambda i: (nt - 1 - i, 0))
    row = pl.BlockSpec((1, LANES), lambda i: (0, 0))
    return _pcall(
        body, name=name, grid=(nt,), in_specs=[blk, blk, row, blk], out_specs=[blk, row],
        out_shape=[SDS((T, LANES), f32), SDS((1, LANES), f32)], scratch_shapes=[pltpu.VMEM((1, LANES), f32)],
        compiler_params=_cp("arbitrary"),
    )(dc, small, fb_row, dsmall_dt)


_NT = (((1,), (1,)), ((), ()))
_TN = (((0,), (0,)), ((), ()))


def _causal(n):
    r = lax.broadcasted_iota(jnp.int32, (n, n), 0)
    c = lax.broadcasted_iota(jnp.int32, (n, n), 1)
    return r >= c


LOG2E = 1.4426950408889634
_SCALE = FOX_DH ** -0.5
_SCALE2 = _SCALE * LOG2E
CW = 256
FLASH_TQ = 1024
FLASH_CH = 1024
FLASH_TK = 1024


def _to_t(a, w):
    T = a.shape[0]
    return a.reshape(T // w, w, FOX_HEADS, FOX_DH).transpose(2, 0, 3, 1)


def _from_t(at):
    hh, n, dh, w = at.shape
    return at.transpose(1, 3, 0, 2).reshape(n * w, hh * dh)


def _flash_fwd(u, qT, vT, ck2, *, name, side=None):
    T = u.shape[0]
    H, dh = FOX_HEADS, FOX_DH
    TQ, CH = min(FLASH_TQ, T), min(FLASH_CH, T)
    NC, NQ, RQ, RC, R = T // CW, T // TQ, TQ // CW, CH // CW, TQ // CH
    ko = MAIN_OFF["k"] // dh
    ns = len(side[1]) if side is not None else 0
    s_in, s_out, s_shapes, s_sems = _side_specs(side) if side is not None else ([], [], [], [])

    def body(qT_ref, k_ref, vT_ref, ck_ref, *rest):
        oT_ref, lse_ref = rest[ns], rest[ns + 1]
        i = pl.program_id(1)
        if side is not None:
            hh = pl.program_id(0)
            _side_run(side, rest[:ns] + rest[ns + 2:], (hh == 0) & (i == 0), (hh == H - 1) & (i == NQ - 1))
        qt = jnp.concatenate([qT_ref[r] for r in range(RQ)], axis=1) if RQ > 1 else qT_ref[0]

        def scores(j):
            st = pl.multiple_of(j * CH, CH)
            kc = k_ref[pl.ds(st, CH), :]
            return jnp.dot(kc, qt, preferred_element_type=f32) * _SCALE2 - ck_ref[pl.ds(st, CH), :]

        def update(j, s, carry):
            m, l, acc = carry
            m_new = jnp.maximum(m, jnp.max(s, axis=0, keepdims=True))
            alpha = jnp.exp2(m - m_new)
            p = jnp.exp2(s - m_new)
            l = alpha * l + jnp.sum(p, axis=0, keepdims=True)
            pb = p.astype(bf16)
            pv = jnp.dot(vT_ref[j * RC], pb[0:CW], preferred_element_type=f32)
            for r in range(1, RC):
                pv = pv + jnp.dot(vT_ref[j * RC + r], pb[r * CW:(r + 1) * CW], preferred_element_type=f32)
            return m_new, l, alpha * acc + pv

        def band(s, d):
            r = lax.broadcasted_iota(jnp.int32, (CH, TQ), 0) + d * CH
            c = lax.broadcasted_iota(jnp.int32, (CH, TQ), 1)
            return jnp.where(r <= c, s, NEG)

        carry = (jnp.full((1, TQ), NEG, f32), jnp.zeros((1, TQ), f32), jnp.zeros((dh, TQ), f32))
        carry = lax.fori_loop(0, i * R, lambda j, c: update(j, scores(j), c), carry)
        for d in range(R):
            carry = update(i * R + d, band(scores(i * R + d), d), carry)
        m, l, acc = carry
        o = (acc / l).astype(oT_ref.dtype)
        lse = m + jnp.log2(l)
        for r in range(RQ):
            oT_ref[r] = o[:, r * CW:(r + 1) * CW]
            lse_ref[r] = lse[:, r * CW:(r + 1) * CW]

    return _pcall(
        body, name=name, grid=(H, NQ),
        in_specs=[pl.BlockSpec((None, RQ, dh, CW), lambda h, i: (h, i, 0, 0)),
                  pl.BlockSpec((T, dh), lambda h, i: (0, ko + h)),
                  pl.BlockSpec((None, NC, dh, CW), lambda h, i: (h, 0, 0, 0)),
                  pl.BlockSpec((None, T, 1), lambda h, i: (h, 0, 0))] + s_in,
        out_specs=[pl.BlockSpec((None, RQ, dh, CW), lambda h, i: (h, i, 0, 0)),
                   pl.BlockSpec((None, RQ, 1, CW), lambda h, i: (h, i, 0, 0))] + s_out,
        out_shape=[SDS((H, NC, dh, CW), bf16), SDS((H, NC, 1, CW), f32)] + s_shapes,
        scratch_shapes=s_sems,
        compiler_params=_cp("parallel", "arbitrary") if side is None else _cp("arbitrary", "arbitrary"),
    )(qT, u, vT, ck2, *(side[1] if side is not None else []))


def _flash_delta(doT, oT, *, name):
    H, NC, dh, _ = doT.shape

    def body(a_ref, b_ref, d_ref):
        for i in range(NC):
            d_ref[i] = jnp.sum(a_ref[i].astype(f32) * b_ref[i].astype(f32), axis=0, keepdims=True)

    blk = pl.BlockSpec((None, NC, dh, CW), lambda h: (h, 0, 0, 0))
    return _pcall(
        body, name=name, grid=(H,), in_specs=[blk, blk],
        out_specs=pl.BlockSpec((None, NC, 1, CW), lambda h: (h, 0, 0, 0)),
        out_shape=SDS((H, NC, 1, CW), f32), compiler_params=_cp("parallel"),
    )(doT, oT)


def _flash_bwd(u, qT, kT, do, doT, ck2, lse2, delta, *, name, side=None):
    T = u.shape[0]
    H, dh = FOX_HEADS, FOX_DH
    TK = min(FLASH_TK, T)
    NC, NK, R = T // CW, T // TK, TK // CW
    qo, ko, vo = MAIN_OFF["q"] // dh, MAIN_OFF["k"] // dh, MAIN_OFF["v"] // dh
    ns = len(side[1]) if side is not None else 0
    s_in, s_out, s_shapes, s_sems = _side_specs(side) if side is not None else ([], [], [], [])

    def body(q_ref, k_ref, v_ref, qT_ref, kT_ref, do_ref, doT_ref, ck_ref, lse_ref, dl_ref, *rest):
        dk_ref, dv_ref, dck_ref, dqT_ref, dcq_ref = rest[ns:ns + 5]
        dk_s, dv_s, dc_s = rest[2 * ns + 5:2 * ns + 8]
        j = pl.program_id(1)
        if side is not None:
            hh = pl.program_id(0)
            _side_run(side, rest[:ns] + rest[ns + 5:2 * ns + 5] + rest[2 * ns + 8:],
                      (hh == 0) & (j == 0), (hh == H - 1) & (j == NK - 1))

        @pl.when(j == 0)
        def _():
            dqT_ref[...] = jnp.zeros_like(dqT_ref)
            dcq_ref[...] = jnp.zeros_like(dcq_ref)

        dk_s[...] = jnp.zeros_like(dk_s)
        dv_s[...] = jnp.zeros_like(dv_s)
        dc_s[...] = jnp.zeros_like(dc_s)
        k, v, kt = k_ref[...], v_ref[...], kT_ref[...]
        ckc = ck_ref[...]

        def chunk(i, d):
            st = pl.multiple_of(i * CW, CW)
            s = jnp.dot(k, qT_ref[i], preferred_element_type=f32) * _SCALE2 - ckc
            if d is not None:
                r = lax.broadcasted_iota(jnp.int32, (TK, CW), 0)
                c = lax.broadcasted_iota(jnp.int32, (TK, CW), 1) + d * CW
                s = jnp.where(r <= c, s, NEG)
            p = jnp.exp2(s - lse_ref[i])
            dp = jnp.dot(v, doT_ref[i], preferred_element_type=f32)
            ds = p * (dp - dl_ref[i])
            dsb = ds.astype(bf16)
            dv_s[...] += jnp.dot(p.astype(bf16), do_ref[pl.ds(st, CW), :], preferred_element_type=f32)
            dk_s[...] += jnp.dot(dsb, q_ref[pl.ds(st, CW), :], preferred_element_type=f32)
            dqT_ref[i] += jnp.dot(kt, dsb, preferred_element_type=f32)
            dcq_ref[i] += jnp.sum(ds, axis=0, keepdims=True)
            part = ds[:, 0:LANES]
            for t in range(1, CW // LANES):
                part = part + ds[:, t * LANES:(t + 1) * LANES]
            dc_s[...] += part

        for d in range(R):
            chunk(j * R + d, d)

        def rest(i, c):
            chunk(i, None)
            return c

        lax.fori_loop(j * R + R, NC, rest, 0)
        dk_ref[...] = (dk_s[...] * _SCALE).astype(dk_ref.dtype)
        dv_ref[...] = dv_s[...].astype(dv_ref.dtype)
        dck_ref[...] = -jnp.sum(dc_s[...], axis=1, keepdims=True)

        @pl.when(j == NK - 1)
        def _():
            dqT_ref[...] = dqT_ref[...] * _SCALE

    res = lambda h, j: (h, 0, 0, 0)
    return _pcall(
        body, name=name, grid=(H, NK),
        in_specs=[pl.BlockSpec((T, dh), lambda h, j: (0, qo + h)),
                  pl.BlockSpec((TK, dh), lambda h, j: (j, ko + h)),
                  pl.BlockSpec((TK, dh), lambda h, j: (j, vo + h)),
                  pl.BlockSpec((None, NC, dh, CW), res),
                  pl.BlockSpec((None, None, dh, TK), lambda h, j: (h, j, 0, 0)),
                  pl.BlockSpec((T, dh), lambda h, j: (0, h)),
                  pl.BlockSpec((None, NC, dh, CW), res),
                  pl.BlockSpec((None, TK, 1), lambda h, j: (h, j, 0)),
                  pl.BlockSpec((None, NC, 1, CW), res),
                  pl.BlockSpec((None, NC, 1, CW), res)] + s_in,
        out_specs=[pl.BlockSpec((TK, dh), lambda h, j: (j, h)),
                   pl.BlockSpec((TK, dh), lambda h, j: (j, h)),
                   pl.BlockSpec((None, TK, 1), lambda h, j: (h, j, 0)),
                   pl.BlockSpec((None, NC, dh, CW), res),
                   pl.BlockSpec((None, NC, 1, CW), res)] + s_out,
        out_shape=[SDS((T, H * dh), bf16), SDS((T, H * dh), bf16), SDS((H, T, 1), f32),
                   SDS((H, NC, dh, CW), f32), SDS((H, NC, 1, CW), f32)] + s_shapes,
        scratch_shapes=[pltpu.VMEM((TK, dh), f32), pltpu.VMEM((TK, dh), f32),
                        pltpu.VMEM((TK, LANES), f32)] + s_sems,
        compiler_params=_cp("parallel", "arbitrary") if side is None else _cp("arbitrary", "arbitrary"),
    )(u, u, u, qT, kT, do, doT, ck2, lse2, delta, *(side[1] if side is not None else []))


def _ssd_common(sm_ref, dtb_ref, alog_ref):
    L = SSD_L
    dt = _softplus(sm_ref[...] + dtb_ref[...])
    a = -jnp.exp(alog_ref[...])
    cs = _hdot(_tri(L, True), dt * a)
    return dt, a, cs, cs.T, cs[L - 1:L, :]


def _ssd_fwd(xbc, small, dtb_row, alog_row, dskip_row, *, name, side=None):
    T = xbc.shape[0]
    L, P, N = SSD_L, SSM_P, SSM_N
    nc = T // L
    hpg = SSM_HEADS // SSM_G
    ns = len(side[1]) if side is not None else 0
    s_in, s_out, s_shapes, s_sems = _side_specs(side) if side is not None else ([], [], [], [])

    def body(x_ref, sm_ref, dtb_ref, alog_ref, dsk_ref, *rest):
        y_ref, hs_ref, h_s = rest[ns], rest[ns + 1], rest[2 * ns + 2]
        if side is not None:
            step = pl.program_id(0)
            _side_run(side, rest[:ns] + rest[ns + 2:2 * ns + 2] + rest[2 * ns + 3:], step == 0, step == nc - 1)

        @pl.when(pl.program_id(0) == 0)
        def _():
            h_s[...] = jnp.zeros_like(h_s)

        dt, a, cs, csT, tot = _ssd_common(sm_ref, dtb_ref, alog_ref)
        hs_ref[...] = h_s[...]
        causal = _causal(L)
        for g in range(SSM_G):
            bg = x_ref[:, SSM_W + g * N:SSM_W + (g + 1) * N]
            cg = x_ref[:, SSM_W + SSM_G * N + g * N:SSM_W + SSM_G * N + (g + 1) * N]
            cb = lax.dot_general(cg, bg, _NT, preferred_element_type=f32)
            hg = h_s[g * hpg * P:(g + 1) * hpg * P, :]
            yoff = lax.dot_general(cg, hg.astype(bf16), _NT, preferred_element_type=f32)
            for hh in range(hpg):
                h = g * hpg + hh
                ln = HL + h
                acol, arow = cs[:, ln:ln + 1], csT[ln:ln + 1, :]
                w = jnp.where(causal, cb * jnp.exp(jnp.minimum(acol - arow, 0.0)), 0.0)
                xs = x_ref[:, h * P:(h + 1) * P].astype(f32)
                xh = xs * dt[:, ln:ln + 1]
                y = jnp.dot(w.astype(bf16), xh.astype(bf16), preferred_element_type=f32)
                y = y + jnp.exp(acol) * yoff[:, hh * P:(hh + 1) * P] + xs * dsk_ref[:, ln:ln + 1]
                y_ref[:, h * P:(h + 1) * P] = y.astype(y_ref.dtype)
                decay = jnp.exp(tot[:, ln:ln + 1] - acol)
                st = lax.dot_general((xh * decay).astype(bf16), bg, _TN, preferred_element_type=f32)
                h_s[h * P:(h + 1) * P, :] = jnp.exp(tot[:, ln:ln + 1]) * h_s[h * P:(h + 1) * P, :] + st

    row = pl.BlockSpec((1, LANES), lambda c: (0, 0))
    return _pcall(
        body, name=name, grid=(nc,),
        in_specs=[pl.BlockSpec((L, XBC_W), lambda c: (c, 0)), pl.BlockSpec((L, LANES), lambda c: (c, 0)),
                  row, row, row] + s_in,
        out_specs=[pl.BlockSpec((L, SSM_W), lambda c: (c, 0)),
                   pl.BlockSpec((None, SSM_HEADS * P, N), lambda c: (c, 0, 0))] + s_out,
        out_shape=[SDS((T, SSM_W), bf16), SDS((nc, SSM_HEADS * P, N), f32)] + s_shapes,
        scratch_shapes=[pltpu.VMEM((SSM_HEADS * P, N), f32)] + s_sems,
        compiler_params=_cp("arbitrary"),
    )(xbc, small, dtb_row, alog_row, dskip_row, *(side[1] if side is not None else []))


def _ssd_bwd(dy, xbc, small, dtb_row, alog_row, dskip_row, hstates, *, name):
    T = xbc.shape[0]
    L, P, N = SSD_L, SSM_P, SSM_N
    nc = T // L
    hpg = SSM_HEADS // SSM_G
    GW = hpg * P

    def body(dy_ref, x_ref, sm_ref, dtb_ref, alog_ref, dsk_ref, hs_ref,
             dx_ref, dsm_ref, ddtb_ref, dalog_ref, ddsk_ref, dh_s, ea_s, dx_s):
        @pl.when(pl.program_id(0) == 0)
        def _():
            dh_s[...] = jnp.zeros_like(dh_s)
            ddtb_ref[...] = jnp.zeros_like(ddtb_ref)
            dalog_ref[...] = jnp.zeros_like(dalog_ref)
            ddsk_ref[...] = jnp.zeros_like(ddsk_ref)

        dt, a, cs, csT, tot = _ssd_common(sm_ref, dtb_ref, alog_ref)
        causal = _causal(L)
        lane = lax.broadcasted_iota(jnp.int32, (L, LANES), 1)
        sub = lax.broadcasted_iota(jnp.int32, (LANES, L), 0)
        lane1 = lax.broadcasted_iota(jnp.int32, (1, LANES), 1)
        da_col = jnp.zeros((L, LANES), f32)
        da_row = jnp.zeros((LANES, L), f32)
        da_tot = jnp.zeros((1, LANES), f32)
        ddt = jnp.zeros((L, LANES), f32)
        ddsk = jnp.zeros((1, LANES), f32)
        for g in range(SSM_G):
            bo, co = SSM_W + g * N, SSM_W + SSM_G * N + g * N
            bg, cg = x_ref[:, bo:bo + N], x_ref[:, co:co + N]
            cb = lax.dot_general(cg, bg, _NT, preferred_element_type=f32)
            hg = hs_ref[g * GW:(g + 1) * GW, :]
            hgb = hg.astype(bf16)
            dhn = dh_s[g * GW:(g + 1) * GW, :]
            dhnb = dhn.astype(bf16)
            yoff = lax.dot_general(cg, hgb, _NT, preferred_element_type=f32)
            bdh = lax.dot_general(bg, dhnb, _NT, preferred_element_type=f32)
            dcb = jnp.zeros((L, L), f32)
            for hh in range(hpg):
                h = g * hpg + hh
                ln = HL + h
                sl = slice(hh * P, (hh + 1) * P)
                acol, arow = cs[:, ln:ln + 1], csT[ln:ln + 1, :]
                e = jnp.where(causal, jnp.exp(jnp.minimum(acol - arow, 0.0)), 0.0)
                w = cb * e
                xs = x_ref[:, h * P:(h + 1) * P].astype(f32)
                dth = dt[:, ln:ln + 1]
                xh = xs * dth
                dyh = dy_ref[:, h * P:(h + 1) * P].astype(f32)
                dyb = dyh.astype(bf16)
                ea = jnp.exp(acol)
                decay = jnp.exp(tot[:, ln:ln + 1] - acol)
                etot = jnp.exp(tot[:, ln:ln + 1])
                dw = lax.dot_general(dyb, xh.astype(bf16), _NT, preferred_element_type=f32)
                m = dw * w
                dcb = dcb + dw * e
                dxs_state = decay * bdh[:, sl]
                dxh = lax.dot_general(w.astype(bf16), dyb, _TN, preferred_element_type=f32) + dxs_state
                r = jnp.sum(xh * dxs_state, axis=1, keepdims=True)
                col = (jnp.sum(m, axis=1, keepdims=True) - r
                       + jnp.sum(dyh * (ea * yoff[:, sl]), axis=1, keepdims=True))
                da_col = da_col + jnp.where(lane == ln, col, 0.0)
                da_row = da_row + jnp.where(sub == ln, jnp.sum(m, axis=0, keepdims=True), 0.0)
                hprod = jnp.sum(jnp.sum(dhn[sl, :] * hg[sl, :], axis=1, keepdims=True), axis=0, keepdims=True)
                t_h = jnp.sum(r, axis=0, keepdims=True) + etot * hprod
                da_tot = da_tot + jnp.where(lane1 == ln, t_h, 0.0)
                ea_s[:, sl] = (ea * dyh).astype(bf16)
                dx_s[:, sl] = (decay * xh).astype(bf16)
                dx_ref[:, h * P:(h + 1) * P] = (dxh * dth + dyh * dsk_ref[:, ln:ln + 1]).astype(dx_ref.dtype)
                ddt = ddt + jnp.where(lane == ln, jnp.sum(dxh * xs, axis=1, keepdims=True), 0.0)
                ddsk = ddsk + jnp.where(
                    lane1 == ln, jnp.sum(jnp.sum(dyh * xs, axis=1, keepdims=True), axis=0, keepdims=True), 0.0)
                dh_s[g * GW + hh * P:g * GW + (hh + 1) * P, :] = etot * dhn[sl, :]
            dcbb = dcb.astype(bf16)
            eab, dxb = ea_s[...], dx_s[...]
            dcg = (jnp.dot(dcbb, bg, preferred_element_type=f32)
                   + jnp.dot(eab, hgb, preferred_element_type=f32))
            dbg = (lax.dot_general(dcbb, cg, _TN, preferred_element_type=f32)
                   + jnp.dot(dxb, dhnb, preferred_element_type=f32))
            dx_ref[:, bo:bo + N] = dbg.astype(dx_ref.dtype)
            dx_ref[:, co:co + N] = dcg.astype(dx_ref.dtype)
            dh_s[g * GW:(g + 1) * GW, :] += lax.dot_general(eab, cg, _TN, preferred_element_type=f32)

        rowi = lax.broadcasted_iota(jnp.int32, (L, LANES), 0)
        da_cs = da_col - da_row.T + jnp.where(rowi == L - 1, da_tot, 0.0)
        dda = _hdot(_tri(L, False), da_cs)
        ddt = ddt + dda * a
        heads = (lane >= HL) & (lane < HL + SSM_HEADS)
        draw = jnp.where(heads, ddt * jax.nn.sigmoid(sm_ref[...] + dtb_ref[...]), 0.0)
        dsm_ref[...] = draw
        ddtb_ref[...] += jnp.sum(draw, axis=0, keepdims=True)
        dalog_ref[...] += jnp.where(lane1 >= HL, jnp.sum(dda * dt, axis=0, keepdims=True) * a, 0.0)
        ddsk_ref[...] += ddsk

    rev = lambda c: (nc - 1 - c, 0)
    row = pl.BlockSpec((1, LANES), lambda c: (0, 0))
    return _pcall(
        body, name=name, grid=(nc,),
        in_specs=[pl.BlockSpec((L, SSM_W), rev), pl.BlockSpec((L, XBC_W), rev), pl.BlockSpec((L, LANES), rev),
                  row, row, row, pl.BlockSpec((None, SSM_HEADS * P, N), lambda c: (nc - 1 - c, 0, 0))],
        out_specs=[pl.BlockSpec((L, XBC_W), rev), pl.BlockSpec((L, LANES), rev), row, row, row],
        out_shape=[SDS((T, XBC_W), bf16), SDS((T, LANES), f32)] + [SDS((1, LANES), f32)] * 3,
        scratch_shapes=[pltpu.VMEM((SSM_HEADS * P, N), f32), pltpu.VMEM((L, GW), bf16), pltpu.VMEM((L, GW), bf16)],
        compiler_params=_cp("arbitrary"),
    )(dy, xbc, small, dtb_row, alog_row, dskip_row, hstates)


def _loss_head(x, w, target, *, name, tt=256):
    T, D = x.shape
    tt = _pick(T, (tt, 128))

    def fn(x, w, tg):
        (y,) = _f_rms(x, w)
        return 0.5 * jnp.sum(jnp.sum(jnp.square(y - tg), axis=1, keepdims=True), axis=0, keepdims=True) / D

    def body(x_ref, w_ref, t_ref, loss_ref, dx_ref, dw_ref):
        i = pl.program_id(0)
        ls, vjp = jax.vjp(fn, x_ref[...], w_ref[...], t_ref[...])
        dx, dw, _ = vjp(jnp.ones((1, 1), f32))
        dx_ref[...] = dx

        @pl.when(i == 0)
        def _():
            loss_ref[...] = ls
            dw_ref[...] = dw

        @pl.when(i > 0)
        def _():
            loss_ref[...] += ls
            dw_ref[...] += dw

    blk = pl.BlockSpec((tt, D), lambda i: (i, 0))
    row = pl.BlockSpec((1, D), lambda i: (0, 0))
    return _pcall(
        body, name=name, grid=(T // tt,), in_specs=[blk, row, blk],
        out_specs=[pl.BlockSpec((1, 1), lambda i: (0, 0)), blk, row],
        out_shape=[SDS((1, 1), f32), SDS((T, D), f32), SDS((1, D), f32)],
        compiler_params=_cp("arbitrary"),
    )(x, w, target)


ADAM_BLOCK_ELEMS = 256 * 1024


def _row_tile(rows, cols, nbuf=1):
    for t in (2048, 1024, 512, 256, 128, 64, 32, 16, 8):
        if t * cols * max(1, nbuf // 2) <= ADAM_BLOCK_ELEMS and rows % t == 0:
            return t
    return rows


def _adamw(gss, w, m, v, *, name):
    ns, Rp, C = gss[0].shape
    npart = len(gss)
    tr = _row_tile(Rp, C, npart)
    per = Rp // tr
    c1 = 1.0 / (1.0 - ADAM_B1 ** ADAM_STEP)
    c2 = 1.0 / (1.0 - ADAM_B2 ** ADAM_STEP)

    def body(*refs):
        g_refs = refs[:npart]
        w_ref, m_ref, v_ref, go_ref, d_ref, mo_ref, vo_ref = refs[npart:]
        part = pl.program_id(0) // per

        def update(g_ref):
            g = g_ref[0].astype(f32)
            for s in range(1, ns):
                g = g + g_ref[s].astype(f32)
            mn = ADAM_B1 * m_ref[...] + (1.0 - ADAM_B1) * g
            vn = ADAM_B2 * v_ref[...] + (1.0 - ADAM_B2) * (g * g)
            go_ref[...] = g
            mo_ref[...] = mn
            vo_ref[...] = vn
            d_ref[...] = -ADAM_LR * ((mn * c1) / (jnp.sqrt(vn * c2) + ADAM_EPS) + ADAM_WD * w_ref[...])

        for p in range(npart):
            @pl.when(part == p)
            def _(p=p):
                update(g_refs[p])

    blk = pl.BlockSpec((tr, C), lambda i: (i, 0))
    g_specs = [pl.BlockSpec((ns, tr, C), functools.partial(
        lambda i, p: (0, jnp.clip(i - p * per, 0, per - 1), 0), p=p)) for p in range(npart)]
    return _pcall(
        body, name=name, grid=(npart * per,), in_specs=g_specs + [blk, blk, blk],
        out_specs=[blk] * 4, out_shape=[SDS((npart * Rp, C), f32)] * 4,
        compiler_params=_cp("parallel"),
    )(*gss, w, m, v)


def _sum_slabs(gs, *, name):
    ns, R, C = gs.shape
    tr = _row_tile(R, C)

    def body(g_ref, o_ref):
        g = g_ref[0]
        for s in range(1, ns):
            g = g + g_ref[s]
        o_ref[...] = g

    return _pcall(
        body, name=name, grid=(R // tr,), in_specs=[pl.BlockSpec((ns, tr, C), lambda i: (0, i, 0))],
        out_specs=pl.BlockSpec((tr, C), lambda i: (i, 0)), out_shape=SDS((R, C), f32),
        compiler_params=_cp("parallel"),
    )(gs)


def _allgather(arrs, *, name):
    n = len(arrs)

    def body(*refs):
        x_refs, out_refs = refs[:n], refs[n:2 * n]
        send_sems, recv_sems, local_sems = refs[2 * n:]
        x, y, c = lax.axis_index("x"), lax.axis_index("y"), lax.axis_index("c")
        me, sibling = (x, y, c), (x, y, 1 - c)
        chips = [(1 - x, y), (x, 1 - y), (1 - x, 1 - y)]

        def copy(a, k, block, to, src=None):
            px, py, pc = block
            slot = out_refs[a].at[4 * px + 2 * py + pc]
            return pltpu.make_async_remote_copy(
                src_ref=slot if src is None else src, dst_ref=slot,
                send_sem=send_sems.at[7 * a + k], recv_sem=recv_sems.at[7 * a + k],
                device_id=to, device_id_type=_MESH)

        mine, first, passed = [], [], []
        for a in range(n):
            cp = pltpu.make_async_copy(x_refs[a], out_refs[a].at[4 * x + 2 * y + c], local_sems.at[a])
            cp.start()
            mine.append(cp)
            first.append(copy(a, 0, me, sibling, src=x_refs[a]))
            first += [copy(a, 1 + j, me, (*chip, c), src=x_refs[a]) for j, chip in enumerate(chips)]
        for cp in first:
            cp.start()
        for j, chip in enumerate(chips):
            for a in range(n):
                copy(a, 1 + j, (*chip, c), me).wait_recv()
                cp = copy(a, 4 + j, (*chip, c), sibling)
                cp.start()
                passed.append(cp)
        for a in range(n):
            copy(a, 0, sibling, me).wait_recv()
            for j, chip in enumerate(chips):
                copy(a, 4 + j, (*chip, 1 - c), me).wait_recv()
        for cp in first + passed:
            cp.wait_send()
        for cp in mine:
            cp.wait()

    return _pcall(
        body, name=name, in_specs=[_HBM] * n, out_specs=[_HBM] * n,
        out_shape=[SDS((N_DEV,) + a.shape, a.dtype) for a in arrs],
        scratch_shapes=[pltpu.SemaphoreType.DMA((7 * n,)), pltpu.SemaphoreType.DMA((7 * n,)),
                        pltpu.SemaphoreType.DMA((n,))],
    )(*arrs)


def _exchange(arrs, *, name):
    n = len(arrs)

    def body(*refs):
        in_refs, out_refs = refs[:n], refs[n:2 * n]
        send_sems, recv_sems, local_sems = refs[2 * n:]
        x, y, c = lax.axis_index("x"), lax.axis_index("y"), lax.axis_index("c")
        me = 4 * x + 2 * y + c
        mine, sends, recvs = [], [], []
        for a in range(n):
            cp = pltpu.make_async_copy(in_refs[a].at[me], out_refs[a].at[me], local_sems.at[a])
            cp.start()
            mine.append(cp)
        for k in range(1, N_DEV):
            px, py, pc = x ^ ((k >> 2) & 1), y ^ ((k >> 1) & 1), c ^ (k & 1)
            peer = 4 * px + 2 * py + pc
            for a in range(n):
                def copy(src_slot, dst_slot):
                    return pltpu.make_async_remote_copy(
                        src_ref=in_refs[a].at[src_slot], dst_ref=out_refs[a].at[dst_slot],
                        send_sem=send_sems.at[7 * a + k - 1], recv_sem=recv_sems.at[7 * a + k - 1],
                        device_id=(px, py, pc), device_id_type=_MESH)

                sends.append(copy(peer, me))
                recvs.append(copy(me, peer))
                sends[-1].start()
        for cp in recvs:
            cp.wait_recv()
        for cp in sends:
            cp.wait_send()
        for cp in mine:
            cp.wait()

    return _pcall(
        body, name=name, in_specs=[_HBM] * n, out_specs=[_HBM] * n,
        out_shape=[SDS(a.shape, a.dtype) for a in arrs],
        scratch_shapes=[pltpu.SemaphoreType.DMA((7 * n,)), pltpu.SemaphoreType.DMA((7 * n,)),
                        pltpu.SemaphoreType.DMA((n,))],
    )(*arrs)


def _row128(vec, off):
    return jnp.pad(vec.astype(f32), (off, LANES - off - vec.shape[0]))[None, :]


def _cb(name, cw):
    return MAIN_OFF[name] // cw


def _layer_fwd(x, p, tag, shard, nxt):
    s = {"x": x}
    sh_gate, sh_branch, sh_out = shard
    (h,) = _rowwise(_f_rms, [(x, 0)], [p["norm_w"]], [bf16], cw=D_MODEL, ncol=1, name=f"rms_{tag}")
    u, g_gate, g_out = _mm(h, p["w_main"], mode="nn", out_dtype=bf16, name=f"in_main_{tag}",
                           side=("gather", [sh_gate, sh_out]))
    small = _mm(h, p["w_small"], mode="nn", out_dtype=f32, name=f"in_small_{tag}")
    s.update(h=h, u=u, small=small)
    W = BW
    SH = D_MODEL // N_DEV
    w_gate = jnp.moveaxis(g_gate.reshape(N_DEV, 4, SH, D_MODEL), 0, 1).reshape(4, D_MODEL, D_MODEL)
    p["w_gate"] = [w_gate[i] for i in range(4)]
    p["w_out"] = g_out.reshape(D_MODEL, D_MODEL)

    ck2 = _fox_gate_fwd(small, p["fb_row"], name=f"fox_gate_{tag}")
    qT = _to_t(u[:, MAIN_OFF["q"]:MAIN_OFF["q"] + W], CW)
    vT = _to_t(u[:, MAIN_OFF["v"]:MAIN_OFF["v"] + W], CW)
    oT, lse2, g_branch, *nxt_a = _flash_fwd(u, qT, vT, ck2, name=f"flash_{tag}",
                                            side=("gather", [sh_branch] + (nxt[:1] if nxt else [])))
    w_branch = jnp.moveaxis(g_branch.reshape(N_DEV, 4, BW, SH), 0, 2).reshape(4, BW, D_MODEL)
    p["w_branch"] = [w_branch[i] for i in range(4)]
    o = _from_t(oT)
    (ya,) = _rowwise(_f_ya, [(o, 0), (u, _cb("ga", W))], [], [bf16], cw=W, ncol=1, name=f"ya_{tag}")
    s.update(ck2=ck2, qT=qT, o=o, oT=oT, lse2=lse2)

    pre_b, xbc_c = _conv_fwd(u, _cb("xbc", 512), XBC_W, p["ssm_conv_w"], p["ssm_conv_b"], with_silu=True,
                             name=f"ssm_conv_{tag}")
    y_ssd, hst, *nxt_b = _ssd_fwd(xbc_c, small, p["dtb_row"], p["alog_row"], p["dskip_row"], name=f"ssd_{tag}",
                                  side=("gather", nxt[1:]) if nxt else None)
    (yb,) = _rowwise(_f_yb, [(y_ssd, 0), (u, _cb("z", W))], [p["ssm_norm_w"]], [bf16], cw=W, ncol=1,
                     name=f"yb_{tag}")
    s.update(pre_b=pre_b, xbc_c=xbc_c, y_ssd=y_ssd, hst=hst)

    (pc,) = _rowwise(_f_mul, [(u, _cb("scc", 512)), (u, _cb("scx", 512))], [], [bf16], cw=512, ncol=2,
                     name=f"sc_pre_{tag}")
    (cv,) = _conv_fwd(pc, 0, W, p["sc_conv_w"], p["sc_conv_b"], with_silu=False, name=f"sc_conv_{tag}")
    (yc,) = _rowwise(_f_yc, [(u, _cb("scb", 512)), (cv, 0), (u, _cb("gc", 512))], [], [bf16], cw=512, ncol=2,
                     name=f"yc_{tag}")
    s.update(pc=pc, cv=cv)

    (pd,) = _rowwise(_f_glu, [(u, _cb("glua", 512)), (u, _cb("glug", 512))], [], [bf16], cw=512, ncol=2,
                     name=f"cf_pre_{tag}")
    (cf,) = _conv_fwd(pd, 0, W, p["cf_conv_w"], p["cf_conv_b"], with_silu=False, name=f"cf_conv_{tag}")
    (yd,) = _rowwise(_f_yd, [(cf, 0), (u, _cb("gd", W))], [p["cf_ln_w"], p["cf_ln_b"]], [bf16], cw=W, ncol=1,
                     name=f"yd_{tag}")
    s.update(pd=pd, cf=cf)

    ys = (ya, yb, yc, yd)
    gs = [_mm(h, p["w_gate"][i], mode="nn", out_dtype=bf16, name=f"gate{i}_{tag}") for i in range(4)]
    ps = [_mm(ys[i], p["w_branch"][i], mode="nn", out_dtype=bf16, name=f"branch{i}_{tag}") for i in range(4)]
    (merged,) = _rowwise(_f_merge, [(a, 0) for a in gs + ps], list(p["b_gate"]), [bf16], cw=512,
                         ncol=D_MODEL // 512, name=f"merge_{tag}")
    out = _mm(merged, p["w_out"], mode="nn", out_dtype=f32, add=x, name=f"out_{tag}")
    s.update(ys=ys, gs=gs, ps=ps, merged=merged)
    return out, s, nxt_a + nxt_b


def _layer_bwd(dout, p, s, tag, prev_in):
    g = {}
    SH = D_MODEL // N_DEV
    W = BW
    u, h, small = s["u"], s["h"], s["small"]
    dob = dout.astype(bf16)
    dmerged = _mm(dob, p["w_out"], mode="nt", out_dtype=bf16, name=f"d_merged_{tag}")
    g["w_out"] = _mm(s["merged"], dob, mode="tn", out_dtype=bf16, name=f"dw_out_{tag}")
    r = _rowwise_bwd(_f_merge, [(a, 0) for a in s["gs"] + s["ps"]], list(p["b_gate"]), [dmerged], [bf16] * 8,
                     cw=512, ncol=D_MODEL // 512, name=f"d_merge_{tag}")
    dgs, dps, g["b_gate"] = r[0:4], r[4:8], r[8:12]
    g["w_gate"] = [_mm(h, dgs[i], mode="tn", out_dtype=bf16, name=f"dw_gate{i}_{tag}") for i in range(4)]
    g["w_branch"] = [_mm(s["ys"][i], dps[i], mode="tn", out_dtype=bf16, name=f"dw_branch{i}_{tag}")
                     for i in range(4)]
    dh = None
    for i in range(4):
        dh = _mm(dgs[i], p["w_gate"][i], mode="nt", out_dtype=f32, add=dh, tn=1024, name=f"dh_gate{i}_{tag}")
    dya, dyb, dyc, dyd = [_mm(dps[i], p["w_branch"][i], mode="nt", out_dtype=bf16, name=f"d_y{i}_{tag}")
                          for i in range(4)]
    du = {}

    do, du["ga"] = _rowwise_bwd(_f_ya, [(s["o"], 0), (u, _cb("ga", W))], [], [dya], [bf16, bf16], cw=W, ncol=1,
                                name=f"d_ya_{tag}")
    doT = _to_t(do, CW)
    delta = _flash_delta(doT, s["oT"], name=f"flash_delta_{tag}")
    kT = _to_t(u[:, MAIN_OFF["k"]:MAIN_OFF["k"] + W], min(FLASH_TK, u.shape[0]))
    own = [jnp.moveaxis(jnp.stack(g["w_gate"]).reshape(4, N_DEV, SH, D_MODEL), 1, 0).reshape(N_DEV, 4 * SH, D_MODEL),
           jnp.moveaxis(jnp.stack(g["w_branch"]).reshape(4, BW, N_DEV, SH), 2, 0).reshape(N_DEV, 4 * BW, SH),
           g["w_out"].reshape(N_DEV, SH, D_MODEL)]
    du["k"], du["v"], dck, dqT, dcq, *got_own = _flash_bwd(u, s["qT"], kT, do, doT, s["ck2"], s["lse2"], delta,
                                                           name=f"flash_bwd_{tag}", side=("exchange", own))
    du["q"] = _from_t(dqT).astype(bf16)
    dc = jnp.pad((dcq.reshape(FOX_HEADS, -1) + dck[:, :, 0]).T, ((0, 0), (0, LANES - FOX_HEADS)))

    dy_ssd, du["z"], g["ssm_norm_w"] = _rowwise_bwd(
        _f_yb, [(s["y_ssd"], 0), (u, _cb("z", W))], [p["ssm_norm_w"]], [dyb], [bf16, bf16], cw=W, ncol=1,
        name=f"d_yb_{tag}")
    dxbc_c, dsmall_dt, g["dtb_row"], g["alog_row"], g["dskip_row"] = _ssd_bwd(
        dy_ssd, s["xbc_c"], small, p["dtb_row"], p["alog_row"], p["dskip_row"], s["hst"], name=f"d_ssd_{tag}")
    (dpre_b,) = _rowwise_bwd(_f_silu, [(s["pre_b"], 0)], [], [dxbc_c], [bf16], cw=512, ncol=3,
                             name=f"d_ssm_silu_{tag}")
    du["xbc"], g["ssm_conv_w"], g["ssm_conv_b"] = _conv_bwd(dpre_b, u, _cb("xbc", 256), XBC_W, p["ssm_conv_w"],
                                                            name=f"d_ssm_conv_{tag}")

    du["scb"], dcv, du["gc"] = _rowwise_bwd(
        _f_yc, [(u, _cb("scb", 512)), (s["cv"], 0), (u, _cb("gc", 512))], [], [dyc], [bf16] * 3, cw=512, ncol=2,
        name=f"d_yc_{tag}")
    dpc, g["sc_conv_w"], g["sc_conv_b"] = _conv_bwd(dcv, s["pc"], 0, W, p["sc_conv_w"], name=f"d_sc_conv_{tag}")
    du["scc"], du["scx"] = _rowwise_bwd(_f_mul, [(u, _cb("scc", 512)), (u, _cb("scx", 512))], [], [dpc],
                                        [bf16, bf16], cw=512, ncol=2, name=f"d_sc_pre_{tag}")

    dcf, du["gd"], g["cf_ln_w"], g["cf_ln_b"] = _rowwise_bwd(
        _f_yd, [(s["cf"], 0), (u, _cb("gd", W))], [p["cf_ln_w"], p["cf_ln_b"]], [dyd], [bf16, bf16], cw=W, ncol=1,
        name=f"d_yd_{tag}")
    dpd, g["cf_conv_w"], g["cf_conv_b"] = _conv_bwd(dcf, s["pd"], 0, W, p["cf_conv_w"], name=f"d_cf_conv_{tag}")
    du["glua"], du["glug"] = _rowwise_bwd(_f_glu, [(u, _cb("glua", 512)), (u, _cb("glug", 512))], [], [dpd],
                                          [bf16, bf16], cw=512, ncol=2, name=f"d_cf_pre_{tag}")

    dsmall, g["fb_row"] = _fox_gate_bwd(dc, small, p["fb_row"], dsmall_dt, name=f"d_fox_gate_{tag}")
    du_main = jnp.concatenate([du[n] for n in MAIN_NAMES], axis=1)
    dsb = dsmall.astype(bf16)
    got_prev = []
    sides = (("exchange", prev_in[:1]), ("exchange", prev_in[1:])) if prev_in else (None, None)
    g["w_main"] = _mm(h, du_main, mode="tn", out_dtype=bf16, name=f"dw_main_{tag}", side=sides[0])
    if prev_in:
        g["w_main"], got_prev = g["w_main"][0], [g["w_main"][1]]
    g["w_small"] = _mm(h, dsb, mode="tn", out_dtype=bf16, name=f"dw_small_{tag}")
    dh = _mm(du_main, p["w_main"], mode="nt", out_dtype=f32, add=dh, tn=1024, name=f"dh_main_{tag}", side=sides[1])
    if prev_in:
        dh, got_prev = dh[0], got_prev + [dh[1]]
    dh = _mm(dsb, p["w_small"], mode="nt", out_dtype=f32, add=dh, tn=1024, name=f"dh_small_{tag}")
    dx, g["norm_w"] = _rowwise_bwd(_f_rms_res, [(s["x"], 0)], [p["norm_w"]], [dh, dout], [f32], cw=D_MODEL,
                                   ncol=1, name=f"d_rms_{tag}")
    return dx, g, got_own, got_prev


_BIG = ("w_in", "w_gate", "w_branch", "w_out")
_REPL = ("norm_w", "fg_bias", "ssm_conv_b", "dt_bias", "a_log", "d_skip", "ssm_norm_w", "sc_conv_b", "cf_conv_b",
         "cf_ln_w", "cf_ln_b", "final_norm_w")
_SHSM = ("ssm_conv_w", "sc_conv_w", "cf_conv_w", "b_gate")


def _to_rows(a, dtype):
    flat = a.astype(dtype).reshape(-1)
    n = flat.shape[0]
    per = 8 * LANES
    flat = jnp.pad(flat, (0, (-n) % per))
    return flat.reshape(-1, LANES)


def _pack(arrs, dtype):
    rows = [_to_rows(a, dtype) for a in arrs]
    n = sum(r.shape[0] for r in rows)
    rows.append(jnp.zeros(((-n) % 256, LANES), dtype))
    return jnp.concatenate(rows, axis=0)


def _unpack(blob, shapes):
    lead = blob.shape[:-2]
    out, r0 = [], 0
    for shp in shapes:
        n = 1
        for d in shp:
            n *= d
        nr = (-(-n // (8 * LANES))) * 8
        piece = blob[..., r0:r0 + nr, :].reshape(lead + (nr * LANES,))[..., :n]
        out.append(piece.reshape(lead + tuple(shp)))
        r0 += nr
    return out


def _main_cols(w):
    off, pieces = 0, {}
    for n, sz in zip(IN_NAMES, IN_SIZES):
        pieces[n] = w[..., off:off + sz]
        off += sz
    main = jnp.concatenate([pieces[n] for n in MAIN_NAMES], axis=-1)
    pad = jnp.zeros(w.shape[:-1] + (LANES - 24,), w.dtype)
    small = jnp.concatenate([pieces["f"], pieces["dt"], pad], axis=-1)
    return main, small


def _orig_cols(main, small):
    pieces = {}
    for n in MAIN_NAMES:
        sz = XBC_W if n == "xbc" else 1024
        pieces[n] = main[..., MAIN_OFF[n]:MAIN_OFF[n] + sz]
    pieces["f"] = small[..., 0:8]
    pieces["dt"] = small[..., 8:24]
    return jnp.concatenate([pieces[n] for n in IN_NAMES], axis=-1)


def kernel(*args):
    names = ["x", "norm_w", "w_in", "fg_bias", "ssm_conv_w", "ssm_conv_b", "dt_bias", "a_log", "d_skip",
             "ssm_norm_w", "sc_conv_w", "sc_conv_b", "cf_conv_w", "cf_conv_b", "cf_ln_w", "cf_ln_b", "w_gate",
             "b_gate", "w_branch", "w_out", "final_norm_w"]
    wnames = names[1:]
    a = dict(zip(names, args[:21]))
    target = args[21]
    mom = dict(zip(["m_" + n for n in wnames], args[22:42]))
    mom.update(zip(["v_" + n for n in wnames], args[42:62]))
    NL = a["w_in"].shape[0]
    x0 = a["x"][0]
    tgt = target[0]
    me = 4 * lax.axis_index("x") + 2 * lax.axis_index("y") + lax.axis_index("c")

    def rows2d(t):
        return t.reshape(-1, t.shape[-1])

    def shards(l):
        return [rows2d(a[n][l]).astype(bf16) for n in _BIG]

    (sm_all,) = _allgather([_pack([a[n] for n in _SHSM], f32)], name="gather_small")
    ssm_cw_g, sc_cw_g, cf_cw_g, b_gate_g = _unpack(sm_all, [a[n].shape for n in _SHSM])

    def cat_last(gathered):
        return jnp.moveaxis(gathered, 0, -2).reshape(gathered.shape[1:-1] + (-1,))

    ssm_cw, sc_cw, cf_cw, b_gate_full = cat_last(ssm_cw_g), cat_last(sc_cw_g), cat_last(cf_cw_g), cat_last(b_gate_g)

    HALF = D_MODEL // 2

    def layer_params(l, g_in):
        w_main, w_small = _main_cols(jnp.concatenate([cat_last(t) for t in g_in], axis=0))
        return dict(
            norm_w=a["norm_w"][l][None], w_main=w_main, w_small=w_small,
            fb_row=_row128(a["fg_bias"][l], 0), dtb_row=_row128(a["dt_bias"][l], HL),
            alog_row=_row128(a["a_log"][l], HL), dskip_row=_row128(a["d_skip"][l], HL),
            ssm_conv_w=ssm_cw[l], ssm_conv_b=a["ssm_conv_b"][l][None], ssm_norm_w=a["ssm_norm_w"][l][None],
            sc_conv_w=sc_cw[l], sc_conv_b=a["sc_conv_b"][l][None],
            cf_conv_w=cf_cw[l], cf_conv_b=a["cf_conv_b"][l][None],
            cf_ln_w=a["cf_ln_w"][l][None], cf_ln_b=a["cf_ln_b"][l][None],
            b_gate=[b_gate_full[l, i][None] for i in range(4)])

    def in_halves(t):
        return [t[..., :HALF, :], t[..., HALF:, :]]

    def in_slabs(g):
        d_in = _orig_cols(g["w_main"], g["w_small"])
        return in_halves(jnp.moveaxis(d_in.reshape(D_MODEL, N_DEV, -1), 1, 0))

    g_in = _allgather(in_halves(shards(0)[0]), name="gather_in_l0")
    xl, saved, layers = x0, [], []
    for l in range(NL):
        layers.append(layer_params(l, g_in))
        nxt = in_halves(shards(l + 1)[0]) if l + 1 < NL else None
        xl, s, g_in = _layer_fwd(xl, layers[l], f"l{l}", shards(l)[1:], nxt)
        saved.append(s)
    loss_part, dx, d_fnw = _loss_head(xl, a["final_norm_w"][None], tgt, name="loss_head")

    grads, got_own, got_in = [None] * NL, [None] * NL, [None] * NL
    for l in reversed(range(NL)):
        prev = in_slabs(grads[l + 1]) if l + 1 < NL else None
        dx, grads[l], got_own[l], res = _layer_bwd(dx, layers[l], saved[l], f"l{l}", prev)
        if prev is not None:
            got_in[l + 1] = res
    got_in[0] = _exchange(in_slabs(grads[0]), name="exchange_in_l0")

    def stack(key):
        return jnp.stack([grads[l][key] for l in range(NL)])

    big_res = [[], [], [], []]
    for j, n in enumerate(_BIG):
        parts = ([h for l in range(NL) for h in got_in[l]] if j == 0 else [got_own[l][j - 1] for l in range(NL)])
        outs = _adamw(parts, rows2d(a[n]), rows2d(mom["m_" + n]), rows2d(mom["v_" + n]), name=f"adamw_{n}")
        for k in range(4):
            big_res[k].append(outs[k].reshape(a[n].shape))

    def vec(row, off, n):
        return row[0, off:off + n]

    small_grads = {
        "norm_w": jnp.concatenate([grads[l]["norm_w"] for l in range(NL)]),
        "fg_bias": jnp.stack([vec(grads[l]["fb_row"], 0, FOX_HEADS) for l in range(NL)]),
        "ssm_conv_b": jnp.concatenate([grads[l]["ssm_conv_b"] for l in range(NL)]),
        "dt_bias": jnp.stack([vec(grads[l]["dtb_row"], HL, SSM_HEADS) for l in range(NL)]),
        "a_log": jnp.stack([vec(grads[l]["alog_row"], HL, SSM_HEADS) for l in range(NL)]),
        "d_skip": jnp.stack([vec(grads[l]["dskip_row"], HL, SSM_HEADS) for l in range(NL)]),
        "ssm_norm_w": jnp.concatenate([grads[l]["ssm_norm_w"] for l in range(NL)]),
        "sc_conv_b": jnp.concatenate([grads[l]["sc_conv_b"] for l in range(NL)]),
        "cf_conv_b": jnp.concatenate([grads[l]["cf_conv_b"] for l in range(NL)]),
        "cf_ln_w": jnp.concatenate([grads[l]["cf_ln_w"] for l in range(NL)]),
        "cf_ln_b": jnp.concatenate([grads[l]["cf_ln_b"] for l in range(NL)]),
        "final_norm_w": d_fnw[0],
        "ssm_conv_w": stack("ssm_conv_w"), "sc_conv_w": stack("sc_conv_w"), "cf_conv_w": stack("cf_conv_w"),
        "b_gate": jnp.stack([jnp.concatenate(grads[l]["b_gate"]) for l in range(NL)]),
    }
    order = _REPL + _SHSM
    full_shapes = [small_grads[n].shape for n in order] + [(1,)]
    (sg_all,) = _allgather([_pack([small_grads[n] for n in order] + [loss_part.reshape(1)], f32)],
                           name="gather_small_grads")

    def mine(n, full):
        if n in _REPL:
            return full
        sz = a[n].shape[-1]
        return lax.dynamic_slice_in_dim(full, me * sz, sz, axis=full.ndim - 1)

    g_full = _unpack(_sum_slabs(sg_all, name="sum_small"), full_shapes)
    loss = g_full[-1][0]
    g_mine = {n: mine(n, gf).reshape(a[n].shape) for n, gf in zip(order, g_full[:-1])}
    gs1 = _pack([g_mine[n] for n in order], f32)
    ws1 = _pack([a[n] for n in order], f32)
    ms1 = _pack([mom["m_" + n] for n in order], f32)
    vs1 = _pack([mom["v_" + n] for n in order], f32)
    sm_out = _adamw([gs1[None]], ws1, ms1, vs1, name="adamw_small")
    sm_res = [_unpack(o, [a[n].shape for n in order]) for o in sm_out]

    res = []
    for k in range(4):
        d = dict(zip(_BIG, big_res[k]))
        d.update(zip(order, sm_res[k]))
        res.append(d)
    outs = [loss, dx[None]]
    for k in range(4):
        outs += [res[k][n] for n in wnames]
    return tuple(outs)
```

```python
import functools

import jax
import jax.numpy as jnp
from jax import lax
from jax.experimental import pallas as pl
from jax.experimental.pallas import tpu as pltpu

f32 = jnp.float32
bf16 = jnp.bfloat16
SDS = jax.ShapeDtypeStruct

N_DEV = 8
LANES = 128
VMEM_LIMIT = 48 * 1024 * 1024
EPS = 1e-6
NEG = -1e30

D_MODEL = 2048
FOX_HEADS = 8
FOX_DH = 128
SSM_HEADS = 16
SSM_P = 64
SSM_N = 128
SSM_G = 2
SSM_W = 1024
XBC_W = 1536
SSM_K, SC_K, CF_K = 4, 3, 31
BW = 1024
SSD_L = 512
HL = 8
CONV_HALO = 32

ADAM_LR, ADAM_B1, ADAM_B2, ADAM_EPS, ADAM_WD, ADAM_STEP = 0.001, 0.9, 0.999, 1e-08, 0.01, 10

IN_NAMES = ("q", "k", "v", "f", "ga", "z", "xbc", "dt", "scb", "scc", "scx", "gc", "glua", "glug", "gd")
IN_SIZES = (1024, 1024, 1024, 8, 1024, 1024, 1536, 16, 1024, 1024, 1024, 1024, 1024, 1024, 1024)
N_IN = sum(IN_SIZES)
MAIN_NAMES = ("q", "k", "v", "ga", "z", "scb", "scc", "scx", "gc", "glua", "glug", "gd", "xbc")
MAIN_OFF = {n: 1024 * i for i, n in enumerate(MAIN_NAMES)}
N_MAIN = 12 * 1024 + XBC_W


def _pcall(body, **kw):
    return pl.pallas_call(body, **kw)


def _cp(*sem):
    return pltpu.CompilerParams(dimension_semantics=sem if sem else None, vmem_limit_bytes=VMEM_LIMIT)


def _pick(n, cands):
    for c in cands:
        if c <= n and n % c == 0:
            return c
    return n


_DIMS = {"nn": ((1,), (0,)), "nt": ((1,), (1,)), "tn": ((0,), (0,))}
_MESH = pl.DeviceIdType.MESH
_HBM = pl.BlockSpec(memory_space=pltpu.HBM)


def _side_specs(side):
    kind, arrs = side
    n = len(arrs)
    shapes = [SDS(((N_DEV,) + a.shape) if kind == "gather" else a.shape, a.dtype) for a in arrs]
    sems = [pltpu.SemaphoreType.DMA((7 * n,)), pltpu.SemaphoreType.DMA((7 * n,)), pltpu.SemaphoreType.DMA((n,))]
    return [_HBM] * n, [_HBM] * n, shapes, sems


def _side_copies(kind, in_refs, out_refs, send_sems, recv_sems, local_sems):
    x, y, c = lax.axis_index("x"), lax.axis_index("y"), lax.axis_index("c")
    me = 4 * x + 2 * y + c
    sends, recvs = [], []
    for a in range(len(in_refs)):
        src_mine = in_refs[a] if kind == "gather" else in_refs[a].at[me]
        sends.append(pltpu.make_async_copy(src_mine, out_refs[a].at[me], local_sems.at[a]))
    for k in range(1, N_DEV):
        px, py, pc = x ^ ((k >> 2) & 1), y ^ ((k >> 1) & 1), c ^ (k & 1)
        peer = 4 * px + 2 * py + pc
        for a in range(len(in_refs)):
            src = in_refs[a] if kind == "gather" else in_refs[a].at[peer]

            def copy(src_ref, dst_slot):
                return pltpu.make_async_remote_copy(
                    src_ref=src_ref, dst_ref=out_refs[a].at[dst_slot],
                    send_sem=send_sems.at[7 * a + k - 1], recv_sem=recv_sems.at[7 * a + k - 1],
                    device_id=(px, py, pc), device_id_type=_MESH)

            sends.append(copy(src, me))
            recvs.append(copy(src, peer))
    return sends, recvs


def _side_run(side, refs, first, last):
    n = len(side[1])
    sends, recvs = _side_copies(side[0], refs[:n], refs[n:2 * n], *refs[2 * n:])

    @pl.when(first)
    def _():
        for cp in sends:
            cp.start()

    @pl.when(last)
    def _():
        for cp in recvs:
            cp.wait_recv()
        for cp in sends[n:]:
            cp.wait_send()
        for cp in sends[:n]:
            cp.wait()


def _mm(a, b, *, mode, out_dtype, name, add=None, tm=512, tn=None, tk=None, side=None):
    if mode == "nn":
        (M, K), (_, N) = a.shape, b.shape
    elif mode == "nt":
        (M, K), (N, _) = a.shape, b.shape
    else:
        (K, M), (_, N) = a.shape, b.shape
    tm = _pick(M, (tm, 256, 128))
    tn = _pick(N, (tn,) if tn else (1536, 1024, 512, 256, 128))
    tk = _pick(K, (tk,) if tk else (2048, 1536, 1024, 512, 256, 128))
    nk = K // tk
    a_spec = (pl.BlockSpec((tk, tm), lambda j, i, k: (k, i)) if mode == "tn"
              else pl.BlockSpec((tm, tk), lambda j, i, k: (i, k)))
    b_spec = (pl.BlockSpec((tn, tk), lambda j, i, k: (j, k)) if mode == "nt"
              else pl.BlockSpec((tk, tn), lambda j, i, k: (k, j)))
    o_spec = pl.BlockSpec((tm, tn), lambda j, i, k: (i, j))
    ins, specs = [a, b], [a_spec, b_spec]
    if add is not None:
        ins.append(add)
        specs.append(o_spec)
    n_in = len(ins)
    dims = (_DIMS[mode], ((), ()))
    out_specs, out_shape = [o_spec], [SDS((M, N), out_dtype)]
    scratch = [pltpu.VMEM((tm, tn), f32)] if nk > 1 else []
    ns = 0
    if side is not None:
        ns = len(side[1])
        s_in, s_out, s_shapes, s_sems = _side_specs(side)
        ins, specs = ins + list(side[1]), specs + s_in
        out_specs, out_shape, scratch = out_specs + s_out, out_shape + s_shapes, scratch + s_sems
    grid = (N // tn, M // tm, nk)

    def body(*refs):
        a_ref, b_ref = refs[0], refs[1]
        add_ref = refs[2] if add is not None else None
        o_ref = refs[n_in + ns]
        rest = refs[n_in + 2 * ns + 1:]
        jj, ii, k = pl.program_id(0), pl.program_id(1), pl.program_id(2)
        if side is not None:
            side_refs = refs[n_in:n_in + ns] + refs[n_in + ns + 1:n_in + 2 * ns + 1] + rest[-3:]
            _side_run(side, side_refs, (jj == 0) & (ii == 0) & (k == 0),
                      (jj == grid[0] - 1) & (ii == grid[1] - 1) & (k == nk - 1))
        p = lax.dot_general(a_ref[...], b_ref[...], dims, preferred_element_type=f32)

        def finish(v):
            if add_ref is not None:
                v = v + add_ref[...].astype(f32)
            o_ref[...] = v.astype(o_ref.dtype)

        if nk == 1:
            finish(p)
        else:
            acc = rest[0]

            @pl.when(k == 0)
            def _():
                acc[...] = p

            @pl.when(k > 0)
            def _():
                acc[...] += p

            @pl.when(k == nk - 1)
            def _():
                finish(acc[...])

    res = _pcall(
        body, name=name, grid=grid, in_specs=specs, out_specs=out_specs, out_shape=out_shape,
        scratch_shapes=scratch,
        compiler_params=_cp("parallel", "parallel", "arbitrary") if side is None else _cp(*(("arbitrary",) * 3)),
    )(*ins)
    return res[0] if side is None else res


def _silu(x):
    return x * jax.nn.sigmoid(x)


def _softplus(x):
    return jnp.maximum(x, 0.0) + jnp.log(1.0 + jnp.exp(-jnp.abs(x)))


def _mean(x):
    return jnp.mean(x, axis=-1, keepdims=True)


def _f_rms(x, w):
    return (x * lax.rsqrt(_mean(x * x) + EPS) * w,)


def _f_rms_res(x, w):
    return _f_rms(x, w) + (x,)


def _f_ya(o, g):
    return (o * _silu(g),)


def _f_yb(y, z, w):
    t = y * _silu(z)
    return (t * lax.rsqrt(_mean(t * t) + EPS) * w,)


def _f_mul(a, b):
    return (a * b,)


def _f_yc(b, cv, g):
    return (b * cv * _silu(g),)


def _f_glu(a, g):
    return (a * jax.nn.sigmoid(g),)


def _f_yd(cf, g, lw, lb):
    mu = _mean(cf)
    var = _mean(jnp.square(cf - mu))
    ln = (cf - mu) * lax.rsqrt(var + EPS) * lw + lb
    return (_silu(ln) * _silu(g),)


def _f_silu(x):
    return (_silu(x),)


def _f_merge(g0, g1, g2, g3, p0, p1, p2, p3, b0, b1, b2, b3):
    sg = jax.nn.sigmoid
    return (sg(g0 + b0) * p0 + sg(g1 + b1) * p1 + sg(g2 + b2) * p2 + sg(g3 + b3) * p3,)


def _row_specs(ins, params, tt, cw):
    specs = [pl.BlockSpec((tt, cw), functools.partial(lambda j, i, base: (i, base + j), base=base))
             for _, base in ins]
    specs += [pl.BlockSpec((p.shape[0], cw), lambda j, i: (0, j)) for p in params]
    return specs


def _rowwise(fn, ins, params, out_dtypes, *, cw, ncol, name, tt=256):
    T = ins[0][0].shape[0]
    tt = _pick(T, (tt, 128))
    ni, npar = len(ins), len(params)

    def body(*refs):
        xs = [r[...].astype(f32) for r in refs[:ni]]
        ps = [r[...] for r in refs[ni:ni + npar]]
        ys = fn(*xs, *ps)
        for o_ref, y in zip(refs[ni + npar:], ys):
            o_ref[...] = y.astype(o_ref.dtype)

    out_spec = pl.BlockSpec((tt, cw), lambda j, i: (i, j))
    outs = _pcall(
        body, name=name, grid=(ncol, T // tt), in_specs=_row_specs(ins, params, tt, cw),
        out_specs=[out_spec] * len(out_dtypes),
        out_shape=[SDS((T, cw * ncol), dt) for dt in out_dtypes],
        compiler_params=_cp("parallel", "parallel"),
    )(*[a for a, _ in ins], *params)
    return outs


def _rowwise_bwd(fn, ins, params, cots, din_dtypes, *, cw, ncol, name, tt=256):
    T = ins[0][0].shape[0]
    tt = _pick(T, (tt, 128))
    ni, npar, nc = len(ins), len(params), len(cots)
    keep = [k for k, dt in enumerate(din_dtypes) if dt is not None]

    def body(*refs):
        xs = [r[...].astype(f32) for r in refs[:ni]]
        ps = [r[...] for r in refs[ni:ni + npar]]
        cs = tuple(r[...].astype(f32) for r in refs[ni + npar:ni + npar + nc])
        outs = refs[ni + npar + nc:]
        _, vjp = jax.vjp(fn, *xs, *ps)
        g = vjp(cs)
        for o_ref, k in zip(outs[:len(keep)], keep):
            o_ref[...] = g[k].astype(o_ref.dtype)
        i = pl.program_id(1)
        for o_ref, gp in zip(outs[len(keep):], g[ni:]):
            @pl.when(i == 0)
            def _(o_ref=o_ref, gp=gp):
                o_ref[...] = gp

            @pl.when(i > 0)
            def _(o_ref=o_ref, gp=gp):
                o_ref[...] += gp

    row_spec = pl.BlockSpec((tt, cw), lambda j, i: (i, j))
    in_specs = _row_specs(ins, params, tt, cw) + [row_spec] * nc
    out_specs = [row_spec] * len(keep) + [pl.BlockSpec((p.shape[0], cw), lambda j, i: (0, j)) for p in params]
    out_shape = [SDS((T, cw * ncol), din_dtypes[k]) for k in keep] + [SDS(p.shape, f32) for p in params]
    return _pcall(
        body, name=name, grid=(ncol, T // tt), in_specs=in_specs, out_specs=out_specs, out_shape=out_shape,
        compiler_params=_cp("arbitrary", "arbitrary"),
    )(*[a for a, _ in ins], *params, *cots)


CONV_RB = 64
SUBLANES = 8


def _shifted_rows(buf, n):
    for sft in range(1, SUBLANES):
        buf[sft, 0:n, :] = buf[0, pl.ds(sft, n), :]


def _tap(buf, o, rb, shifted):
    if shifted:
        return buf[o % SUBLANES, o - o % SUBLANES:o - o % SUBLANES + rb, :]
    return buf[0, pl.ds(o, rb), :]


def _conv_fwd(x, xbase, C, w, b, *, with_silu, name, out_dtype=bf16, tt=512, cw=512):
    xa = x
    T = xa.shape[0]
    K = w.shape[0]
    tt = _pick(T, (tt,))
    hp = CONV_HALO
    r = tt // hp
    ncol = C // cw
    rb = min(CONV_RB, tt)
    n = hp + tt
    shifted = K > SUBLANES

    def body(cur_ref, halo_ref, w_ref, b_ref, *rest):
        outs, ext = rest[:-1], rest[-1]
        i = pl.program_id(1)
        ext[0, 0:hp, :] = jnp.where(i > 0, halo_ref[...].astype(f32), 0.0)
        ext[0, hp:n, :] = cur_ref[...].astype(f32)
        ext[0, n:, :] = jnp.zeros((SUBLANES, cw), f32)
        if shifted:
            _shifted_rows(ext, n)
        for r0 in range(0, tt, rb):
            acc = jnp.zeros((rb, cw), f32) + b_ref[...]
            for k in range(K):
                acc = acc + w_ref[k:k + 1, :] * _tap(ext, hp - (K - 1) + k + r0, rb, shifted)
            outs[0][r0:r0 + rb, :] = acc.astype(outs[0].dtype)
            if with_silu:
                outs[1][r0:r0 + rb, :] = _silu(acc).astype(outs[1].dtype)

    n_out = 2 if with_silu else 1
    o_spec = pl.BlockSpec((tt, cw), lambda j, i: (i, j))
    return _pcall(
        body, name=name, grid=(ncol, T // tt),
        in_specs=[pl.BlockSpec((tt, cw), lambda j, i: (i, xbase + j)),
                  pl.BlockSpec((hp, cw), lambda j, i: (jnp.maximum(i * r - 1, 0), xbase + j)),
                  pl.BlockSpec((K, cw), lambda j, i: (0, j)),
                  pl.BlockSpec((1, cw), lambda j, i: (0, j))],
        out_specs=[o_spec] * n_out, out_shape=[SDS((T, C), out_dtype)] * n_out,
        scratch_shapes=[pltpu.VMEM((SUBLANES if shifted else 1, n + SUBLANES, cw), f32)],
        compiler_params=_cp("parallel", "parallel"),
    )(xa, xa, w, b)


def _conv_bwd(dy, x, xbase, C, w, *, name, dx_dtype=bf16, tt=512, cw=256):
    T = dy.shape[0]
    K = w.shape[0]
    tt = _pick(T, (tt,))
    hp = CONV_HALO
    r = tt // hp
    nt = T // tt
    ncol = C // cw
    rb = min(CONV_RB, tt)
    n = tt + hp
    shifted = K > SUBLANES

    def fold8(v):
        out = v[0:SUBLANES]
        for g in range(1, rb // SUBLANES):
            out = out + v[SUBLANES * g:SUBLANES * (g + 1)]
        return out

    def body(dy_ref, dyn_ref, x_ref, xh_ref, w_ref, dx_ref, dw_ref, db_ref, exd, exx, dwp):
        i = pl.program_id(1)
        exd[0, 0:tt, :] = dy_ref[...].astype(f32)
        exd[0, tt:n, :] = jnp.where(i < nt - 1, dyn_ref[...].astype(f32), 0.0)
        exd[0, n:, :] = jnp.zeros((SUBLANES, cw), f32)
        exx[0, 0:hp, :] = jnp.where(i > 0, xh_ref[...].astype(f32), 0.0)
        exx[0, hp:n, :] = x_ref[...].astype(f32)
        exx[0, n:, :] = jnp.zeros((SUBLANES, cw), f32)
        if shifted:
            _shifted_rows(exd, n)
            _shifted_rows(exx, n)

        @pl.when(i == 0)
        def _():
            dwp[...] = jnp.zeros_like(dwp)

        for r0 in range(0, tt, rb):
            dyc = exd[0, r0:r0 + rb, :]
            acc = jnp.zeros((rb, cw), f32)
            for k in range(K):
                acc = acc + w_ref[k:k + 1, :] * _tap(exd, K - 1 - k + r0, rb, shifted)
                dwp[SUBLANES * k:SUBLANES * (k + 1), :] += fold8(dyc * _tap(exx, hp - (K - 1) + k + r0, rb, shifted))
            dx_ref[r0:r0 + rb, :] = acc.astype(dx_ref.dtype)
            dwp[SUBLANES * K:SUBLANES * (K + 1), :] += fold8(dyc)

        @pl.when(i == nt - 1)
        def _():
            for k in range(K):
                dw_ref[k:k + 1, :] = jnp.sum(dwp[SUBLANES * k:SUBLANES * (k + 1), :], axis=0, keepdims=True)
            db_ref[...] = jnp.sum(dwp[SUBLANES * K:SUBLANES * (K + 1), :], axis=0, keepdims=True)

    nsh = SUBLANES if shifted else 1
    return _pcall(
        body, name=name, grid=(ncol, nt),
        in_specs=[pl.BlockSpec((tt, cw), lambda j, i: (i, j)),
                  pl.BlockSpec((hp, cw), lambda j, i: (jnp.minimum((i + 1) * r, nt * r - 1), j)),
                  pl.BlockSpec((tt, cw), lambda j, i: (i, xbase + j)),
                  pl.BlockSpec((hp, cw), lambda j, i: (jnp.maximum(i * r - 1, 0), xbase + j)),
                  pl.BlockSpec((K, cw), lambda j, i: (0, j))],
        out_specs=[pl.BlockSpec((tt, cw), lambda j, i: (i, j)),
                   pl.BlockSpec((K, cw), lambda j, i: (0, j)),
                   pl.BlockSpec((1, cw), lambda j, i: (0, j))],
        out_shape=[SDS((T, C), dx_dtype), SDS((K, C), f32), SDS((1, C), f32)],
        scratch_shapes=[pltpu.VMEM((nsh, n + SUBLANES, cw), f32), pltpu.VMEM((nsh, n + SUBLANES, cw), f32),
                        pltpu.VMEM((SUBLANES * (K + 1), cw), f32)],
        compiler_params=_cp("arbitrary", "arbitrary"),
    )(dy, dy, x, x, w)


def _tri(n, lower):
    r = lax.broadcasted_iota(jnp.int32, (n, n), 0)
    c = lax.broadcasted_iota(jnp.int32, (n, n), 1)
    return jnp.where((r >= c) if lower else (r <= c), 1.0, 0.0).astype(f32)


def _hdot(a, b):
    return jnp.dot(a, b, precision=lax.Precision.HIGHEST, preferred_element_type=f32)


def _fox_gate_fwd(small, fb_row, *, name, tt=512):
    T = small.shape[0]
    tt = _pick(T, (tt,))

    def body(sm_ref, fb_ref, ck_ref, carry):
        @pl.when(pl.program_id(0) == 0)
        def _():
            carry[...] = jnp.zeros_like(carry)

        logf = -_softplus(-(sm_ref[...] + fb_ref[...]))
        cs = _hdot(_tri(tt, True), logf) + carry[...]
        c2 = cs * LOG2E
        for h in range(FOX_HEADS):
            ck_ref[h] = c2[:, h:h + 1]
        carry[...] = cs[tt - 1:tt, :]

    blk = pl.BlockSpec((tt, LANES), lambda i: (i, 0))
    return _pcall(
        body, name=name, grid=(T // tt,), in_specs=[blk, pl.BlockSpec((1, LANES), lambda i: (0, 0))],
        out_specs=pl.BlockSpec((FOX_HEADS, tt, 1), lambda i: (0, i, 0)),
        out_shape=SDS((FOX_HEADS, T, 1), f32), scratch_shapes=[pltpu.VMEM((1, LANES), f32)],
        compiler_params=_cp("arbitrary"),
    )(small, fb_row)


def _fox_gate_bwd(dc, small, fb_row, dsmall_dt, *, name, tt=512):
    T = small.shape[0]
    tt = _pick(T, (tt,))
    nt = T // tt

    def body(dc_ref, sm_ref, fb_ref, dd_ref, ds_ref, dfb_ref, carry):
        @pl.when(pl.program_id(0) == 0)
        def _():
            carry[...] = jnp.zeros_like(carry)
            dfb_ref[...] = jnp.zeros_like(dfb_ref)

        dl = _hdot(_tri(tt, False), dc_ref[...]) + carry[...]
        carry[...] = dl[0:1, :]
        lane = lax.broadcasted_iota(jnp.int32, (tt, LANES), 1)
        df = jnp.where(lane < FOX_HEADS, dl * jax.nn.sigmoid(-(sm_ref[...] + fb_ref[...])), 0.0)
        ds_ref[...] = jnp.where(lane < FOX_HEADS, df, dd_ref[...])
        dfb_ref[...] += jnp.sum(df, axis=0, keepdims=True)

    blk = pl.BlockSpec((tt, LANES), lambda i: (nt - 1 - i, 0))
    row = pl.BlockSpec((1, LANES), lambda i: (0, 0))
    return _pcall(
        body, name=name, grid=(nt,), in_specs=[blk, blk, row, blk], out_specs=[blk, row],
        out_shape=[SDS((T, LANES), f32), SDS((1, LANES), f32)], scratch_shapes=[pltpu.VMEM((1, LANES), f32)],
        compiler_params=_cp("arbitrary"),
    )(dc, small, fb_row, dsmall_dt)


_NT = (((1,), (1,)), ((), ()))
_TN = (((0,), (0,)), ((), ()))


def _causal(n):
    r = lax.broadcasted_iota(jnp.int32, (n, n), 0)
    c = lax.broadcasted_iota(jnp.int32, (n, n), 1)
    return r >= c


LOG2E = 1.4426950408889634
_SCALE = FOX_DH ** -0.5
_SCALE2 = _SCALE * LOG2E
CW = 256
FLASH_TQ = 1024
FLASH_CH = 1024
FLASH_TK = 1024


def _to_t(a, w):
    T = a.shape[0]
    return a.reshape(T // w, w, FOX_HEADS, FOX_DH).transpose(2, 0, 3, 1)


def _from_t(at):
    hh, n, dh, w = at.shape
    return at.transpose(1, 3, 0, 2).reshape(n * w, hh * dh)


def _flash_fwd(u, qT, vT, ck2, *, name, side=None):
    T = u.shape[0]
    H, dh = FOX_HEADS, FOX_DH
    TQ, CH = min(FLASH_TQ, T), min(FLASH_CH, T)
    NC, NQ, RQ, RC, R = T // CW, T // TQ, TQ // CW, CH // CW, TQ // CH
    ko = MAIN_OFF["k"] // dh
    ns = len(side[1]) if side is not None else 0
    s_in, s_out, s_shapes, s_sems = _side_specs(side) if side is not None else ([], [], [], [])

    def body(qT_ref, k_ref, vT_ref, ck_ref, *rest):
        oT_ref, lse_ref = rest[ns], rest[ns + 1]
        i = pl.program_id(1)
        if side is not None:
            hh = pl.program_id(0)
            _side_run(side, rest[:ns] + rest[ns + 2:], (hh == 0) & (i == 0), (hh == H - 1) & (i == NQ - 1))
        qt = jnp.concatenate([qT_ref[r] for r in range(RQ)], axis=1) if RQ > 1 else qT_ref[0]

        def scores(j):
            st = pl.multiple_of(j * CH, CH)
            kc = k_ref[pl.ds(st, CH), :]
            return jnp.dot(kc, qt, preferred_element_type=f32) * _SCALE2 - ck_ref[pl.ds(st, CH), :]

        def update(j, s, carry):
            m, l, acc = carry
            m_new = jnp.maximum(m, jnp.max(s, axis=0, keepdims=True))
            alpha = jnp.exp2(m - m_new)
            p = jnp.exp2(s - m_new)
            l = alpha * l + jnp.sum(p, axis=0, keepdims=True)
            pb = p.astype(bf16)
            pv = jnp.dot(vT_ref[j * RC], pb[0:CW], preferred_element_type=f32)
            for r in range(1, RC):
                pv = pv + jnp.dot(vT_ref[j * RC + r], pb[r * CW:(r + 1) * CW], preferred_element_type=f32)
            return m_new, l, alpha * acc + pv

        def band(s, d):
            r = lax.broadcasted_iota(jnp.int32, (CH, TQ), 0) + d * CH
            c = lax.broadcasted_iota(jnp.int32, (CH, TQ), 1)
            return jnp.where(r <= c, s, NEG)

        carry = (jnp.full((1, TQ), NEG, f32), jnp.zeros((1, TQ), f32), jnp.zeros((dh, TQ), f32))
        carry = lax.fori_loop(0, i * R, lambda j, c: update(j, scores(j), c), carry)
        for d in range(R):
            carry = update(i * R + d, band(scores(i * R + d), d), carry)
        m, l, acc = carry
        o = (acc / l).astype(oT_ref.dtype)
        lse = m + jnp.log2(l)
        for r in range(RQ):
            oT_ref[r] = o[:, r * CW:(r + 1) * CW]
            lse_ref[r] = lse[:, r * CW:(r + 1) * CW]

    return _pcall(
        body, name=name, grid=(H, NQ),
        in_specs=[pl.BlockSpec((None, RQ, dh, CW), lambda h, i: (h, i, 0, 0)),
                  pl.BlockSpec((T, dh), lambda h, i: (0, ko + h)),
                  pl.BlockSpec((None, NC, dh, CW), lambda h, i: (h, 0, 0, 0)),
                  pl.BlockSpec((None, T, 1), lambda h, i: (h, 0, 0))] + s_in,
        out_specs=[pl.BlockSpec((None, RQ, dh, CW), lambda h, i: (h, i, 0, 0)),
                   pl.BlockSpec((None, RQ, 1, CW), lambda h, i: (h, i, 0, 0))] + s_out,
        out_shape=[SDS((H, NC, dh, CW), bf16), SDS((H, NC, 1, CW), f32)] + s_shapes,
        scratch_shapes=s_sems,
        compiler_params=_cp("parallel", "arbitrary") if side is None else _cp("arbitrary", "arbitrary"),
    )(qT, u, vT, ck2, *(side[1] if side is not None else []))


def _flash_delta(doT, oT, *, name):
    H, NC, dh, _ = doT.shape

    def body(a_ref, b_ref, d_ref):
        for i in range(NC):
            d_ref[i] = jnp.sum(a_ref[i].astype(f32) * b_ref[i].astype(f32), axis=0, keepdims=True)

    blk = pl.BlockSpec((None, NC, dh, CW), lambda h: (h, 0, 0, 0))
    return _pcall(
        body, name=name, grid=(H,), in_specs=[blk, blk],
        out_specs=pl.BlockSpec((None, NC, 1, CW), lambda h: (h, 0, 0, 0)),
        out_shape=SDS((H, NC, 1, CW), f32), compiler_params=_cp("parallel"),
    )(doT, oT)


def _flash_bwd(u, qT, kT, do, doT, ck2, lse2, delta, *, name, side=None):
    T = u.shape[0]
    H, dh = FOX_HEADS, FOX_DH
    TK = min(FLASH_TK, T)
    NC, NK, R = T // CW, T // TK, TK // CW
    qo, ko, vo = MAIN_OFF["q"] // dh, MAIN_OFF["k"] // dh, MAIN_OFF["v"] // dh
    ns = len(side[1]) if side is not None else 0
    s_in, s_out, s_shapes, s_sems = _side_specs(side) if side is not None else ([], [], [], [])

    def body(q_ref, k_ref, v_ref, qT_ref, kT_ref, do_ref, doT_ref, ck_ref, lse_ref, dl_ref, *rest):
        dk_ref, dv_ref, dck_ref, dqT_ref, dcq_ref = rest[ns:ns + 5]
        dk_s, dv_s, dc_s = rest[2 * ns + 5:2 * ns + 8]
        j = pl.program_id(1)
        if side is not None:
            hh = pl.program_id(0)
            _side_run(side, rest[:ns] + rest[ns + 5:2 * ns + 5] + rest[2 * ns + 8:],
                      (hh == 0) & (j == 0), (hh == H - 1) & (j == NK - 1))

        @pl.when(j == 0)
        def _():
            dqT_ref[...] = jnp.zeros_like(dqT_ref)
            dcq_ref[...] = jnp.zeros_like(dcq_ref)

        dk_s[...] = jnp.zeros_like(dk_s)
        dv_s[...] = jnp.zeros_like(dv_s)
        dc_s[...] = jnp.zeros_like(dc_s)
        k, v, kt = k_ref[...], v_ref[...], kT_ref[...]
        ckc = ck_ref[...]

        def chunk(i, d):
            st = pl.multiple_of(i * CW, CW)
            s = jnp.dot(k, qT_ref[i], preferred_element_type=f32) * _SCALE2 - ckc
            if d is not None:
                r = lax.broadcasted_iota(jnp.int32, (TK, CW), 0)
                c = lax.broadcasted_iota(jnp.int32, (TK, CW), 1) + d * CW
                s = jnp.where(r <= c, s, NEG)
            p = jnp.exp2(s - lse_ref[i])
            dp = jnp.dot(v, doT_ref[i], preferred_element_type=f32)
            ds = p * (dp - dl_ref[i])
            dsb = ds.astype(bf16)
            dv_s[...] += jnp.dot(p.astype(bf16), do_ref[pl.ds(st, CW), :], preferred_element_type=f32)
            dk_s[...] += jnp.dot(dsb, q_ref[pl.ds(st, CW), :], preferred_element_type=f32)
            dqT_ref[i] += jnp.dot(kt, dsb, preferred_element_type=f32)
            dcq_ref[i] += jnp.sum(ds, axis=0, keepdims=True)
            part = ds[:, 0:LANES]
            for t in range(1, CW // LANES):
                part = part + ds[:, t * LANES:(t + 1) * LANES]
            dc_s[...] += part

        for d in range(R):
            chunk(j * R + d, d)

        def rest(i, c):
            chunk(i, None)
            return c

        lax.fori_loop(j * R + R, NC, rest, 0)
        dk_ref[...] = (dk_s[...] * _SCALE).astype(dk_ref.dtype)
        dv_ref[...] = dv_s[...].astype(dv_ref.dtype)
        dck_ref[...] = -jnp.sum(dc_s[...], axis=1, keepdims=True)

        @pl.when(j == NK - 1)
        def _():
            dqT_ref[...] = dqT_ref[...] * _SCALE

    res = lambda h, j: (h, 0, 0, 0)
    return _pcall(
        body, name=name, grid=(H, NK),
        in_specs=[pl.BlockSpec((T, dh), lambda h, j: (0, qo + h)),
                  pl.BlockSpec((TK, dh), lambda h, j: (j, ko + h)),
                  pl.BlockSpec((TK, dh), lambda h, j: (j, vo + h)),
                  pl.BlockSpec((None, NC, dh, CW), res),
                  pl.BlockSpec((None, None, dh, TK), lambda h, j: (h, j, 0, 0)),
                  pl.BlockSpec((T, dh), lambda h, j: (0, h)),
                  pl.BlockSpec((None, NC, dh, CW), res),
                  pl.BlockSpec((None, TK, 1), lambda h, j: (h, j, 0)),
                  pl.BlockSpec((None, NC, 1, CW), res),
                  pl.BlockSpec((None, NC, 1, CW), res)] + s_in,
        out_specs=[pl.BlockSpec((TK, dh), lambda h, j: (j, h)),
                   pl.BlockSpec((TK, dh), lambda h, j: (j, h)),
                   pl.BlockSpec((None, TK, 1), lambda h, j: (h, j, 0)),
                   pl.BlockSpec((None, NC, dh, CW), res),
                   pl.BlockSpec((None, NC, 1, CW), res)] + s_out,
        out_shape=[SDS((T, H * dh), bf16), SDS((T, H * dh), bf16), SDS((H, T, 1), f32),
                   SDS((H, NC, dh, CW), f32), SDS((H, NC, 1, CW), f32)] + s_shapes,
        scratch_shapes=[pltpu.VMEM((TK, dh), f32), pltpu.VMEM((TK, dh), f32),
                        pltpu.VMEM((TK, LANES), f32)] + s_sems,
        compiler_params=_cp("parallel", "arbitrary") if side is None else _cp("arbitrary", "arbitrary"),
    )(u, u, u, qT, kT, do, doT, ck2, lse2, delta, *(side[1] if side is not None else []))


def _ssd_common(sm_ref, dtb_ref, alog_ref):
    L = SSD_L
    dt = _softplus(sm_ref[...] + dtb_ref[...])
    a = -jnp.exp(alog_ref[...])
    cs = _hdot(_tri(L, True), dt * a)
    return dt, a, cs, cs.T, cs[L - 1:L, :]


def _ssd_fwd(xbc, small, dtb_row, alog_row, dskip_row, *, name, side=None):
    T = xbc.shape[0]
    L, P, N = SSD_L, SSM_P, SSM_N
    nc = T // L
    hpg = SSM_HEADS // SSM_G
    ns = len(side[1]) if side is not None else 0
    s_in, s_out, s_shapes, s_sems = _side_specs(side) if side is not None else ([], [], [], [])

    def body(x_ref, sm_ref, dtb_ref, alog_ref, dsk_ref, *rest):
        y_ref, hs_ref, h_s = rest[ns], rest[ns + 1], rest[2 * ns + 2]
        if side is not None:
            step = pl.program_id(0)
            _side_run(side, rest[:ns] + rest[ns + 2:2 * ns + 2] + rest[2 * ns + 3:], step == 0, step == nc - 1)

        @pl.when(pl.program_id(0) == 0)
        def _():
            h_s[...] = jnp.zeros_like(h_s)

        dt, a, cs, csT, tot = _ssd_common(sm_ref, dtb_ref, alog_ref)
        hs_ref[...] = h_s[...]
        causal = _causal(L)
        for g in range(SSM_G):
            bg = x_ref[:, SSM_W + g * N:SSM_W + (g + 1) * N]
            cg = x_ref[:, SSM_W + SSM_G * N + g * N:SSM_W + SSM_G * N + (g + 1) * N]
            cb = lax.dot_general(cg, bg, _NT, preferred_element_type=f32)
            hg = h_s[g * hpg * P:(g + 1) * hpg * P, :]
            yoff = lax.dot_general(cg, hg.astype(bf16), _NT, preferred_element_type=f32)
            for hh in range(hpg):
                h = g * hpg + hh
                ln = HL + h
                acol, arow = cs[:, ln:ln + 1], csT[ln:ln + 1, :]
                w = jnp.where(causal, cb * jnp.exp(jnp.minimum(acol - arow, 0.0)), 0.0)
                xs = x_ref[:, h * P:(h + 1) * P].astype(f32)
                xh = xs * dt[:, ln:ln + 1]
                y = jnp.dot(w.astype(bf16), xh.astype(bf16), preferred_element_type=f32)
                y = y + jnp.exp(acol) * yoff[:, hh * P:(hh + 1) * P] + xs * dsk_ref[:, ln:ln + 1]
                y_ref[:, h * P:(h + 1) * P] = y.astype(y_ref.dtype)
                decay = jnp.exp(tot[:, ln:ln + 1] - acol)
                st = lax.dot_general((xh * decay).astype(bf16), bg, _TN, preferred_element_type=f32)
                h_s[h * P:(h + 1) * P, :] = jnp.exp(tot[:, ln:ln + 1]) * h_s[h * P:(h + 1) * P, :] + st

    row = pl.BlockSpec((1, LANES), lambda c: (0, 0))
    return _pcall(
        body, name=name, grid=(nc,),
        in_specs=[pl.BlockSpec((L, XBC_W), lambda c: (c, 0)), pl.BlockSpec((L, LANES), lambda c: (c, 0)),
                  row, row, row] + s_in,
        out_specs=[pl.BlockSpec((L, SSM_W), lambda c: (c, 0)),
                   pl.BlockSpec((None, SSM_HEADS * P, N), lambda c: (c, 0, 0))] + s_out,
        out_shape=[SDS((T, SSM_W), bf16), SDS((nc, SSM_HEADS * P, N), f32)] + s_shapes,
        scratch_shapes=[pltpu.VMEM((SSM_HEADS * P, N), f32)] + s_sems,
        compiler_params=_cp("arbitrary"),
    )(xbc, small, dtb_row, alog_row, dskip_row, *(side[1] if side is not None else []))


def _ssd_bwd(dy, xbc, small, dtb_row, alog_row, dskip_row, hstates, *, name):
    T = xbc.shape[0]
    L, P, N = SSD_L, SSM_P, SSM_N
    nc = T // L
    hpg = SSM_HEADS // SSM_G
    GW = hpg * P

    def body(dy_ref, x_ref, sm_ref, dtb_ref, alog_ref, dsk_ref, hs_ref,
             dx_ref, dsm_ref, ddtb_ref, dalog_ref, ddsk_ref, dh_s, ea_s, dx_s):
        @pl.when(pl.program_id(0) == 0)
        def _():
            dh_s[...] = jnp.zeros_like(dh_s)
            ddtb_ref[...] = jnp.zeros_like(ddtb_ref)
            dalog_ref[...] = jnp.zeros_like(dalog_ref)
            ddsk_ref[...] = jnp.zeros_like(ddsk_ref)

        dt, a, cs, csT, tot = _ssd_common(sm_ref, dtb_ref, alog_ref)
        causal = _causal(L)
        lane = lax.broadcasted_iota(jnp.int32, (L, LANES), 1)
        sub = lax.broadcasted_iota(jnp.int32, (LANES, L), 0)
        lane1 = lax.broadcasted_iota(jnp.int32, (1, LANES), 1)
        da_col = jnp.zeros((L, LANES), f32)
        da_row = jnp.zeros((LANES, L), f32)
        da_tot = jnp.zeros((1, LANES), f32)
        ddt = jnp.zeros((L, LANES), f32)
        ddsk = jnp.zeros((1, LANES), f32)
        for g in range(SSM_G):
            bo, co = SSM_W + g * N, SSM_W + SSM_G * N + g * N
            bg, cg = x_ref[:, bo:bo + N], x_ref[:, co:co + N]
            cb = lax.dot_general(cg, bg, _NT, preferred_element_type=f32)
            hg = hs_ref[g * GW:(g + 1) * GW, :]
            hgb = hg.astype(bf16)
            dhn = dh_s[g * GW:(g + 1) * GW, :]
            dhnb = dhn.astype(bf16)
            yoff = lax.dot_general(cg, hgb, _NT, preferred_element_type=f32)
            bdh = lax.dot_general(bg, dhnb, _NT, preferred_element_type=f32)
            dcb = jnp.zeros((L, L), f32)
            for hh in range(hpg):
                h = g * hpg + hh
                ln = HL + h
                sl = slice(hh * P, (hh + 1) * P)
                acol, arow = cs[:, ln:ln + 1], csT[ln:ln + 1, :]
                e = jnp.where(causal, jnp.exp(jnp.minimum(acol - arow, 0.0)), 0.0)
                w = cb * e
                xs = x_ref[:, h * P:(h + 1) * P].astype(f32)
                dth = dt[:, ln:ln + 1]
                xh = xs * dth
                dyh = dy_ref[:, h * P:(h + 1) * P].astype(f32)
                dyb = dyh.astype(bf16)
                ea = jnp.exp(acol)
                decay = jnp.exp(tot[:, ln:ln + 1] - acol)
                etot = jnp.exp(tot[:, ln:ln + 1])
                dw = lax.dot_general(dyb, xh.astype(bf16), _NT, preferred_element_type=f32)
                m = dw * w
                dcb = dcb + dw * e
                dxs_state = decay * bdh[:, sl]
                dxh = lax.dot_general(w.astype(bf16), dyb, _TN, preferred_element_type=f32) + dxs_state
                r = jnp.sum(xh * dxs_state, axis=1, keepdims=True)
                col = (jnp.sum(m, axis=1, keepdims=True) - r
                       + jnp.sum(dyh * (ea * yoff[:, sl]), axis=1, keepdims=True))
                da_col = da_col + jnp.where(lane == ln, col, 0.0)
                da_row = da_row + jnp.where(sub == ln, jnp.sum(m, axis=0, keepdims=True), 0.0)
                hprod = jnp.sum(jnp.sum(dhn[sl, :] * hg[sl, :], axis=1, keepdims=True), axis=0, keepdims=True)
                t_h = jnp.sum(r, axis=0, keepdims=True) + etot * hprod
                da_tot = da_tot + jnp.where(lane1 == ln, t_h, 0.0)
                ea_s[:, sl] = (ea * dyh).astype(bf16)
                dx_s[:, sl] = (decay * xh).astype(bf16)
                dx_ref[:, h * P:(h + 1) * P] = (dxh * dth + dyh * dsk_ref[:, ln:ln + 1]).astype(dx_ref.dtype)
                ddt = ddt + jnp.where(lane == ln, jnp.sum(dxh * xs, axis=1, keepdims=True), 0.0)
                ddsk = ddsk + jnp.where(
                    lane1 == ln, jnp.sum(jnp.sum(dyh * xs, axis=1, keepdims=True), axis=0, keepdims=True), 0.0)
                dh_s[g * GW + hh * P:g * GW + (hh + 1) * P, :] = etot * dhn[sl, :]
            dcbb = dcb.astype(bf16)
            eab, dxb = ea_s[...], dx_s[...]
            dcg = (jnp.dot(dcbb, bg, preferred_element_type=f32)
                   + jnp.dot(eab, hgb, preferred_element_type=f32))
            dbg = (lax.dot_general(dcbb, cg, _TN, preferred_element_type=f32)
                   + jnp.dot(dxb, dhnb, preferred_element_type=f32))
            dx_ref[:, bo:bo + N] = dbg.astype(dx_ref.dtype)
            dx_ref[:, co:co + N] = dcg.astype(dx_ref.dtype)
            dh_s[g * GW:(g + 1) * GW, :] += lax.dot_general(eab, cg, _TN, preferred_element_type=f32)

        rowi = lax.broadcasted_iota(jnp.int32, (L, LANES), 0)
        da_cs = da_col - da_row.T + jnp.where(rowi == L - 1, da_tot, 0.0)
        dda = _hdot(_tri(L, False), da_cs)
        ddt = ddt + dda * a
        heads = (lane >= HL) & (lane < HL + SSM_HEADS)
        draw = jnp.where(heads, ddt * jax.nn.sigmoid(sm_ref[...] + dtb_ref[...]), 0.0)
        dsm_ref[...] = draw
        ddtb_ref[...] += jnp.sum(draw, axis=0, keepdims=True)
        dalog_ref[...] += jnp.where(lane1 >= HL, jnp.sum(dda * dt, axis=0, keepdims=True) * a, 0.0)
        ddsk_ref[...] += ddsk

    rev = lambda c: (nc - 1 - c, 0)
    row = pl.BlockSpec((1, LANES), lambda c: (0, 0))
    return _pcall(
        body, name=name, grid=(nc,),
        in_specs=[pl.BlockSpec((L, SSM_W), rev), pl.BlockSpec((L, XBC_W), rev), pl.BlockSpec((L, LANES), rev),
                  row, row, row, pl.BlockSpec((None, SSM_HEADS * P, N), lambda c: (nc - 1 - c, 0, 0))],
        out_specs=[pl.BlockSpec((L, XBC_W), rev), pl.BlockSpec((L, LANES), rev), row, row, row],
        out_shape=[SDS((T, XBC_W), bf16), SDS((T, LANES), f32)] + [SDS((1, LANES), f32)] * 3,
        scratch_shapes=[pltpu.VMEM((SSM_HEADS * P, N), f32), pltpu.VMEM((L, GW), bf16), pltpu.VMEM((L, GW), bf16)],
        compiler_params=_cp("arbitrary"),
    )(dy, xbc, small, dtb_row, alog_row, dskip_row, hstates)


def _loss_head(x, w, target, *, name, tt=256):
    T, D = x.shape
    tt = _pick(T, (tt, 128))

    def fn(x, w, tg):
        (y,) = _f_rms(x, w)
        return 0.5 * jnp.sum(jnp.sum(jnp.square(y - tg), axis=1, keepdims=True), axis=0, keepdims=True) / D

    def body(x_ref, w_ref, t_ref, loss_ref, dx_ref, dw_ref):
        i = pl.program_id(0)
        ls, vjp = jax.vjp(fn, x_ref[...], w_ref[...], t_ref[...])
        dx, dw, _ = vjp(jnp.ones((1, 1), f32))
        dx_ref[...] = dx

        @pl.when(i == 0)
        def _():
            loss_ref[...] = ls
            dw_ref[...] = dw

        @pl.when(i > 0)
        def _():
            loss_ref[...] += ls
            dw_ref[...] += dw

    blk = pl.BlockSpec((tt, D), lambda i: (i, 0))
    row = pl.BlockSpec((1, D), lambda i: (0, 0))
    return _pcall(
        body, name=name, grid=(T // tt,), in_specs=[blk, row, blk],
        out_specs=[pl.BlockSpec((1, 1), lambda i: (0, 0)), blk, row],
        out_shape=[SDS((1, 1), f32), SDS((T, D), f32), SDS((1, D), f32)],
        compiler_params=_cp("arbitrary"),
    )(x, w, target)


ADAM_BLOCK_ELEMS = 256 * 1024


def _row_tile(rows, cols, nbuf=1):
    for t in (2048, 1024, 512, 256, 128, 64, 32, 16, 8):
        if t * cols * max(1, nbuf // 2) <= ADAM_BLOCK_ELEMS and rows % t == 0:
            return t
    return rows


def _adamw(gss, w, m, v, *, name):
    ns, Rp, C = gss[0].shape
    npart = len(gss)
    tr = _row_tile(Rp, C, npart)
    per = Rp // tr
    c1 = 1.0 / (1.0 - ADAM_B1 ** ADAM_STEP)
    c2 = 1.0 / (1.0 - ADAM_B2 ** ADAM_STEP)

    def body(*refs):
        g_refs = refs[:npart]
        w_ref, m_ref, v_ref, go_ref, d_ref, mo_ref, vo_ref = refs[npart:]
        part = pl.program_id(0) // per

        def update(g_ref):
            g = g_ref[0].astype(f32)
            for s in range(1, ns):
                g = g + g_ref[s].astype(f32)
            mn = ADAM_B1 * m_ref[...] + (1.0 - ADAM_B1) * g
            vn = ADAM_B2 * v_ref[...] + (1.0 - ADAM_B2) * (g * g)
            go_ref[...] = g
            mo_ref[...] = mn
            vo_ref[...] = vn
            d_ref[...] = -ADAM_LR * ((mn * c1) / (jnp.sqrt(vn * c2) + ADAM_EPS) + ADAM_WD * w_ref[...])

        for p in range(npart):
            @pl.when(part == p)
            def _(p=p):
                update(g_refs[p])

    blk = pl.BlockSpec((tr, C), lambda i: (i, 0))
    g_specs = [pl.BlockSpec((ns, tr, C), functools.partial(
        lambda i, p: (0, jnp.clip(i - p * per, 0, per - 1), 0), p=p)) for p in range(npart)]
    return _pcall(
        body, name=name, grid=(npart * per,), in_specs=g_specs + [blk, blk, blk],
        out_specs=[blk] * 4, out_shape=[SDS((npart * Rp, C), f32)] * 4,
        compiler_params=_cp("parallel"),
    )(*gss, w, m, v)


def _sum_slabs(gs, *, name):
    ns, R, C = gs.shape
    tr = _row_tile(R, C)

    def body(g_ref, o_ref):
        g = g_ref[0]
        for s in range(1, ns):
            g = g + g_ref[s]
        o_ref[...] = g

    return _pcall(
        body, name=name, grid=(R // tr,), in_specs=[pl.BlockSpec((ns, tr, C), lambda i: (0, i, 0))],
        out_specs=pl.BlockSpec((tr, C), lambda i: (i, 0)), out_shape=SDS((R, C), f32),
        compiler_params=_cp("parallel"),
    )(gs)


def _allgather(arrs, *, name):
    n = len(arrs)

    def body(*refs):
        x_refs, out_refs = refs[:n], refs[n:2 * n]
        send_sems, recv_sems, local_sems = refs[2 * n:]
        x, y, c = lax.axis_index("x"), lax.axis_index("y"), lax.axis_index("c")
        me, sibling = (x, y, c), (x, y, 1 - c)
        chips = [(1 - x, y), (x, 1 - y), (1 - x, 1 - y)]

        def copy(a, k, block, to, src=None):
            px, py, pc = block
            slot = out_refs[a].at[4 * px + 2 * py + pc]
            return pltpu.make_async_remote_copy(
                src_ref=slot if src is None else src, dst_ref=slot,
                send_sem=send_sems.at[7 * a + k], recv_sem=recv_sems.at[7 * a + k],
                device_id=to, device_id_type=_MESH)

        mine, first, passed = [], [], []
        for a in range(n):
            cp = pltpu.make_async_copy(x_refs[a], out_refs[a].at[4 * x + 2 * y + c], local_sems.at[a])
            cp.start()
            mine.append(cp)
            first.append(copy(a, 0, me, sibling, src=x_refs[a]))
            first += [copy(a, 1 + j, me, (*chip, c), src=x_refs[a]) for j, chip in enumerate(chips)]
        for cp in first:
            cp.start()
        for j, chip in enumerate(chips):
            for a in range(n):
                copy(a, 1 + j, (*chip, c), me).wait_recv()
                cp = copy(a, 4 + j, (*chip, c), sibling)
                cp.start()
                passed.append(cp)
        for a in range(n):
            copy(a, 0, sibling, me).wait_recv()
            for j, chip in enumerate(chips):
                copy(a, 4 + j, (*chip, 1 - c), me).wait_recv()
        for cp in first + passed:
            cp.wait_send()
        for cp in mine:
            cp.wait()

    return _pcall(
        body, name=name, in_specs=[_HBM] * n, out_specs=[_HBM] * n,
        out_shape=[SDS((N_DEV,) + a.shape, a.dtype) for a in arrs],
        scratch_shapes=[pltpu.SemaphoreType.DMA((7 * n,)), pltpu.SemaphoreType.DMA((7 * n,)),
                        pltpu.SemaphoreType.DMA((n,))],
    )(*arrs)


def _exchange(arrs, *, name):
    n = len(arrs)

    def body(*refs):
        in_refs, out_refs = refs[:n], refs[n:2 * n]
        send_sems, recv_sems, local_sems = refs[2 * n:]
        x, y, c = lax.axis_index("x"), lax.axis_index("y"), lax.axis_index("c")
        me = 4 * x + 2 * y + c
        mine, sends, recvs = [], [], []
        for a in range(n):
            cp = pltpu.make_async_copy(in_refs[a].at[me], out_refs[a].at[me], local_sems.at[a])
            cp.start()
            mine.append(cp)
        for k in range(1, N_DEV):
            px, py, pc = x ^ ((k >> 2) & 1), y ^ ((k >> 1) & 1), c ^ (k & 1)
            peer = 4 * px + 2 * py + pc
            for a in range(n):
                def copy(src_slot, dst_slot):
                    return pltpu.make_async_remote_copy(
                        src_ref=in_refs[a].at[src_slot], dst_ref=out_refs[a].at[dst_slot],
                        send_sem=send_sems.at[7 * a + k - 1], recv_sem=recv_sems.at[7 * a + k - 1],
                        device_id=(px, py, pc), device_id_type=_MESH)

                sends.append(copy(peer, me))
                recvs.append(copy(me, peer))
                sends[-1].start()
        for cp in recvs:
            cp.wait_recv()
        for cp in sends:
            cp.wait_send()
        for cp in mine:
            cp.wait()

    return _pcall(
        body, name=name, in_specs=[_HBM] * n, out_specs=[_HBM] * n,
        out_shape=[SDS(a.shape, a.dtype) for a in arrs],
        scratch_shapes=[pltpu.SemaphoreType.DMA((7 * n,)), pltpu.SemaphoreType.DMA((7 * n,)),
                        pltpu.SemaphoreType.DMA((n,))],
    )(*arrs)


def _row128(vec, off):
    return jnp.pad(vec.astype(f32), (off, LANES - off - vec.shape[0]))[None, :]


def _cb(name, cw):
    return MAIN_OFF[name] // cw


def _layer_fwd(x, p, tag, shard, nxt):
    s = {"x": x}
    sh_gate, sh_branch, sh_out = shard
    (h,) = _rowwise(_f_rms, [(x, 0)], [p["norm_w"]], [bf16], cw=D_MODEL, ncol=1, name=f"rms_{tag}")
    u, g_gate, g_out = _mm(h, p["w_main"], mode="nn", out_dtype=bf16, name=f"in_main_{tag}",
                           side=("gather", [sh_gate, sh_out]))
    small = _mm(h, p["w_small"], mode="nn", out_dtype=f32, name=f"in_small_{tag}")
    s.update(h=h, u=u, small=small)
    W = BW
    SH = D_MODEL // N_DEV
    w_gate = jnp.moveaxis(g_gate.reshape(N_DEV, 4, SH, D_MODEL), 0, 1).reshape(4, D_MODEL, D_MODEL)
    p["w_gate"] = [w_gate[i] for i in range(4)]
    p["w_out"] = g_out.reshape(D_MODEL, D_MODEL)

    ck2 = _fox_gate_fwd(small, p["fb_row"], name=f"fox_gate_{tag}")
    qT = _to_t(u[:, MAIN_OFF["q"]:MAIN_OFF["q"] + W], CW)
    vT = _to_t(u[:, MAIN_OFF["v"]:MAIN_OFF["v"] + W], CW)
    oT, lse2, g_branch, *nxt_a = _flash_fwd(u, qT, vT, ck2, name=f"flash_{tag}",
                                            side=("gather", [sh_branch] + (nxt[:1] if nxt else [])))
    w_branch = jnp.moveaxis(g_branch.reshape(N_DEV, 4, BW, SH), 0, 2).reshape(4, BW, D_MODEL)
    p["w_branch"] = [w_branch[i] for i in range(4)]
    o = _from_t(oT)
    (ya,) = _rowwise(_f_ya, [(o, 0), (u, _cb("ga", W))], [], [bf16], cw=W, ncol=1, name=f"ya_{tag}")
    s.update(ck2=ck2, qT=qT, o=o, oT=oT, lse2=lse2)

    pre_b, xbc_c = _conv_fwd(u, _cb("xbc", 512), XBC_W, p["ssm_conv_w"], p["ssm_conv_b"], with_silu=True,
                             name=f"ssm_conv_{tag}")
    y_ssd, hst, *nxt_b = _ssd_fwd(xbc_c, small, p["dtb_row"], p["alog_row"], p["dskip_row"], name=f"ssd_{tag}",
                                  side=("gather", nxt[1:]) if nxt else None)
    (yb,) = _rowwise(_f_yb, [(y_ssd, 0), (u, _cb("z", W))], [p["ssm_norm_w"]], [bf16], cw=W, ncol=1,
                     name=f"yb_{tag}")
    s.update(pre_b=pre_b, xbc_c=xbc_c, y_ssd=y_ssd, hst=hst)

    (pc,) = _rowwise(_f_mul, [(u, _cb("scc", 512)), (u, _cb("scx", 512))], [], [bf16], cw=512, ncol=2,
                     name=f"sc_pre_{tag}")
    (cv,) = _conv_fwd(pc, 0, W, p["sc_conv_w"], p["sc_conv_b"], with_silu=False, name=f"sc_conv_{tag}")
    (yc,) = _rowwise(_f_yc, [(u, _cb("scb", 512)), (cv, 0), (u, _cb("gc", 512))], [], [bf16], cw=512, ncol=2,
                     name=f"yc_{tag}")
    s.update(pc=pc, cv=cv)

    (pd,) = _rowwise(_f_glu, [(u, _cb("glua", 512)), (u, _cb("glug", 512))], [], [bf16], cw=512, ncol=2,
                     name=f"cf_pre_{tag}")
    (cf,) = _conv_fwd(pd, 0, W, p["cf_conv_w"], p["cf_conv_b"], with_silu=False, name=f"cf_conv_{tag}")
    (yd,) = _rowwise(_f_yd, [(cf, 0), (u, _cb("gd", W))], [p["cf_ln_w"], p["cf_ln_b"]], [bf16], cw=W, ncol=1,
                     name=f"yd_{tag}")
    s.update(pd=pd, cf=cf)

    ys = (ya, yb, yc, yd)
    gs = [_mm(h, p["w_gate"][i], mode="nn", out_dtype=bf16, name=f"gate{i}_{tag}") for i in range(4)]
    ps = [_mm(ys[i], p["w_branch"][i], mode="nn", out_dtype=bf16, name=f"branch{i}_{tag}") for i in range(4)]
    (merged,) = _rowwise(_f_merge, [(a, 0) for a in gs + ps], list(p["b_gate"]), [bf16], cw=512,
                         ncol=D_MODEL // 512, name=f"merge_{tag}")
    out = _mm(merged, p["w_out"], mode="nn", out_dtype=f32, add=x, name=f"out_{tag}")
    s.update(ys=ys, gs=gs, ps=ps, merged=merged)
    return out, s, nxt_a + nxt_b


def _layer_bwd(dout, p, s, tag, prev_in):
    g = {}
    SH = D_MODEL // N_DEV
    W = BW
    u, h, small = s["u"], s["h"], s["small"]
    dob = dout.astype(bf16)
    dmerged = _mm(dob, p["w_out"], mode="nt", out_dtype=bf16, name=f"d_merged_{tag}")
    g["w_out"] = _mm(s["merged"], dob, mode="tn", out_dtype=bf16, name=f"dw_out_{tag}")
    r = _rowwise_bwd(_f_merge, [(a, 0) for a in s["gs"] + s["ps"]], list(p["b_gate"]), [dmerged], [bf16] * 8,
                     cw=512, ncol=D_MODEL // 512, name=f"d_merge_{tag}")
    dgs, dps, g["b_gate"] = r[0:4], r[4:8], r[8:12]
    g["w_gate"] = [_mm(h, dgs[i], mode="tn", out_dtype=bf16, name=f"dw_gate{i}_{tag}") for i in range(4)]
    g["w_branch"] = [_mm(s["ys"][i], dps[i], mode="tn", out_dtype=bf16, name=f"dw_branch{i}_{tag}")
                     for i in range(4)]
    dh = None
    for i in range(4):
        dh = _mm(dgs[i], p["w_gate"][i], mode="nt", out_dtype=f32, add=dh, tn=1024, name=f"dh_gate{i}_{tag}")
    dya, dyb, dyc, dyd = [_mm(dps[i], p["w_branch"][i], mode="nt", out_dtype=bf16, name=f"d_y{i}_{tag}")
                          for i in range(4)]
    du = {}

    do, du["ga"] = _rowwise_bwd(_f_ya, [(s["o"], 0), (u, _cb("ga", W))], [], [dya], [bf16, bf16], cw=W, ncol=1,
                                name=f"d_ya_{tag}")
    doT = _to_t(do, CW)
    delta = _flash_delta(doT, s["oT"], name=f"flash_delta_{tag}")
    kT = _to_t(u[:, MAIN_OFF["k"]:MAIN_OFF["k"] + W], min(FLASH_TK, u.shape[0]))
    own = [jnp.moveaxis(jnp.stack(g["w_gate"]).reshape(4, N_DEV, SH, D_MODEL), 1, 0).reshape(N_DEV, 4 * SH, D_MODEL),
           jnp.moveaxis(jnp.stack(g["w_branch"]).reshape(4, BW, N_DEV, SH), 2, 0).reshape(N_DEV, 4 * BW, SH),
           g["w_out"].reshape(N_DEV, SH, D_MODEL)]
    du["k"], du["v"], dck, dqT, dcq, *got_own = _flash_bwd(u, s["qT"], kT, do, doT, s["ck2"], s["lse2"], delta,
                                                           name=f"flash_bwd_{tag}", side=("exchange", own))
    du["q"] = _from_t(dqT).astype(bf16)
    dc = jnp.pad((dcq.reshape(FOX_HEADS, -1) + dck[:, :, 0]).T, ((0, 0), (0, LANES - FOX_HEADS)))

    dy_ssd, du["z"], g["ssm_norm_w"] = _rowwise_bwd(
        _f_yb, [(s["y_ssd"], 0), (u, _cb("z", W))], [p["ssm_norm_w"]], [dyb], [bf16, bf16], cw=W, ncol=1,
        name=f"d_yb_{tag}")
    dxbc_c, dsmall_dt, g["dtb_row"], g["alog_row"], g["dskip_row"] = _ssd_bwd(
        dy_ssd, s["xbc_c"], small, p["dtb_row"], p["alog_row"], p["dskip_row"], s["hst"], name=f"d_ssd_{tag}")
    (dpre_b,) = _rowwise_bwd(_f_silu, [(s["pre_b"], 0)], [], [dxbc_c], [bf16], cw=512, ncol=3,
                             name=f"d_ssm_silu_{tag}")
    du["xbc"], g["ssm_conv_w"], g["ssm_conv_b"] = _conv_bwd(dpre_b, u, _cb("xbc", 256), XBC_W, p["ssm_conv_w"],
                                                            name=f"d_ssm_conv_{tag}")

    du["scb"], dcv, du["gc"] = _rowwise_bwd(
        _f_yc, [(u, _cb("scb", 512)), (s["cv"], 0), (u, _cb("gc", 512))], [], [dyc], [bf16] * 3, cw=512, ncol=2,
        name=f"d_yc_{tag}")
    dpc, g["sc_conv_w"], g["sc_conv_b"] = _conv_bwd(dcv, s["pc"], 0, W, p["sc_conv_w"], name=f"d_sc_conv_{tag}")
    du["scc"], du["scx"] = _rowwise_bwd(_f_mul, [(u, _cb("scc", 512)), (u, _cb("scx", 512))], [], [dpc],
                                        [bf16, bf16], cw=512, ncol=2, name=f"d_sc_pre_{tag}")

    dcf, du["gd"], g["cf_ln_w"], g["cf_ln_b"] = _rowwise_bwd(
        _f_yd, [(s["cf"], 0), (u, _cb("gd", W))], [p["cf_ln_w"], p["cf_ln_b"]], [dyd], [bf16, bf16], cw=W, ncol=1,
        name=f"d_yd_{tag}")
    dpd, g["cf_conv_w"], g["cf_conv_b"] = _conv_bwd(dcf, s["pd"], 0, W, p["cf_conv_w"], name=f"d_cf_conv_{tag}")
    du["glua"], du["glug"] = _rowwise_bwd(_f_glu, [(u, _cb("glua", 512)), (u, _cb("glug", 512))], [], [dpd],
                                          [bf16, bf16], cw=512, ncol=2, name=f"d_cf_pre_{tag}")

    dsmall, g["fb_row"] = _fox_gate_bwd(dc, small, p["fb_row"], dsmall_dt, name=f"d_fox_gate_{tag}")
    du_main = jnp.concatenate([du[n] for n in MAIN_NAMES], axis=1)
    dsb = dsmall.astype(bf16)
    got_prev = []
    sides = (("exchange", prev_in[:1]), ("exchange", prev_in[1:])) if prev_in else (None, None)
    g["w_main"] = _mm(h, du_main, mode="tn", out_dtype=bf16, name=f"dw_main_{tag}", side=sides[0])
    if prev_in:
        g["w_main"], got_prev = g["w_main"][0], [g["w_main"][1]]
    g["w_small"] = _mm(h, dsb, mode="tn", out_dtype=bf16, name=f"dw_small_{tag}")
    dh = _mm(du_main, p["w_main"], mode="nt", out_dtype=f32, add=dh, tn=1024, name=f"dh_main_{tag}", side=sides[1])
    if prev_in:
        dh, got_prev = dh[0], got_prev + [dh[1]]
    dh = _mm(dsb, p["w_small"], mode="nt", out_dtype=f32, add=dh, tn=1024, name=f"dh_small_{tag}")
    dx, g["norm_w"] = _rowwise_bwd(_f_rms_res, [(s["x"], 0)], [p["norm_w"]], [dh, dout], [f32], cw=D_MODEL,
                                   ncol=1, name=f"d_rms_{tag}")
    return dx, g, got_own, got_prev


_BIG = ("w_in", "w_gate", "w_branch", "w_out")
_REPL = ("norm_w", "fg_bias", "ssm_conv_b", "dt_bias", "a_log", "d_skip", "ssm_norm_w", "sc_conv_b", "cf_conv_b",
         "cf_ln_w", "cf_ln_b", "final_norm_w")
_SHSM = ("ssm_conv_w", "sc_conv_w", "cf_conv_w", "b_gate")


def _to_rows(a, dtype):
    flat = a.astype(dtype).reshape(-1)
    n = flat.shape[0]
    per = 8 * LANES
    flat = jnp.pad(flat, (0, (-n) % per))
    return flat.reshape(-1, LANES)


def _pack(arrs, dtype):
    rows = [_to_rows(a, dtype) for a in arrs]
    n = sum(r.shape[0] for r in rows)
    rows.append(jnp.zeros(((-n) % 256, LANES), dtype))
    return jnp.concatenate(rows, axis=0)


def _unpack(blob, shapes):
    lead = blob.shape[:-2]
    out, r0 = [], 0
    for shp in shapes:
        n = 1
        for d in shp:
            n *= d
        nr = (-(-n // (8 * LANES))) * 8
        piece = blob[..., r0:r0 + nr, :].reshape(lead + (nr * LANES,))[..., :n]
        out.append(piece.reshape(lead + tuple(shp)))
        r0 += nr
    return out


def _main_cols(w):
    off, pieces = 0, {}
    for n, sz in zip(IN_NAMES, IN_SIZES):
        pieces[n] = w[..., off:off + sz]
        off += sz
    main = jnp.concatenate([pieces[n] for n in MAIN_NAMES], axis=-1)
    pad = jnp.zeros(w.shape[:-1] + (LANES - 24,), w.dtype)
    small = jnp.concatenate([pieces["f"], pieces["dt"], pad], axis=-1)
    return main, small


def _orig_cols(main, small):
    pieces = {}
    for n in MAIN_NAMES:
        sz = XBC_W if n == "xbc" else 1024
        pieces[n] = main[..., MAIN_OFF[n]:MAIN_OFF[n] + sz]
    pieces["f"] = small[..., 0:8]
    pieces["dt"] = small[..., 8:24]
    return jnp.concatenate([pieces[n] for n in IN_NAMES], axis=-1)


def kernel(*args):
    names = ["x", "norm_w", "w_in", "fg_bias", "ssm_conv_w", "ssm_conv_b", "dt_bias", "a_log", "d_skip",
             "ssm_norm_w", "sc_conv_w", "sc_conv_b", "cf_conv_w", "cf_conv_b", "cf_ln_w", "cf_ln_b", "w_gate",
             "b_gate", "w_branch", "w_out", "final_norm_w"]
    wnames = names[1:]
    a = dict(zip(names, args[:21]))
    target = args[21]
    mom = dict(zip(["m_" + n for n in wnames], args[22:42]))
    mom.update(zip(["v_" + n for n in wnames], args[42:62]))
    NL = a["w_in"].shape[0]
    x0 = a["x"][0]
    tgt = target[0]
    me = 4 * lax.axis_index("x") + 2 * lax.axis_index("y") + lax.axis_index("c")

    def rows2d(t):
        return t.reshape(-1, t.shape[-1])

    def shards(l):
        return [rows2d(a[n][l]).astype(bf16) for n in _BIG]

    (sm_all,) = _allgather([_pack([a[n] for n in _SHSM], f32)], name="gather_small")
    ssm_cw_g, sc_cw_g, cf_cw_g, b_gate_g = _unpack(sm_all, [a[n].shape for n in _SHSM])

    def cat_last(gathered):
        return jnp.moveaxis(gathered, 0, -2).reshape(gathered.shape[1:-1] + (-1,))

    ssm_cw, sc_cw, cf_cw, b_gate_full = cat_last(ssm_cw_g), cat_last(sc_cw_g), cat_last(cf_cw_g), cat_last(b_gate_g)

    HALF = D_MODEL // 2

    def layer_params(l, g_in):
        w_main, w_small = _main_cols(jnp.concatenate([cat_last(t) for t in g_in], axis=0))
        return dict(
            norm_w=a["norm_w"][l][None], w_main=w_main, w_small=w_small,
            fb_row=_row128(a["fg_bias"][l], 0), dtb_row=_row128(a["dt_bias"][l], HL),
            alog_row=_row128(a["a_log"][l], HL), dskip_row=_row128(a["d_skip"][l], HL),
            ssm_conv_w=ssm_cw[l], ssm_conv_b=a["ssm_conv_b"][l][None], ssm_norm_w=a["ssm_norm_w"][l][None],
            sc_conv_w=sc_cw[l], sc_conv_b=a["sc_conv_b"][l][None],
            cf_conv_w=cf_cw[l], cf_conv_b=a["cf_conv_b"][l][None],
            cf_ln_w=a["cf_ln_w"][l][None], cf_ln_b=a["cf_ln_b"][l][None],
            b_gate=[b_gate_full[l, i][None] for i in range(4)])

    def in_halves(t):
        return [t[..., :HALF, :], t[..., HALF:, :]]

    def in_slabs(g):
        d_in = _orig_cols(g["w_main"], g["w_small"])
        return in_halves(jnp.moveaxis(d_in.reshape(D_MODEL, N_DEV, -1), 1, 0))

    g_in = _allgather(in_halves(shards(0)[0]), name="gather_in_l0")
    xl, saved, layers = x0, [], []
    for l in range(NL):
        layers.append(layer_params(l, g_in))
        nxt = in_halves(shards(l + 1)[0]) if l + 1 < NL else None
        xl, s, g_in = _layer_fwd(xl, layers[l], f"l{l}", shards(l)[1:], nxt)
        saved.append(s)
    loss_part, dx, d_fnw = _loss_head(xl, a["final_norm_w"][None], tgt, name="loss_head")

    grads, got_own, got_in = [None] * NL, [None] * NL, [None] * NL
    for l in reversed(range(NL)):
        prev = in_slabs(grads[l + 1]) if l + 1 < NL else None
        dx, grads[l], got_own[l], res = _layer_bwd(dx, layers[l], saved[l], f"l{l}", prev)
        if prev is not None:
            got_in[l + 1] = res
    got_in[0] = _exchange(in_slabs(grads[0]), name="exchange_in_l0")

    def stack(key):
        return jnp.stack([grads[l][key] for l in range(NL)])

    big_res = [[], [], [], []]
    for j, n in enumerate(_BIG):
        parts = ([h for l in range(NL) for h in got_in[l]] if j == 0 else [got_own[l][j - 1] for l in range(NL)])
        outs = _adamw(parts, rows2d(a[n]), rows2d(mom["m_" + n]), rows2d(mom["v_" + n]), name=f"adamw_{n}")
        for k in range(4):
            big_res[k].append(outs[k].reshape(a[n].shape))

    def vec(row, off, n):
        return row[0, off:off + n]

    small_grads = {
        "norm_w": jnp.concatenate([grads[l]["norm_w"] for l in range(NL)]),
        "fg_bias": jnp.stack([vec(grads[l]["fb_row"], 0, FOX_HEADS) for l in range(NL)]),
        "ssm_conv_b": jnp.concatenate([grads[l]["ssm_conv_b"] for l in range(NL)]),
        "dt_bias": jnp.stack([vec(grads[l]["dtb_row"], HL, SSM_HEADS) for l in range(NL)]),
        "a_log": jnp.stack([vec(grads[l]["alog_row"], HL, SSM_HEADS) for l in range(NL)]),
        "d_skip": jnp.stack([vec(grads[l]["dskip_row"], HL, SSM_HEADS) for l in range(NL)]),
        "ssm_norm_w": jnp.concatenate([grads[l]["ssm_norm_w"] for l in range(NL)]),
        "sc_conv_b": jnp.concatenate([grads[l]["sc_conv_b"] for l in range(NL)]),
        "cf_conv_b": jnp.concatenate([grads[l]["cf_conv_b"] for l in range(NL)]),
        "cf_ln_w": jnp.concatenate([grads[l]["cf_ln_w"] for l in range(NL)]),
        "cf_ln_b": jnp.concatenate([grads[l]["cf_ln_b"] for l in range(NL)]),
        "final_norm_w": d_fnw[0],
        "ssm_conv_w": stack("ssm_conv_w"), "sc_conv_w": stack("sc_conv_w"), "cf_conv_w": stack("cf_conv_w"),
        "b_gate": jnp.stack([jnp.concatenate(grads[l]["b_gate"]) for l in range(NL)]),
    }
    order = _REPL + _SHSM
    full_shapes = [small_grads[n].shape for n in order] + [(1,)]
    (sg_all,) = _allgather([_pack([small_grads[n] for n in order] + [loss_part.reshape(1)], f32)],
                           name="gather_small_grads")

    def mine(n, full):
        if n in _REPL:
            return full
        sz = a[n].shape[-1]
        return lax.dynamic_slice_in_dim(full, me * sz, sz, axis=full.ndim - 1)

    g_full = _unpack(_sum_slabs(sg_all, name="sum_small"), full_shapes)
    loss = g_full[-1][0]
    g_mine = {n: mine(n, gf).reshape(a[n].shape) for n, gf in zip(order, g_full[:-1])}
    gs1 = _pack([g_mine[n] for n in order], f32)
    ws1 = _pack([a[n] for n in order], f32)
    ms1 = _pack([mom["m_" + n] for n in order], f32)
    vs1 = _pack([mom["v_" + n] for n in order], f32)
    sm_out = _adamw([gs1[None]], ws1, ms1, vs1, name="adamw_small")
    sm_res = [_unpack(o, [a[n].shape for n in order]) for o in sm_out]

    res = []
    for k in range(4):
        d = dict(zip(_BIG, big_res[k]))
        d.update(zip(order, sm_res[k]))
        res.append(d)
    outs = [loss, dx[None]]
    for k in range(4):
        outs += [res[k][n] for n in wnames]
    return tuple(outs)
```

```python
import functools

import jax
import jax.numpy as jnp
from jax import lax
from jax.experimental import pallas as pl
from jax.experimental.pallas import tpu as pltpu

f32 = jnp.float32
bf16 = jnp.bfloat16
SDS = jax.ShapeDtypeStruct

N_DEV = 8
LANES = 128
VMEM_LIMIT = 48 * 1024 * 1024
EPS = 1e-6
NEG = -1e30

D_MODEL = 2048
FOX_HEADS = 8
FOX_DH = 128
SSM_HEADS = 16
SSM_P = 64
SSM_N = 128
SSM_G = 2
SSM_W = 1024
XBC_W = 1536
SSM_K, SC_K, CF_K = 4, 3, 31
BW = 1024
SSD_L = 512
HL = 8
CONV_HALO = 32
IN_SPLIT = (256, 1472)

ADAM_LR, ADAM_B1, ADAM_B2, ADAM_EPS, ADAM_WD, ADAM_STEP = 0.001, 0.9, 0.999, 1e-08, 0.01, 10

IN_NAMES = ("q", "k", "v", "f", "ga", "z", "xbc", "dt", "scb", "scc", "scx", "gc", "glua", "glug", "gd")
IN_SIZES = (1024, 1024, 1024, 8, 1024, 1024, 1536, 16, 1024, 1024, 1024, 1024, 1024, 1024, 1024)
N_IN = sum(IN_SIZES)
MAIN_NAMES = ("q", "k", "v", "ga", "z", "scb", "scc", "scx", "gc", "glua", "glug", "gd", "xbc")
MAIN_OFF = {n: 1024 * i for i, n in enumerate(MAIN_NAMES)}
N_MAIN = 12 * 1024 + XBC_W


def _pcall(body, **kw):
    return pl.pallas_call(body, **kw)


def _cp(*sem):
    return pltpu.CompilerParams(dimension_semantics=sem if sem else None, vmem_limit_bytes=VMEM_LIMIT)


def _pick(n, cands):
    for c in cands:
        if c <= n and n % c == 0:
            return c
    return n


_DIMS = {"nn": ((1,), (0,)), "nt": ((1,), (1,)), "tn": ((0,), (0,))}
_MESH = pl.DeviceIdType.MESH
_HBM = pl.BlockSpec(memory_space=pltpu.HBM)


def _side_specs(side):
    kind, arrs = side
    n = len(arrs)
    shapes = [SDS(((N_DEV,) + a.shape) if kind == "gather" else a.shape, a.dtype) for a in arrs]
    sems = [pltpu.SemaphoreType.DMA((7 * n,)), pltpu.SemaphoreType.DMA((7 * n,)), pltpu.SemaphoreType.DMA((n,))]
    return [_HBM] * n, [_HBM] * n, shapes, sems


def _side_copies(kind, in_refs, out_refs, send_sems, recv_sems, local_sems):
    x, y, c = lax.axis_index("x"), lax.axis_index("y"), lax.axis_index("c")
    me = 4 * x + 2 * y + c
    sends, recvs = [], []
    for a in range(len(in_refs)):
        src_mine = in_refs[a] if kind == "gather" else in_refs[a].at[me]
        sends.append(pltpu.make_async_copy(src_mine, out_refs[a].at[me], local_sems.at[a]))
    for k in range(1, N_DEV):
        px, py, pc = x ^ ((k >> 2) & 1), y ^ ((k >> 1) & 1), c ^ (k & 1)
        peer = 4 * px + 2 * py + pc
        for a in range(len(in_refs)):
            src = in_refs[a] if kind == "gather" else in_refs[a].at[peer]

            def copy(src_ref, dst_slot):
                return pltpu.make_async_remote_copy(
                    src_ref=src_ref, dst_ref=out_refs[a].at[dst_slot],
                    send_sem=send_sems.at[7 * a + k - 1], recv_sem=recv_sems.at[7 * a + k - 1],
                    device_id=(px, py, pc), device_id_type=_MESH)

            sends.append(copy(src, me))
            recvs.append(copy(src, peer))
    return sends, recvs


def _side_run(side, refs, first, last):
    n = len(side[1])
    sends, recvs = _side_copies(side[0], refs[:n], refs[n:2 * n], *refs[2 * n:])

    @pl.when(first)
    def _():
        for cp in sends:
            cp.start()

    @pl.when(last)
    def _():
        for cp in recvs:
            cp.wait_recv()
        for cp in sends[n:]:
            cp.wait_send()
        for cp in sends[:n]:
            cp.wait()


def _mm(a, b, *, mode, out_dtype, name, add=None, tm=512, tn=None, tk=None, side=None):
    if mode == "nn":
        (M, K), (_, N) = a.shape, b.shape
    elif mode == "nt":
        (M, K), (N, _) = a.shape, b.shape
    else:
        (K, M), (_, N) = a.shape, b.shape
    tm = _pick(M, (tm, 256, 128))
    tn = _pick(N, (tn,) if tn else (1536, 1024, 512, 256, 128))
    tk = _pick(K, (tk,) if tk else (2048, 1536, 1024, 512, 256, 128))
    nk = K // tk
    a_spec = (pl.BlockSpec((tk, tm), lambda j, i, k: (k, i)) if mode == "tn"
              else pl.BlockSpec((tm, tk), lambda j, i, k: (i, k)))
    b_spec = (pl.BlockSpec((tn, tk), lambda j, i, k: (j, k)) if mode == "nt"
              else pl.BlockSpec((tk, tn), lambda j, i, k: (k, j)))
    o_spec = pl.BlockSpec((tm, tn), lambda j, i, k: (i, j))
    ins, specs = [a, b], [a_spec, b_spec]
    if add is not None:
        ins.append(add)
        specs.append(o_spec)
    n_in = len(ins)
    dims = (_DIMS[mode], ((), ()))
    out_specs, out_shape = [o_spec], [SDS((M, N), out_dtype)]
    scratch = [pltpu.VMEM((tm, tn), f32)] if nk > 1 else []
    ns = 0
    if side is not None:
        ns = len(side[1])
        s_in, s_out, s_shapes, s_sems = _side_specs(side)
        ins, specs = ins + list(side[1]), specs + s_in
        out_specs, out_shape, scratch = out_specs + s_out, out_shape + s_shapes, scratch + s_sems
    grid = (N // tn, M // tm, nk)

    def body(*refs):
        a_ref, b_ref = refs[0], refs[1]
        add_ref = refs[2] if add is not None else None
        o_ref = refs[n_in + ns]
        rest = refs[n_in + 2 * ns + 1:]
        jj, ii, k = pl.program_id(0), pl.program_id(1), pl.program_id(2)
        if side is not None:
            side_refs = refs[n_in:n_in + ns] + refs[n_in + ns + 1:n_in + 2 * ns + 1] + rest[-3:]
            _side_run(side, side_refs, (jj == 0) & (ii == 0) & (k == 0),
                      (jj == grid[0] - 1) & (ii == grid[1] - 1) & (k == nk - 1))
        p = lax.dot_general(a_ref[...], b_ref[...], dims, preferred_element_type=f32)

        def finish(v):
            if add_ref is not None:
                v = v + add_ref[...].astype(f32)
            o_ref[...] = v.astype(o_ref.dtype)

        if nk == 1:
            finish(p)
        else:
            acc = rest[0]

            @pl.when(k == 0)
            def _():
                acc[...] = p

            @pl.when(k > 0)
            def _():
                acc[...] += p

            @pl.when(k == nk - 1)
            def _():
                finish(acc[...])

    res = _pcall(
        body, name=name, grid=grid, in_specs=specs, out_specs=out_specs, out_shape=out_shape,
        scratch_shapes=scratch,
        compiler_params=_cp("parallel", "parallel", "arbitrary") if side is None else _cp(*(("arbitrary",) * 3)),
    )(*ins)
    return res[0] if side is None else res


def _silu(x):
    return x * jax.nn.sigmoid(x)


def _softplus(x):
    return jnp.maximum(x, 0.0) + jnp.log(1.0 + jnp.exp(-jnp.abs(x)))


def _mean(x):
    return jnp.mean(x, axis=-1, keepdims=True)


def _f_rms(x, w):
    return (x * lax.rsqrt(_mean(x * x) + EPS) * w,)


def _f_rms_res(x, w):
    return _f_rms(x, w) + (x,)


def _f_ya(o, g):
    return (o * _silu(g),)


def _f_yb(y, z, w):
    t = y * _silu(z)
    return (t * lax.rsqrt(_mean(t * t) + EPS) * w,)


def _f_mul(a, b):
    return (a * b,)


def _f_yc(b, cv, g):
    return (b * cv * _silu(g),)


def _f_glu(a, g):
    return (a * jax.nn.sigmoid(g),)


def _f_yd(cf, g, lw, lb):
    mu = _mean(cf)
    var = _mean(jnp.square(cf - mu))
    ln = (cf - mu) * lax.rsqrt(var + EPS) * lw + lb
    return (_silu(ln) * _silu(g),)


def _f_silu(x):
    return (_silu(x),)


def _f_merge(g0, g1, g2, g3, p0, p1, p2, p3, b0, b1, b2, b3):
    sg = jax.nn.sigmoid
    return (sg(g0 + b0) * p0 + sg(g1 + b1) * p1 + sg(g2 + b2) * p2 + sg(g3 + b3) * p3,)


def _row_specs(ins, params, tt, cw):
    specs = [pl.BlockSpec((tt, cw), functools.partial(lambda j, i, base: (i, base + j), base=base))
             for _, base in ins]
    specs += [pl.BlockSpec((p.shape[0], cw), lambda j, i: (0, j)) for p in params]
    return specs


def _rowwise(fn, ins, params, out_dtypes, *, cw, ncol, name, tt=256):
    T = ins[0][0].shape[0]
    tt = _pick(T, (tt, 128))
    ni, npar = len(ins), len(params)

    def body(*refs):
        xs = [r[...].astype(f32) for r in refs[:ni]]
        ps = [r[...] for r in refs[ni:ni + npar]]
        ys = fn(*xs, *ps)
        for o_ref, y in zip(refs[ni + npar:], ys):
            o_ref[...] = y.astype(o_ref.dtype)

    out_spec = pl.BlockSpec((tt, cw), lambda j, i: (i, j))
    outs = _pcall(
        body, name=name, grid=(ncol, T // tt), in_specs=_row_specs(ins, params, tt, cw),
        out_specs=[out_spec] * len(out_dtypes),
        out_shape=[SDS((T, cw * ncol), dt) for dt in out_dtypes],
        compiler_params=_cp("parallel", "parallel"),
    )(*[a for a, _ in ins], *params)
    return outs


def _rowwise_bwd(fn, ins, params, cots, din_dtypes, *, cw, ncol, name, tt=256):
    T = ins[0][0].shape[0]
    tt = _pick(T, (tt, 128))
    ni, npar, nc = len(ins), len(params), len(cots)
    keep = [k for k, dt in enumerate(din_dtypes) if dt is not None]

    def body(*refs):
        xs = [r[...].astype(f32) for r in refs[:ni]]
        ps = [r[...] for r in refs[ni:ni + npar]]
        cs = tuple(r[...].astype(f32) for r in refs[ni + npar:ni + npar + nc])
        outs = refs[ni + npar + nc:]
        _, vjp = jax.vjp(fn, *xs, *ps)
        g = vjp(cs)
        for o_ref, k in zip(outs[:len(keep)], keep):
            o_ref[...] = g[k].astype(o_ref.dtype)
        i = pl.program_id(1)
        for o_ref, gp in zip(outs[len(keep):], g[ni:]):
            @pl.when(i == 0)
            def _(o_ref=o_ref, gp=gp):
                o_ref[...] = gp

            @pl.when(i > 0)
            def _(o_ref=o_ref, gp=gp):
                o_ref[...] += gp

    row_spec = pl.BlockSpec((tt, cw), lambda j, i: (i, j))
    in_specs = _row_specs(ins, params, tt, cw) + [row_spec] * nc
    out_specs = [row_spec] * len(keep) + [pl.BlockSpec((p.shape[0], cw), lambda j, i: (0, j)) for p in params]
    out_shape = [SDS((T, cw * ncol), din_dtypes[k]) for k in keep] + [SDS(p.shape, f32) for p in params]
    return _pcall(
        body, name=name, grid=(ncol, T // tt), in_specs=in_specs, out_specs=out_specs, out_shape=out_shape,
        compiler_params=_cp("arbitrary", "arbitrary"),
    )(*[a for a, _ in ins], *params, *cots)


CONV_RB = 64
SUBLANES = 8


def _shifted_rows(buf, n):
    for sft in range(1, SUBLANES):
        buf[sft, 0:n, :] = buf[0, pl.ds(sft, n), :]


def _tap(buf, o, rb, shifted):
    if shifted:
        return buf[o % SUBLANES, o - o % SUBLANES:o - o % SUBLANES + rb, :]
    return buf[0, pl.ds(o, rb), :]


def _conv_fwd(x, xbase, C, w, b, *, with_silu, name, out_dtype=bf16, tt=512, cw=512):
    xa = x
    T = xa.shape[0]
    K = w.shape[0]
    tt = _pick(T, (tt,))
    hp = CONV_HALO
    r = tt // hp
    ncol = C // cw
    rb = min(CONV_RB, tt)
    n = hp + tt
    shifted = K > SUBLANES

    def body(cur_ref, halo_ref, w_ref, b_ref, *rest):
        outs, ext = rest[:-1], rest[-1]
        i = pl.program_id(1)
        ext[0, 0:hp, :] = jnp.where(i > 0, halo_ref[...].astype(f32), 0.0)
        ext[0, hp:n, :] = cur_ref[...].astype(f32)
        ext[0, n:, :] = jnp.zeros((SUBLANES, cw), f32)
        if shifted:
            _shifted_rows(ext, n)
        for r0 in range(0, tt, rb):
            acc = jnp.zeros((rb, cw), f32) + b_ref[...]
            for k in range(K):
                acc = acc + w_ref[k:k + 1, :] * _tap(ext, hp - (K - 1) + k + r0, rb, shifted)
            outs[0][r0:r0 + rb, :] = acc.astype(outs[0].dtype)
            if with_silu:
                outs[1][r0:r0 + rb, :] = _silu(acc).astype(outs[1].dtype)

    n_out = 2 if with_silu else 1
    o_spec = pl.BlockSpec((tt, cw), lambda j, i: (i, j))
    return _pcall(
        body, name=name, grid=(ncol, T // tt),
        in_specs=[pl.BlockSpec((tt, cw), lambda j, i: (i, xbase + j)),
                  pl.BlockSpec((hp, cw), lambda j, i: (jnp.maximum(i * r - 1, 0), xbase + j)),
                  pl.BlockSpec((K, cw), lambda j, i: (0, j)),
                  pl.BlockSpec((1, cw), lambda j, i: (0, j))],
        out_specs=[o_spec] * n_out, out_shape=[SDS((T, C), out_dtype)] * n_out,
        scratch_shapes=[pltpu.VMEM((SUBLANES if shifted else 1, n + SUBLANES, cw), f32)],
        compiler_params=_cp("parallel", "parallel"),
    )(xa, xa, w, b)


def _conv_bwd(dy, x, xbase, C, w, *, name, dx_dtype=bf16, tt=512, cw=256):
    T = dy.shape[0]
    K = w.shape[0]
    tt = _pick(T, (tt,))
    hp = CONV_HALO
    r = tt // hp
    nt = T // tt
    ncol = C // cw
    rb = min(CONV_RB, tt)
    n = tt + hp
    shifted = K > SUBLANES

    def fold8(v):
        out = v[0:SUBLANES]
        for g in range(1, rb // SUBLANES):
            out = out + v[SUBLANES * g:SUBLANES * (g + 1)]
        return out

    def body(dy_ref, dyn_ref, x_ref, xh_ref, w_ref, dx_ref, dw_ref, db_ref, exd, exx, dwp):
        i = pl.program_id(1)
        exd[0, 0:tt, :] = dy_ref[...].astype(f32)
        exd[0, tt:n, :] = jnp.where(i < nt - 1, dyn_ref[...].astype(f32), 0.0)
        exd[0, n:, :] = jnp.zeros((SUBLANES, cw), f32)
        exx[0, 0:hp, :] = jnp.where(i > 0, xh_ref[...].astype(f32), 0.0)
        exx[0, hp:n, :] = x_ref[...].astype(f32)
        exx[0, n:, :] = jnp.zeros((SUBLANES, cw), f32)
        if shifted:
            _shifted_rows(exd, n)
            _shifted_rows(exx, n)

        @pl.when(i == 0)
        def _():
            dwp[...] = jnp.zeros_like(dwp)

        for r0 in range(0, tt, rb):
            dyc = exd[0, r0:r0 + rb, :]
            acc = jnp.zeros((rb, cw), f32)
            for k in range(K):
                acc = acc + w_ref[k:k + 1, :] * _tap(exd, K - 1 - k + r0, rb, shifted)
                dwp[SUBLANES * k:SUBLANES * (k + 1), :] += fold8(dyc * _tap(exx, hp - (K - 1) + k + r0, rb, shifted))
            dx_ref[r0:r0 + rb, :] = acc.astype(dx_ref.dtype)
            dwp[SUBLANES * K:SUBLANES * (K + 1), :] += fold8(dyc)

        @pl.when(i == nt - 1)
        def _():
            for k in range(K):
                dw_ref[k:k + 1, :] = jnp.sum(dwp[SUBLANES * k:SUBLANES * (k + 1), :], axis=0, keepdims=True)
            db_ref[...] = jnp.sum(dwp[SUBLANES * K:SUBLANES * (K + 1), :], axis=0, keepdims=True)

    nsh = SUBLANES if shifted else 1
    return _pcall(
        body, name=name, grid=(ncol, nt),
        in_specs=[pl.BlockSpec((tt, cw), lambda j, i: (i, j)),
                  pl.BlockSpec((hp, cw), lambda j, i: (jnp.minimum((i + 1) * r, nt * r - 1), j)),
                  pl.BlockSpec((tt, cw), lambda j, i: (i, xbase + j)),
                  pl.BlockSpec((hp, cw), lambda j, i: (jnp.maximum(i * r - 1, 0), xbase + j)),
                  pl.BlockSpec((K, cw), lambda j, i: (0, j))],
        out_specs=[pl.BlockSpec((tt, cw), lambda j, i: (i, j)),
                   pl.BlockSpec((K, cw), lambda j, i: (0, j)),
                   pl.BlockSpec((1, cw), lambda j, i: (0, j))],
        out_shape=[SDS((T, C), dx_dtype), SDS((K, C), f32), SDS((1, C), f32)],
        scratch_shapes=[pltpu.VMEM((nsh, n + SUBLANES, cw), f32), pltpu.VMEM((nsh, n + SUBLANES, cw), f32),
                        pltpu.VMEM((SUBLANES * (K + 1), cw), f32)],
        compiler_params=_cp("arbitrary", "arbitrary"),
    )(dy, dy, x, x, w)


def _tri(n, lower):
    r = lax.broadcasted_iota(jnp.int32, (n, n), 0)
    c = lax.broadcasted_iota(jnp.int32, (n, n), 1)
    return jnp.where((r >= c) if lower else (r <= c), 1.0, 0.0).astype(f32)


def _hdot(a, b):
    return jnp.dot(a, b, precision=lax.Precision.HIGHEST, preferred_element_type=f32)


def _fox_gate_fwd(small, fb_row, *, name, tt=512):
    T = small.shape[0]
    tt = _pick(T, (tt,))

    def body(sm_ref, fb_ref, ck_ref, carry):
        @pl.when(pl.program_id(0) == 0)
        def _():
            carry[...] = jnp.zeros_like(carry)

        logf = -_softplus(-(sm_ref[...] + fb_ref[...]))
        cs = _hdot(_tri(tt, True), logf) + carry[...]
        c2 = cs * LOG2E
        for h in range(FOX_HEADS):
            ck_ref[h] = c2[:, h:h + 1]
        carry[...] = cs[tt - 1:tt, :]

    blk = pl.BlockSpec((tt, LANES), lambda i: (i, 0))
    return _pcall(
        body, name=name, grid=(T // tt,), in_specs=[blk, pl.BlockSpec((1, LANES), lambda i: (0, 0))],
        out_specs=pl.BlockSpec((FOX_HEADS, tt, 1), lambda i: (0, i, 0)),
        out_shape=SDS((FOX_HEADS, T, 1), f32), scratch_shapes=[pltpu.VMEM((1, LANES), f32)],
        compiler_params=_cp("arbitrary"),
    )(small, fb_row)


def _fox_gate_bwd(dc, small, fb_row, dsmall_dt, *, name, tt=512):
    T = small.shape[0]
    tt = _pick(T, (tt,))
    nt = T // tt

    def body(dc_ref, sm_ref, fb_ref, dd_ref, ds_ref, dfb_ref, carry):
        @pl.when(pl.program_id(0) == 0)
        def _():
            carry[...] = jnp.zeros_like(carry)
            dfb_ref[...] = jnp.zeros_like(dfb_ref)

        dl = _hdot(_tri(tt, False), dc_ref[...]) + carry[...]
        carry[...] = dl[0:1, :]
        lane = lax.broadcasted_iota(jnp.int32, (tt, LANES), 1)
        df = jnp.where(lane < FOX_HEADS, dl * jax.nn.sigmoid(-(sm_ref[...] + fb_ref[...])), 0.0)
        ds_ref[...] = jnp.where(lane < FOX_HEADS, df, dd_ref[...])
        dfb_ref[...] += jnp.sum(df, axis=0, keepdims=True)

    blk = pl.BlockSpec((tt, LANES), lambda i: (nt - 1 - i, 0))
    row = pl.BlockSpec((1, LANES), lambda i: (0, 0))
    return _pcall(
        body, name=name, grid=(nt,), in_specs=[blk, blk, row, blk], out_specs=[blk, row],
        out_shape=[SDS((T, LANES), f32), SDS((1, LANES), f32)], scratch_shapes=[pltpu.VMEM((1, LANES), f32)],
        compiler_params=_cp("arbitrary"),
    )(dc, small, fb_row, dsmall_dt)


_NT = (((1,), (1,)), ((), ()))
_TN = (((0,), (0,)), ((), ()))


def _causal(n):
    r = lax.broadcasted_iota(jnp.int32, (n, n), 0)
    c = lax.broadcasted_iota(jnp.int32, (n, n), 1)
    return r >= c


LOG2E = 1.4426950408889634
_SCALE = FOX_DH ** -0.5
_SCALE2 = _SCALE * LOG2E
CW = 256
FLASH_TQ = 1024
FLASH_CH = 1024
FLASH_TK = 1024


def _to_t(a, w):
    T = a.shape[0]
    return a.reshape(T // w, w, FOX_HEADS, FOX_DH).transpose(2, 0, 3, 1)


def _from_t(at):
    hh, n, dh, w = at.shape
    return at.transpose(1, 3, 0, 2).reshape(n * w, hh * dh)


def _flash_fwd(u, qT, vT, ck2, *, name, side=None):
    T = u.shape[0]
    H, dh = FOX_HEADS, FOX_DH
    TQ, CH = min(FLASH_TQ, T), min(FLASH_CH, T)
    NC, NQ, RQ, RC, R = T // CW, T // TQ, TQ // CW, CH // CW, TQ // CH
    ko = MAIN_OFF["k"] // dh
    ns = len(side[1]) if side is not None else 0
    s_in, s_out, s_shapes, s_sems = _side_specs(side) if side is not None else ([], [], [], [])

    def body(qT_ref, k_ref, vT_ref, ck_ref, *rest):
        oT_ref, lse_ref = rest[ns], rest[ns + 1]
        i = pl.program_id(1)
        if side is not None:
            hh = pl.program_id(0)
            _side_run(side, rest[:ns] + rest[ns + 2:], (hh == 0) & (i == 0), (hh == H - 1) & (i == NQ - 1))
        qt = jnp.concatenate([qT_ref[r] for r in range(RQ)], axis=1) if RQ > 1 else qT_ref[0]

        def scores(j):
            st = pl.multiple_of(j * CH, CH)
            kc = k_ref[pl.ds(st, CH), :]
            return jnp.dot(kc, qt, preferred_element_type=f32) * _SCALE2 - ck_ref[pl.ds(st, CH), :]

        def update(j, s, carry):
            m, l, acc = carry
            m_new = jnp.maximum(m, jnp.max(s, axis=0, keepdims=True))
            alpha = jnp.exp2(m - m_new)
            p = jnp.exp2(s - m_new)
            l = alpha * l + jnp.sum(p, axis=0, keepdims=True)
            pb = p.astype(bf16)
            pv = jnp.dot(vT_ref[j * RC], pb[0:CW], preferred_element_type=f32)
            for r in range(1, RC):
                pv = pv + jnp.dot(vT_ref[j * RC + r], pb[r * CW:(r + 1) * CW], preferred_element_type=f32)
            return m_new, l, alpha * acc + pv

        def band(s, d):
            r = lax.broadcasted_iota(jnp.int32, (CH, TQ), 0) + d * CH
            c = lax.broadcasted_iota(jnp.int32, (CH, TQ), 1)
            return jnp.where(r <= c, s, NEG)

        carry = (jnp.full((1, TQ), NEG, f32), jnp.zeros((1, TQ), f32), jnp.zeros((dh, TQ), f32))
        carry = lax.fori_loop(0, i * R, lambda j, c: update(j, scores(j), c), carry)
        for d in range(R):
            carry = update(i * R + d, band(scores(i * R + d), d), carry)
        m, l, acc = carry
        o = (acc / l).astype(oT_ref.dtype)
        lse = m + jnp.log2(l)
        for r in range(RQ):
            oT_ref[r] = o[:, r * CW:(r + 1) * CW]
            lse_ref[r] = lse[:, r * CW:(r + 1) * CW]

    return _pcall(
        body, name=name, grid=(H, NQ),
        in_specs=[pl.BlockSpec((None, RQ, dh, CW), lambda h, i: (h, i, 0, 0)),
                  pl.BlockSpec((T, dh), lambda h, i: (0, ko + h)),
                  pl.BlockSpec((None, NC, dh, CW), lambda h, i: (h, 0, 0, 0)),
                  pl.BlockSpec((None, T, 1), lambda h, i: (h, 0, 0))] + s_in,
        out_specs=[pl.BlockSpec((None, RQ, dh, CW), lambda h, i: (h, i, 0, 0)),
                   pl.BlockSpec((None, RQ, 1, CW), lambda h, i: (h, i, 0, 0))] + s_out,
        out_shape=[SDS((H, NC, dh, CW), bf16), SDS((H, NC, 1, CW), f32)] + s_shapes,
        scratch_shapes=s_sems,
        compiler_params=_cp("parallel", "arbitrary") if side is None else _cp("arbitrary", "arbitrary"),
    )(qT, u, vT, ck2, *(side[1] if side is not None else []))


def _flash_delta(doT, oT, *, name):
    H, NC, dh, _ = doT.shape

    def body(a_ref, b_ref, d_ref):
        for i in range(NC):
            d_ref[i] = jnp.sum(a_ref[i].astype(f32) * b_ref[i].astype(f32), axis=0, keepdims=True)

    blk = pl.BlockSpec((None, NC, dh, CW), lambda h: (h, 0, 0, 0))
    return _pcall(
        body, name=name, grid=(H,), in_specs=[blk, blk],
        out_specs=pl.BlockSpec((None, NC, 1, CW), lambda h: (h, 0, 0, 0)),
        out_shape=SDS((H, NC, 1, CW), f32), compiler_params=_cp("parallel"),
    )(doT, oT)


def _flash_bwd(u, qT, kT, do, doT, ck2, lse2, delta, *, name, side=None):
    T = u.shape[0]
    H, dh = FOX_HEADS, FOX_DH
    TK = min(FLASH_TK, T)
    NC, NK, R = T // CW, T // TK, TK // CW
    qo, ko, vo = MAIN_OFF["q"] // dh, MAIN_OFF["k"] // dh, MAIN_OFF["v"] // dh
    ns = len(side[1]) if side is not None else 0
    s_in, s_out, s_shapes, s_sems = _side_specs(side) if side is not None else ([], [], [], [])

    def body(q_ref, k_ref, v_ref, qT_ref, kT_ref, do_ref, doT_ref, ck_ref, lse_ref, dl_ref, *rest):
        dk_ref, dv_ref, dck_ref, dqT_ref, dcq_ref = rest[ns:ns + 5]
        dk_s, dv_s, dc_s = rest[2 * ns + 5:2 * ns + 8]
        j = pl.program_id(1)
        if side is not None:
            hh = pl.program_id(0)
            _side_run(side, rest[:ns] + rest[ns + 5:2 * ns + 5] + rest[2 * ns + 8:],
                      (hh == 0) & (j == 0), (hh == H - 1) & (j == NK - 1))

        @pl.when(j == 0)
        def _():
            dqT_ref[...] = jnp.zeros_like(dqT_ref)
            dcq_ref[...] = jnp.zeros_like(dcq_ref)

        dk_s[...] = jnp.zeros_like(dk_s)
        dv_s[...] = jnp.zeros_like(dv_s)
        dc_s[...] = jnp.zeros_like(dc_s)
        k, v, kt = k_ref[...], v_ref[...], kT_ref[...]
        ckc = ck_ref[...]

        def chunk(i, d):
            st = pl.multiple_of(i * CW, CW)
            s = jnp.dot(k, qT_ref[i], preferred_element_type=f32) * _SCALE2 - ckc
            if d is not None:
                r = lax.broadcasted_iota(jnp.int32, (TK, CW), 0)
                c = lax.broadcasted_iota(jnp.int32, (TK, CW), 1) + d * CW
                s = jnp.where(r <= c, s, NEG)
            p = jnp.exp2(s - lse_ref[i])
            dp = jnp.dot(v, doT_ref[i], preferred_element_type=f32)
            ds = p * (dp - dl_ref[i])
            dsb = ds.astype(bf16)
            dv_s[...] += jnp.dot(p.astype(bf16), do_ref[pl.ds(st, CW), :], preferred_element_type=f32)
            dk_s[...] += jnp.dot(dsb, q_ref[pl.ds(st, CW), :], preferred_element_type=f32)
            dqT_ref[i] += jnp.dot(kt, dsb, preferred_element_type=f32)
            dcq_ref[i] += jnp.sum(ds, axis=0, keepdims=True)
            part = ds[:, 0:LANES]
            for t in range(1, CW // LANES):
                part = part + ds[:, t * LANES:(t + 1) * LANES]
            dc_s[...] += part

        for d in range(R):
            chunk(j * R + d, d)

        def rest(i, c):
            chunk(i, None)
            return c

        lax.fori_loop(j * R + R, NC, rest, 0)
        dk_ref[...] = (dk_s[...] * _SCALE).astype(dk_ref.dtype)
        dv_ref[...] = dv_s[...].astype(dv_ref.dtype)
        dck_ref[...] = -jnp.sum(dc_s[...], axis=1, keepdims=True)

        @pl.when(j == NK - 1)
        def _():
            dqT_ref[...] = dqT_ref[...] * _SCALE

    res = lambda h, j: (h, 0, 0, 0)
    return _pcall(
        body, name=name, grid=(H, NK),
        in_specs=[pl.BlockSpec((T, dh), lambda h, j: (0, qo + h)),
                  pl.BlockSpec((TK, dh), lambda h, j: (j, ko + h)),
                  pl.BlockSpec((TK, dh), lambda h, j: (j, vo + h)),
                  pl.BlockSpec((None, NC, dh, CW), res),
                  pl.BlockSpec((None, None, dh, TK), lambda h, j: (h, j, 0, 0)),
                  pl.BlockSpec((T, dh), lambda h, j: (0, h)),
                  pl.BlockSpec((None, NC, dh, CW), res),
                  pl.BlockSpec((None, TK, 1), lambda h, j: (h, j, 0)),
                  pl.BlockSpec((None, NC, 1, CW), res),
                  pl.BlockSpec((None, NC, 1, CW), res)] + s_in,
        out_specs=[pl.BlockSpec((TK, dh), lambda h, j: (j, h)),
                   pl.BlockSpec((TK, dh), lambda h, j: (j, h)),
                   pl.BlockSpec((None, TK, 1), lambda h, j: (h, j, 0)),
                   pl.BlockSpec((None, NC, dh, CW), res),
                   pl.BlockSpec((None, NC, 1, CW), res)] + s_out,
        out_shape=[SDS((T, H * dh), bf16), SDS((T, H * dh), bf16), SDS((H, T, 1), f32),
                   SDS((H, NC, dh, CW), f32), SDS((H, NC, 1, CW), f32)] + s_shapes,
        scratch_shapes=[pltpu.VMEM((TK, dh), f32), pltpu.VMEM((TK, dh), f32),
                        pltpu.VMEM((TK, LANES), f32)] + s_sems,
        compiler_params=_cp("parallel", "arbitrary") if side is None else _cp("arbitrary", "arbitrary"),
    )(u, u, u, qT, kT, do, doT, ck2, lse2, delta, *(side[1] if side is not None else []))


def _ssd_common(sm_ref, dtb_ref, alog_ref):
    L = SSD_L
    dt = _softplus(sm_ref[...] + dtb_ref[...])
    a = -jnp.exp(alog_ref[...])
    cs = _hdot(_tri(L, True), dt * a)
    return dt, a, cs, cs.T, cs[L - 1:L, :]


def _ssd_fwd(xbc, small, dtb_row, alog_row, dskip_row, *, name, side=None):
    T = xbc.shape[0]
    L, P, N = SSD_L, SSM_P, SSM_N
    nc = T // L
    hpg = SSM_HEADS // SSM_G
    ns = len(side[1]) if side is not None else 0
    s_in, s_out, s_shapes, s_sems = _side_specs(side) if side is not None else ([], [], [], [])

    def body(x_ref, sm_ref, dtb_ref, alog_ref, dsk_ref, *rest):
        y_ref, hs_ref, h_s = rest[ns], rest[ns + 1], rest[2 * ns + 2]
        if side is not None:
            step = pl.program_id(0)
            _side_run(side, rest[:ns] + rest[ns + 2:2 * ns + 2] + rest[2 * ns + 3:], step == 0, step == nc - 1)

        @pl.when(pl.program_id(0) == 0)
        def _():
            h_s[...] = jnp.zeros_like(h_s)

        dt, a, cs, csT, tot = _ssd_common(sm_ref, dtb_ref, alog_ref)
        hs_ref[...] = h_s[...]
        causal = _causal(L)
        for g in range(SSM_G):
            bg = x_ref[:, SSM_W + g * N:SSM_W + (g + 1) * N]
            cg = x_ref[:, SSM_W + SSM_G * N + g * N:SSM_W + SSM_G * N + (g + 1) * N]
            cb = lax.dot_general(cg, bg, _NT, preferred_element_type=f32)
            hg = h_s[g * hpg * P:(g + 1) * hpg * P, :]
            yoff = lax.dot_general(cg, hg.astype(bf16), _NT, preferred_element_type=f32)
            for hh in range(hpg):
                h = g * hpg + hh
                ln = HL + h
                acol, arow = cs[:, ln:ln + 1], csT[ln:ln + 1, :]
                w = jnp.where(causal, cb * jnp.exp(jnp.minimum(acol - arow, 0.0)), 0.0)
                xs = x_ref[:, h * P:(h + 1) * P].astype(f32)
                xh = xs * dt[:, ln:ln + 1]
                y = jnp.dot(w.astype(bf16), xh.astype(bf16), preferred_element_type=f32)
                y = y + jnp.exp(acol) * yoff[:, hh * P:(hh + 1) * P] + xs * dsk_ref[:, ln:ln + 1]
                y_ref[:, h * P:(h + 1) * P] = y.astype(y_ref.dtype)
                decay = jnp.exp(tot[:, ln:ln + 1] - acol)
                st = lax.dot_general((xh * decay).astype(bf16), bg, _TN, preferred_element_type=f32)
                h_s[h * P:(h + 1) * P, :] = jnp.exp(tot[:, ln:ln + 1]) * h_s[h * P:(h + 1) * P, :] + st

    row = pl.BlockSpec((1, LANES), lambda c: (0, 0))
    return _pcall(
        body, name=name, grid=(nc,),
        in_specs=[pl.BlockSpec((L, XBC_W), lambda c: (c, 0)), pl.BlockSpec((L, LANES), lambda c: (c, 0)),
                  row, row, row] + s_in,
        out_specs=[pl.BlockSpec((L, SSM_W), lambda c: (c, 0)),
                   pl.BlockSpec((None, SSM_HEADS * P, N), lambda c: (c, 0, 0))] + s_out,
        out_shape=[SDS((T, SSM_W), bf16), SDS((nc, SSM_HEADS * P, N), f32)] + s_shapes,
        scratch_shapes=[pltpu.VMEM((SSM_HEADS * P, N), f32)] + s_sems,
        compiler_params=_cp("arbitrary"),
    )(xbc, small, dtb_row, alog_row, dskip_row, *(side[1] if side is not None else []))


def _ssd_bwd(dy, xbc, small, dtb_row, alog_row, dskip_row, hstates, *, name):
    T = xbc.shape[0]
    L, P, N = SSD_L, SSM_P, SSM_N
    nc = T // L
    hpg = SSM_HEADS // SSM_G
    GW = hpg * P

    def body(dy_ref, x_ref, sm_ref, dtb_ref, alog_ref, dsk_ref, hs_ref,
             dx_ref, dsm_ref, ddtb_ref, dalog_ref, ddsk_ref, dh_s, ea_s, dx_s):
        @pl.when(pl.program_id(0) == 0)
        def _():
            dh_s[...] = jnp.zeros_like(dh_s)
            ddtb_ref[...] = jnp.zeros_like(ddtb_ref)
            dalog_ref[...] = jnp.zeros_like(dalog_ref)
            ddsk_ref[...] = jnp.zeros_like(ddsk_ref)

        dt, a, cs, csT, tot = _ssd_common(sm_ref, dtb_ref, alog_ref)
        causal = _causal(L)
        lane = lax.broadcasted_iota(jnp.int32, (L, LANES), 1)
        sub = lax.broadcasted_iota(jnp.int32, (LANES, L), 0)
        lane1 = lax.broadcasted_iota(jnp.int32, (1, LANES), 1)
        da_col = jnp.zeros((L, LANES), f32)
        da_row = jnp.zeros((LANES, L), f32)
        da_tot = jnp.zeros((1, LANES), f32)
        ddt = jnp.zeros((L, LANES), f32)
        ddsk = jnp.zeros((1, LANES), f32)
        for g in range(SSM_G):
            bo, co = SSM_W + g * N, SSM_W + SSM_G * N + g * N
            bg, cg = x_ref[:, bo:bo + N], x_ref[:, co:co + N]
            cb = lax.dot_general(cg, bg, _NT, preferred_element_type=f32)
            hg = hs_ref[g * GW:(g + 1) * GW, :]
            hgb = hg.astype(bf16)
            dhn = dh_s[g * GW:(g + 1) * GW, :]
            dhnb = dhn.astype(bf16)
            yoff = lax.dot_general(cg, hgb, _NT, preferred_element_type=f32)
            bdh = lax.dot_general(bg, dhnb, _NT, preferred_element_type=f32)
            dcb = jnp.zeros((L, L), f32)
            for hh in range(hpg):
                h = g * hpg + hh
                ln = HL + h
                sl = slice(hh * P, (hh + 1) * P)
                acol, arow = cs[:, ln:ln + 1], csT[ln:ln + 1, :]
                e = jnp.where(causal, jnp.exp(jnp.minimum(acol - arow, 0.0)), 0.0)
                w = cb * e
                xs = x_ref[:, h * P:(h + 1) * P].astype(f32)
                dth = dt[:, ln:ln + 1]
                xh = xs * dth
                dyh = dy_ref[:, h * P:(h + 1) * P].astype(f32)
                dyb = dyh.astype(bf16)
                ea = jnp.exp(acol)
                decay = jnp.exp(tot[:, ln:ln + 1] - acol)
                etot = jnp.exp(tot[:, ln:ln + 1])
                dw = lax.dot_general(dyb, xh.astype(bf16), _NT, preferred_element_type=f32)
                m = dw * w
                dcb = dcb + dw * e
                dxs_state = decay * bdh[:, sl]
                dxh = lax.dot_general(w.astype(bf16), dyb, _TN, preferred_element_type=f32) + dxs_state
                r = jnp.sum(xh * dxs_state, axis=1, keepdims=True)
                col = (jnp.sum(m, axis=1, keepdims=True) - r
                       + jnp.sum(dyh * (ea * yoff[:, sl]), axis=1, keepdims=True))
                da_col = da_col + jnp.where(lane == ln, col, 0.0)
                da_row = da_row + jnp.where(sub == ln, jnp.sum(m, axis=0, keepdims=True), 0.0)
                hprod = jnp.sum(jnp.sum(dhn[sl, :] * hg[sl, :], axis=1, keepdims=True), axis=0, keepdims=True)
                t_h = jnp.sum(r, axis=0, keepdims=True) + etot * hprod
                da_tot = da_tot + jnp.where(lane1 == ln, t_h, 0.0)
                ea_s[:, sl] = (ea * dyh).astype(bf16)
                dx_s[:, sl] = (decay * xh).astype(bf16)
                dx_ref[:, h * P:(h + 1) * P] = (dxh * dth + dyh * dsk_ref[:, ln:ln + 1]).astype(dx_ref.dtype)
                ddt = ddt + jnp.where(lane == ln, jnp.sum(dxh * xs, axis=1, keepdims=True), 0.0)
                ddsk = ddsk + jnp.where(
                    lane1 == ln, jnp.sum(jnp.sum(dyh * xs, axis=1, keepdims=True), axis=0, keepdims=True), 0.0)
                dh_s[g * GW + hh * P:g * GW + (hh + 1) * P, :] = etot * dhn[sl, :]
            dcbb = dcb.astype(bf16)
            eab, dxb = ea_s[...], dx_s[...]
            dcg = (jnp.dot(dcbb, bg, preferred_element_type=f32)
                   + jnp.dot(eab, hgb, preferred_element_type=f32))
            dbg = (lax.dot_general(dcbb, cg, _TN, preferred_element_type=f32)
                   + jnp.dot(dxb, dhnb, preferred_element_type=f32))
            dx_ref[:, bo:bo + N] = dbg.astype(dx_ref.dtype)
            dx_ref[:, co:co + N] = dcg.astype(dx_ref.dtype)
            dh_s[g * GW:(g + 1) * GW, :] += lax.dot_general(eab, cg, _TN, preferred_element_type=f32)

        rowi = lax.broadcasted_iota(jnp.int32, (L, LANES), 0)
        da_cs = da_col - da_row.T + jnp.where(rowi == L - 1, da_tot, 0.0)
        dda = _hdot(_tri(L, False), da_cs)
        ddt = ddt + dda * a
        heads = (lane >= HL) & (lane < HL + SSM_HEADS)
        draw = jnp.where(heads, ddt * jax.nn.sigmoid(sm_ref[...] + dtb_ref[...]), 0.0)
        dsm_ref[...] = draw
        ddtb_ref[...] += jnp.sum(draw, axis=0, keepdims=True)
        dalog_ref[...] += jnp.where(lane1 >= HL, jnp.sum(dda * dt, axis=0, keepdims=True) * a, 0.0)
        ddsk_ref[...] += ddsk

    rev = lambda c: (nc - 1 - c, 0)
    row = pl.BlockSpec((1, LANES), lambda c: (0, 0))
    return _pcall(
        body, name=name, grid=(nc,),
        in_specs=[pl.BlockSpec((L, SSM_W), rev), pl.BlockSpec((L, XBC_W), rev), pl.BlockSpec((L, LANES), rev),
                  row, row, row, pl.BlockSpec((None, SSM_HEADS * P, N), lambda c: (nc - 1 - c, 0, 0))],
        out_specs=[pl.BlockSpec((L, XBC_W), rev), pl.BlockSpec((L, LANES), rev), row, row, row],
        out_shape=[SDS((T, XBC_W), bf16), SDS((T, LANES), f32)] + [SDS((1, LANES), f32)] * 3,
        scratch_shapes=[pltpu.VMEM((SSM_HEADS * P, N), f32), pltpu.VMEM((L, GW), bf16), pltpu.VMEM((L, GW), bf16)],
        compiler_params=_cp("arbitrary"),
    )(dy, xbc, small, dtb_row, alog_row, dskip_row, hstates)


def _loss_head(x, w, target, *, name, tt=256):
    T, D = x.shape
    tt = _pick(T, (tt, 128))

    def fn(x, w, tg):
        (y,) = _f_rms(x, w)
        return 0.5 * jnp.sum(jnp.sum(jnp.square(y - tg), axis=1, keepdims=True), axis=0, keepdims=True) / D

    def body(x_ref, w_ref, t_ref, loss_ref, dx_ref, dw_ref):
        i = pl.program_id(0)
        ls, vjp = jax.vjp(fn, x_ref[...], w_ref[...], t_ref[...])
        dx, dw, _ = vjp(jnp.ones((1, 1), f32))
        dx_ref[...] = dx

        @pl.when(i == 0)
        def _():
            loss_ref[...] = ls
            dw_ref[...] = dw

        @pl.when(i > 0)
        def _():
            loss_ref[...] += ls
            dw_ref[...] += dw

    blk = pl.BlockSpec((tt, D), lambda i: (i, 0))
    row = pl.BlockSpec((1, D), lambda i: (0, 0))
    return _pcall(
        body, name=name, grid=(T // tt,), in_specs=[blk, row, blk],
        out_specs=[pl.BlockSpec((1, 1), lambda i: (0, 0)), blk, row],
        out_shape=[SDS((1, 1), f32), SDS((T, D), f32), SDS((1, D), f32)],
        compiler_params=_cp("arbitrary"),
    )(x, w, target)


ADAM_BLOCK_ELEMS = 256 * 1024


def _row_tile(rows, cols, nbuf=1):
    for t in (2048, 1024, 512, 256, 128, 64, 32, 16, 8):
        if t * cols * max(1, nbuf // 2) <= ADAM_BLOCK_ELEMS and rows % t == 0:
            return t
    return rows


def _adamw(gss, w, m, v, *, name):
    ns, Rp, C = gss[0].shape
    npart = len(gss)
    tr = _row_tile(Rp, C, npart)
    per = Rp // tr
    c1 = 1.0 / (1.0 - ADAM_B1 ** ADAM_STEP)
    c2 = 1.0 / (1.0 - ADAM_B2 ** ADAM_STEP)

    def body(*refs):
        g_refs = refs[:npart]
        w_ref, m_ref, v_ref, go_ref, d_ref, mo_ref, vo_ref = refs[npart:]
        part = pl.program_id(0) // per

        def update(g_ref):
            g = g_ref[0].astype(f32)
            for s in range(1, ns):
                g = g + g_ref[s].astype(f32)
            mn = ADAM_B1 * m_ref[...] + (1.0 - ADAM_B1) * g
            vn = ADAM_B2 * v_ref[...] + (1.0 - ADAM_B2) * (g * g)
            go_ref[...] = g
            mo_ref[...] = mn
            vo_ref[...] = vn
            d_ref[...] = -ADAM_LR * ((mn * c1) / (jnp.sqrt(vn * c2) + ADAM_EPS) + ADAM_WD * w_ref[...])

        for p in range(npart):
            @pl.when(part == p)
            def _(p=p):
                update(g_refs[p])

    blk = pl.BlockSpec((tr, C), lambda i: (i, 0))
    g_specs = [pl.BlockSpec((ns, tr, C), functools.partial(
        lambda i, p: (0, jnp.clip(i - p * per, 0, per - 1), 0), p=p)) for p in range(npart)]
    return _pcall(
        body, name=name, grid=(npart * per,), in_specs=g_specs + [blk, blk, blk],
        out_specs=[blk] * 4, out_shape=[SDS((npart * Rp, C), f32)] * 4,
        compiler_params=_cp("parallel"),
    )(*gss, w, m, v)


def _sum_slabs(gs, *, name):
    ns, R, C = gs.shape
    tr = _row_tile(R, C)

    def body(g_ref, o_ref):
        g = g_ref[0]
        for s in range(1, ns):
            g = g + g_ref[s]
        o_ref[...] = g

    return _pcall(
        body, name=name, grid=(R // tr,), in_specs=[pl.BlockSpec((ns, tr, C), lambda i: (0, i, 0))],
        out_specs=pl.BlockSpec((tr, C), lambda i: (i, 0)), out_shape=SDS((R, C), f32),
        compiler_params=_cp("parallel"),
    )(gs)


def _allgather(arrs, *, name):
    n = len(arrs)

    def body(*refs):
        x_refs, out_refs = refs[:n], refs[n:2 * n]
        send_sems, recv_sems, local_sems = refs[2 * n:]
        x, y, c = lax.axis_index("x"), lax.axis_index("y"), lax.axis_index("c")
        me, sibling = (x, y, c), (x, y, 1 - c)
        chips = [(1 - x, y), (x, 1 - y), (1 - x, 1 - y)]

        def copy(a, k, block, to, src=None):
            px, py, pc = block
            slot = out_refs[a].at[4 * px + 2 * py + pc]
            return pltpu.make_async_remote_copy(
                src_ref=slot if src is None else src, dst_ref=slot,
                send_sem=send_sems.at[7 * a + k], recv_sem=recv_sems.at[7 * a + k],
                device_id=to, device_id_type=_MESH)

        mine, first, passed = [], [], []
        for a in range(n):
            cp = pltpu.make_async_copy(x_refs[a], out_refs[a].at[4 * x + 2 * y + c], local_sems.at[a])
            cp.start()
            mine.append(cp)
            first.append(copy(a, 0, me, sibling, src=x_refs[a]))
            first += [copy(a, 1 + j, me, (*chip, c), src=x_refs[a]) for j, chip in enumerate(chips)]
        for cp in first:
            cp.start()
        for j, chip in enumerate(chips):
            for a in range(n):
                copy(a, 1 + j, (*chip, c), me).wait_recv()
                cp = copy(a, 4 + j, (*chip, c), sibling)
                cp.start()
                passed.append(cp)
        for a in range(n):
            copy(a, 0, sibling, me).wait_recv()
            for j, chip in enumerate(chips):
                copy(a, 4 + j, (*chip, 1 - c), me).wait_recv()
        for cp in first + passed:
            cp.wait_send()
        for cp in mine:
            cp.wait()

    return _pcall(
        body, name=name, in_specs=[_HBM] * n, out_specs=[_HBM] * n,
        out_shape=[SDS((N_DEV,) + a.shape, a.dtype) for a in arrs],
        scratch_shapes=[pltpu.SemaphoreType.DMA((7 * n,)), pltpu.SemaphoreType.DMA((7 * n,)),
                        pltpu.SemaphoreType.DMA((n,))],
    )(*arrs)


def _exchange(arrs, *, name):
    n = len(arrs)

    def body(*refs):
        in_refs, out_refs = refs[:n], refs[n:2 * n]
        send_sems, recv_sems, local_sems = refs[2 * n:]
        x, y, c = lax.axis_index("x"), lax.axis_index("y"), lax.axis_index("c")
        me = 4 * x + 2 * y + c
        mine, sends, recvs = [], [], []
        for a in range(n):
            cp = pltpu.make_async_copy(in_refs[a].at[me], out_refs[a].at[me], local_sems.at[a])
            cp.start()
            mine.append(cp)
        for k in range(1, N_DEV):
            px, py, pc = x ^ ((k >> 2) & 1), y ^ ((k >> 1) & 1), c ^ (k & 1)
            peer = 4 * px + 2 * py + pc
            for a in range(n):
                def copy(src_slot, dst_slot):
                    return pltpu.make_async_remote_copy(
                        src_ref=in_refs[a].at[src_slot], dst_ref=out_refs[a].at[dst_slot],
                        send_sem=send_sems.at[7 * a + k - 1], recv_sem=recv_sems.at[7 * a + k - 1],
                        device_id=(px, py, pc), device_id_type=_MESH)

                sends.append(copy(peer, me))
                recvs.append(copy(me, peer))
                sends[-1].start()
        for cp in recvs:
            cp.wait_recv()
        for cp in sends:
            cp.wait_send()
        for cp in mine:
            cp.wait()

    return _pcall(
        body, name=name, in_specs=[_HBM] * n, out_specs=[_HBM] * n,
        out_shape=[SDS(a.shape, a.dtype) for a in arrs],
        scratch_shapes=[pltpu.SemaphoreType.DMA((7 * n,)), pltpu.SemaphoreType.DMA((7 * n,)),
                        pltpu.SemaphoreType.DMA((n,))],
    )(*arrs)


def _row128(vec, off):
    return jnp.pad(vec.astype(f32), (off, LANES - off - vec.shape[0]))[None, :]


def _cb(name, cw):
    return MAIN_OFF[name] // cw


def _layer_fwd(x, p, tag, shard, nxt):
    s = {"x": x}
    sh_gate, sh_branch, sh_out = shard
    (h,) = _rowwise(_f_rms, [(x, 0)], [p["norm_w"]], [bf16], cw=D_MODEL, ncol=1, name=f"rms_{tag}")
    u, g_gate, g_out, *nxt_0 = _mm(h, p["w_main"], mode="nn", out_dtype=bf16, name=f"in_main_{tag}",
                                   side=("gather", [sh_gate, sh_out] + (nxt[:1] if nxt else [])))
    small = _mm(h, p["w_small"], mode="nn", out_dtype=f32, name=f"in_small_{tag}")
    s.update(h=h, u=u, small=small)
    W = BW
    SH = D_MODEL // N_DEV
    w_gate = jnp.moveaxis(g_gate.reshape(N_DEV, 4, SH, D_MODEL), 0, 1).reshape(4, D_MODEL, D_MODEL)
    p["w_gate"] = [w_gate[i] for i in range(4)]
    p["w_out"] = g_out.reshape(D_MODEL, D_MODEL)

    ck2 = _fox_gate_fwd(small, p["fb_row"], name=f"fox_gate_{tag}")
    qT = _to_t(u[:, MAIN_OFF["q"]:MAIN_OFF["q"] + W], CW)
    vT = _to_t(u[:, MAIN_OFF["v"]:MAIN_OFF["v"] + W], CW)
    oT, lse2, g_branch, *nxt_a = _flash_fwd(u, qT, vT, ck2, name=f"flash_{tag}",
                                            side=("gather", [sh_branch] + (nxt[1:2] if nxt else [])))
    w_branch = jnp.moveaxis(g_branch.reshape(N_DEV, 4, BW, SH), 0, 2).reshape(4, BW, D_MODEL)
    p["w_branch"] = [w_branch[i] for i in range(4)]
    o = _from_t(oT)
    (ya,) = _rowwise(_f_ya, [(o, 0), (u, _cb("ga", W))], [], [bf16], cw=W, ncol=1, name=f"ya_{tag}")
    s.update(ck2=ck2, qT=qT, o=o, oT=oT, lse2=lse2)

    pre_b, xbc_c = _conv_fwd(u, _cb("xbc", 512), XBC_W, p["ssm_conv_w"], p["ssm_conv_b"], with_silu=True,
                             name=f"ssm_conv_{tag}")
    y_ssd, hst, *nxt_b = _ssd_fwd(xbc_c, small, p["dtb_row"], p["alog_row"], p["dskip_row"], name=f"ssd_{tag}",
                                  side=("gather", nxt[2:]) if nxt else None)
    (yb,) = _rowwise(_f_yb, [(y_ssd, 0), (u, _cb("z", W))], [p["ssm_norm_w"]], [bf16], cw=W, ncol=1,
                     name=f"yb_{tag}")
    s.update(pre_b=pre_b, xbc_c=xbc_c, y_ssd=y_ssd, hst=hst)

    (pc,) = _rowwise(_f_mul, [(u, _cb("scc", 512)), (u, _cb("scx", 512))], [], [bf16], cw=512, ncol=2,
                     name=f"sc_pre_{tag}")
    (cv,) = _conv_fwd(pc, 0, W, p["sc_conv_w"], p["sc_conv_b"], with_silu=False, name=f"sc_conv_{tag}")
    (yc,) = _rowwise(_f_yc, [(u, _cb("scb", 512)), (cv, 0), (u, _cb("gc", 512))], [], [bf16], cw=512, ncol=2,
                     name=f"yc_{tag}")
    s.update(pc=pc, cv=cv)

    (pd,) = _rowwise(_f_glu, [(u, _cb("glua", 512)), (u, _cb("glug", 512))], [], [bf16], cw=512, ncol=2,
                     name=f"cf_pre_{tag}")
    (cf,) = _conv_fwd(pd, 0, W, p["cf_conv_w"], p["cf_conv_b"], with_silu=False, name=f"cf_conv_{tag}")
    (yd,) = _rowwise(_f_yd, [(cf, 0), (u, _cb("gd", W))], [p["cf_ln_w"], p["cf_ln_b"]], [bf16], cw=W, ncol=1,
                     name=f"yd_{tag}")
    s.update(pd=pd, cf=cf)

    ys = (ya, yb, yc, yd)
    gs = [_mm(h, p["w_gate"][i], mode="nn", out_dtype=bf16, name=f"gate{i}_{tag}") for i in range(4)]
    ps = [_mm(ys[i], p["w_branch"][i], mode="nn", out_dtype=bf16, name=f"branch{i}_{tag}") for i in range(4)]
    (merged,) = _rowwise(_f_merge, [(a, 0) for a in gs + ps], list(p["b_gate"]), [bf16], cw=512,
                         ncol=D_MODEL // 512, name=f"merge_{tag}")
    out = _mm(merged, p["w_out"], mode="nn", out_dtype=f32, add=x, name=f"out_{tag}")
    s.update(ys=ys, gs=gs, ps=ps, merged=merged)
    return out, s, nxt_0 + nxt_a + nxt_b


def _layer_bwd(dout, p, s, tag, prev_in):
    g = {}
    SH = D_MODEL // N_DEV
    W = BW
    u, h, small = s["u"], s["h"], s["small"]
    dob = dout.astype(bf16)
    dmerged = _mm(dob, p["w_out"], mode="nt", out_dtype=bf16, name=f"d_merged_{tag}")
    g["w_out"] = _mm(s["merged"], dob, mode="tn", out_dtype=bf16, name=f"dw_out_{tag}")
    r = _rowwise_bwd(_f_merge, [(a, 0) for a in s["gs"] + s["ps"]], list(p["b_gate"]), [dmerged], [bf16] * 8,
                     cw=512, ncol=D_MODEL // 512, name=f"d_merge_{tag}")
    dgs, dps, g["b_gate"] = r[0:4], r[4:8], r[8:12]
    g["w_gate"] = [_mm(h, dgs[i], mode="tn", out_dtype=bf16, name=f"dw_gate{i}_{tag}") for i in range(4)]
    g["w_branch"] = [_mm(s["ys"][i], dps[i], mode="tn", out_dtype=bf16, name=f"dw_branch{i}_{tag}")
                     for i in range(4)]
    dh = None
    for i in range(4):
        dh = _mm(dgs[i], p["w_gate"][i], mode="nt", out_dtype=f32, add=dh, tn=1024, name=f"dh_gate{i}_{tag}")
    dya, dyb, dyc, dyd = [_mm(dps[i], p["w_branch"][i], mode="nt", out_dtype=bf16, name=f"d_y{i}_{tag}")
                          for i in range(4)]
    du = {}

    do, du["ga"] = _rowwise_bwd(_f_ya, [(s["o"], 0), (u, _cb("ga", W))], [], [dya], [bf16, bf16], cw=W, ncol=1,
                                name=f"d_ya_{tag}")
    doT = _to_t(do, CW)
    delta = _flash_delta(doT, s["oT"], name=f"flash_delta_{tag}")
    kT = _to_t(u[:, MAIN_OFF["k"]:MAIN_OFF["k"] + W], min(FLASH_TK, u.shape[0]))
    own = [jnp.moveaxis(jnp.stack(g["w_gate"]).reshape(4, N_DEV, SH, D_MODEL), 1, 0).reshape(N_DEV, 4 * SH, D_MODEL),
           jnp.moveaxis(jnp.stack(g["w_branch"]).reshape(4, BW, N_DEV, SH), 2, 0).reshape(N_DEV, 4 * BW, SH),
           g["w_out"].reshape(N_DEV, SH, D_MODEL)]
    du["k"], du["v"], dck, dqT, dcq, *got_own = _flash_bwd(u, s["qT"], kT, do, doT, s["ck2"], s["lse2"], delta,
                                                           name=f"flash_bwd_{tag}", side=("exchange", own))
    du["q"] = _from_t(dqT).astype(bf16)
    dc = jnp.pad((dcq.reshape(FOX_HEADS, -1) + dck[:, :, 0]).T, ((0, 0), (0, LANES - FOX_HEADS)))

    dy_ssd, du["z"], g["ssm_norm_w"] = _rowwise_bwd(
        _f_yb, [(s["y_ssd"], 0), (u, _cb("z", W))], [p["ssm_norm_w"]], [dyb], [bf16, bf16], cw=W, ncol=1,
        name=f"d_yb_{tag}")
    dxbc_c, dsmall_dt, g["dtb_row"], g["alog_row"], g["dskip_row"] = _ssd_bwd(
        dy_ssd, s["xbc_c"], small, p["dtb_row"], p["alog_row"], p["dskip_row"], s["hst"], name=f"d_ssd_{tag}")
    (dpre_b,) = _rowwise_bwd(_f_silu, [(s["pre_b"], 0)], [], [dxbc_c], [bf16], cw=512, ncol=3,
                             name=f"d_ssm_silu_{tag}")
    du["xbc"], g["ssm_conv_w"], g["ssm_conv_b"] = _conv_bwd(dpre_b, u, _cb("xbc", 256), XBC_W, p["ssm_conv_w"],
                                                            name=f"d_ssm_conv_{tag}")

    du["scb"], dcv, du["gc"] = _rowwise_bwd(
        _f_yc, [(u, _cb("scb", 512)), (s["cv"], 0), (u, _cb("gc", 512))], [], [dyc], [bf16] * 3, cw=512, ncol=2,
        name=f"d_yc_{tag}")
    dpc, g["sc_conv_w"], g["sc_conv_b"] = _conv_bwd(dcv, s["pc"], 0, W, p["sc_conv_w"], name=f"d_sc_conv_{tag}")
    du["scc"], du["scx"] = _rowwise_bwd(_f_mul, [(u, _cb("scc", 512)), (u, _cb("scx", 512))], [], [dpc],
                                        [bf16, bf16], cw=512, ncol=2, name=f"d_sc_pre_{tag}")

    dcf, du["gd"], g["cf_ln_w"], g["cf_ln_b"] = _rowwise_bwd(
        _f_yd, [(s["cf"], 0), (u, _cb("gd", W))], [p["cf_ln_w"], p["cf_ln_b"]], [dyd], [bf16, bf16], cw=W, ncol=1,
        name=f"d_yd_{tag}")
    dpd, g["cf_conv_w"], g["cf_conv_b"] = _conv_bwd(dcf, s["pd"], 0, W, p["cf_conv_w"], name=f"d_cf_conv_{tag}")
    du["glua"], du["glug"] = _rowwise_bwd(_f_glu, [(u, _cb("glua", 512)), (u, _cb("glug", 512))], [], [dpd],
                                          [bf16, bf16], cw=512, ncol=2, name=f"d_cf_pre_{tag}")

    dsmall, g["fb_row"] = _fox_gate_bwd(dc, small, p["fb_row"], dsmall_dt, name=f"d_fox_gate_{tag}")
    du_main = jnp.concatenate([du[n] for n in MAIN_NAMES], axis=1)
    dsb = dsmall.astype(bf16)
    got_prev = []
    sides = (("exchange", prev_in[:1]), ("exchange", prev_in[1:])) if prev_in else (None, None)
    g["w_main"] = _mm(h, du_main, mode="tn", out_dtype=bf16, name=f"dw_main_{tag}", side=sides[0])
    if prev_in:
        g["w_main"], got_prev = g["w_main"][0], [g["w_main"][1]]
    g["w_small"] = _mm(h, dsb, mode="tn", out_dtype=bf16, name=f"dw_small_{tag}")
    dh = _mm(du_main, p["w_main"], mode="nt", out_dtype=f32, add=dh, tn=1024, name=f"dh_main_{tag}", side=sides[1])
    if prev_in:
        dh, got_prev = dh[0], got_prev + [dh[1]]
    dh = _mm(dsb, p["w_small"], mode="nt", out_dtype=f32, add=dh, tn=1024, name=f"dh_small_{tag}")
    dx, g["norm_w"] = _rowwise_bwd(_f_rms_res, [(s["x"], 0)], [p["norm_w"]], [dh, dout], [f32], cw=D_MODEL,
                                   ncol=1, name=f"d_rms_{tag}")
    return dx, g, got_own, got_prev


_BIG = ("w_in", "w_gate", "w_branch", "w_out")
_REPL = ("norm_w", "fg_bias", "ssm_conv_b", "dt_bias", "a_log", "d_skip", "ssm_norm_w", "sc_conv_b", "cf_conv_b",
         "cf_ln_w", "cf_ln_b", "final_norm_w")
_SHSM = ("ssm_conv_w", "sc_conv_w", "cf_conv_w", "b_gate")


def _to_rows(a, dtype):
    flat = a.astype(dtype).reshape(-1)
    n = flat.shape[0]
    per = 8 * LANES
    flat = jnp.pad(flat, (0, (-n) % per))
    return flat.reshape(-1, LANES)


def _pack(arrs, dtype):
    rows = [_to_rows(a, dtype) for a in arrs]
    n = sum(r.shape[0] for r in rows)
    rows.append(jnp.zeros(((-n) % 256, LANES), dtype))
    return jnp.concatenate(rows, axis=0)


def _unpack(blob, shapes):
    lead = blob.shape[:-2]
    out, r0 = [], 0
    for shp in shapes:
        n = 1
        for d in shp:
            n *= d
        nr = (-(-n // (8 * LANES))) * 8
        piece = blob[..., r0:r0 + nr, :].reshape(lead + (nr * LANES,))[..., :n]
        out.append(piece.reshape(lead + tuple(shp)))
        r0 += nr
    return out


def _main_cols(w):
    off, pieces = 0, {}
    for n, sz in zip(IN_NAMES, IN_SIZES):
        pieces[n] = w[..., off:off + sz]
        off += sz
    main = jnp.concatenate([pieces[n] for n in MAIN_NAMES], axis=-1)
    pad = jnp.zeros(w.shape[:-1] + (LANES - 24,), w.dtype)
    small = jnp.concatenate([pieces["f"], pieces["dt"], pad], axis=-1)
    return main, small


def _orig_cols(main, small):
    pieces = {}
    for n in MAIN_NAMES:
        sz = XBC_W if n == "xbc" else 1024
        pieces[n] = main[..., MAIN_OFF[n]:MAIN_OFF[n] + sz]
    pieces["f"] = small[..., 0:8]
    pieces["dt"] = small[..., 8:24]
    return jnp.concatenate([pieces[n] for n in IN_NAMES], axis=-1)


def kernel(*args):
    names = ["x", "norm_w", "w_in", "fg_bias", "ssm_conv_w", "ssm_conv_b", "dt_bias", "a_log", "d_skip",
             "ssm_norm_w", "sc_conv_w", "sc_conv_b", "cf_conv_w", "cf_conv_b", "cf_ln_w", "cf_ln_b", "w_gate",
             "b_gate", "w_branch", "w_out", "final_norm_w"]
    wnames = names[1:]
    a = dict(zip(names, args[:21]))
    target = args[21]
    mom = dict(zip(["m_" + n for n in wnames], args[22:42]))
    mom.update(zip(["v_" + n for n in wnames], args[42:62]))
    NL = a["w_in"].shape[0]
    x0 = a["x"][0]
    tgt = target[0]
    me = 4 * lax.axis_index("x") + 2 * lax.axis_index("y") + lax.axis_index("c")

    def rows2d(t):
        return t.reshape(-1, t.shape[-1])

    def shards(l):
        return [rows2d(a[n][l]).astype(bf16) for n in _BIG]

    (sm_all,) = _allgather([_pack([a[n] for n in _SHSM], f32)], name="gather_small")
    ssm_cw_g, sc_cw_g, cf_cw_g, b_gate_g = _unpack(sm_all, [a[n].shape for n in _SHSM])

    def cat_last(gathered):
        return jnp.moveaxis(gathered, 0, -2).reshape(gathered.shape[1:-1] + (-1,))

    ssm_cw, sc_cw, cf_cw, b_gate_full = cat_last(ssm_cw_g), cat_last(sc_cw_g), cat_last(cf_cw_g), cat_last(b_gate_g)

    HALF = D_MODEL // 2

    def layer_params(l, g_in):
        w_main, w_small = _main_cols(jnp.concatenate([cat_last(t) for t in g_in], axis=0))
        return dict(
            norm_w=a["norm_w"][l][None], w_main=w_main, w_small=w_small,
            fb_row=_row128(a["fg_bias"][l], 0), dtb_row=_row128(a["dt_bias"][l], HL),
            alog_row=_row128(a["a_log"][l], HL), dskip_row=_row128(a["d_skip"][l], HL),
            ssm_conv_w=ssm_cw[l], ssm_conv_b=a["ssm_conv_b"][l][None], ssm_norm_w=a["ssm_norm_w"][l][None],
            sc_conv_w=sc_cw[l], sc_conv_b=a["sc_conv_b"][l][None],
            cf_conv_w=cf_cw[l], cf_conv_b=a["cf_conv_b"][l][None],
            cf_ln_w=a["cf_ln_w"][l][None], cf_ln_b=a["cf_ln_b"][l][None],
            b_gate=[b_gate_full[l, i][None] for i in range(4)])

    def in_halves(t):
        return [t[..., :HALF, :], t[..., HALF:, :]]

    def in_slabs(g):
        d_in = _orig_cols(g["w_main"], g["w_small"])
        return in_halves(jnp.moveaxis(d_in.reshape(D_MODEL, N_DEV, -1), 1, 0))

    def in_parts(t):
        return [t[:IN_SPLIT[0]], t[IN_SPLIT[0]:IN_SPLIT[1]], t[IN_SPLIT[1]:]]

    g_in = _allgather(in_parts(shards(0)[0]), name="gather_in_l0")
    xl, saved, layers = x0, [], []
    for l in range(NL):
        layers.append(layer_params(l, g_in))
        nxt = in_parts(shards(l + 1)[0]) if l + 1 < NL else None
        xl, s, g_in = _layer_fwd(xl, layers[l], f"l{l}", shards(l)[1:], nxt)
        saved.append(s)
    loss_part, dx, d_fnw = _loss_head(xl, a["final_norm_w"][None], tgt, name="loss_head")

    grads, got_own, got_in = [None] * NL, [None] * NL, [None] * NL
    for l in reversed(range(NL)):
        prev = in_slabs(grads[l + 1]) if l + 1 < NL else None
        dx, grads[l], got_own[l], res = _layer_bwd(dx, layers[l], saved[l], f"l{l}", prev)
        if prev is not None:
            got_in[l + 1] = res
    got_in[0] = _exchange(in_slabs(grads[0]), name="exchange_in_l0")

    def stack(key):
        return jnp.stack([grads[l][key] for l in range(NL)])

    big_res = [[], [], [], []]
    for j, n in enumerate(_BIG):
        parts = ([h for l in range(NL) for h in got_in[l]] if j == 0 else [got_own[l][j - 1] for l in range(NL)])
        outs = _adamw(parts, rows2d(a[n]), rows2d(mom["m_" + n]), rows2d(mom["v_" + n]), name=f"adamw_{n}")
        for k in range(4):
            big_res[k].append(outs[k].reshape(a[n].shape))

    def vec(row, off, n):
        return row[0, off:off + n]

    small_grads = {
        "norm_w": jnp.concatenate([grads[l]["norm_w"] for l in range(NL)]),
        "fg_bias": jnp.stack([vec(grads[l]["fb_row"], 0, FOX_HEADS) for l in range(NL)]),
        "ssm_conv_b": jnp.concatenate([grads[l]["ssm_conv_b"] for l in range(NL)]),
        "dt_bias": jnp.stack([vec(grads[l]["dtb_row"], HL, SSM_HEADS) for l in range(NL)]),
        "a_log": jnp.stack([vec(grads[l]["alog_row"], HL, SSM_HEADS) for l in range(NL)]),
        "d_skip": jnp.stack([vec(grads[l]["dskip_row"], HL, SSM_HEADS) for l in range(NL)]),
        "ssm_norm_w": jnp.concatenate([grads[l]["ssm_norm_w"] for l in range(NL)]),
        "sc_conv_b": jnp.concatenate([grads[l]["sc_conv_b"] for l in range(NL)]),
        "cf_conv_b": jnp.concatenate([grads[l]["cf_conv_b"] for l in range(NL)]),
        "cf_ln_w": jnp.concatenate([grads[l]["cf_ln_w"] for l in range(NL)]),
        "cf_ln_b": jnp.concatenate([grads[l]["cf_ln_b"] for l in range(NL)]),
        "final_norm_w": d_fnw[0],
        "ssm_conv_w": stack("ssm_conv_w"), "sc_conv_w": stack("sc_conv_w"), "cf_conv_w": stack("cf_conv_w"),
        "b_gate": jnp.stack([jnp.concatenate(grads[l]["b_gate"]) for l in range(NL)]),
    }
    order = _REPL + _SHSM
    full_shapes = [small_grads[n].shape for n in order] + [(1,)]
    (sg_all,) = _allgather([_pack([small_grads[n] for n in order] + [loss_part.reshape(1)], f32)],
                           name="gather_small_grads")

    def mine(n, full):
        if n in _REPL:
            return full
        sz = a[n].shape[-1]
        return lax.dynamic_slice_in_dim(full, me * sz, sz, axis=full.ndim - 1)

    g_full = _unpack(_sum_slabs(sg_all, name="sum_small"), full_shapes)
    loss = g_full[-1][0]
    g_mine = {n: mine(n, gf).reshape(a[n].shape) for n, gf in zip(order, g_full[:-1])}
    gs1 = _pack([g_mine[n] for n in order], f32)
    ws1 = _pack([a[n] for n in order], f32)
    ms1 = _pack([mom["m_" + n] for n in order], f32)
    vs1 = _pack([mom["v_" + n] for n in order], f32)
    sm_out = _adamw([gs1[None]], ws1, ms1, vs1, name="adamw_small")
    sm_res = [_unpack(o, [a[n].shape for n in order]) for o in sm_out]

    res = []
    for k in range(4):
        d = dict(zip(_BIG, big_res[k]))
        d.update(zip(order, sm_res[k]))
        res.append(d)
    outs = [loss, dx[None]]
    for k in range(4):
        outs += [res[k][n] for n in wnames]
    return tuple(outs)
```
